```python
import math
import jax, jax.numpy as jnp
from jax import lax
import numpy as np

D_MODEL = 1024
BATCH = 2
SEQ = 8192
DEPTH = 2

MIX_WIDTH = D_MODEL
DIFF_HEADS = 4
DIFF_QK_DIM = 64
DIFF_V_DIM = 2 * DIFF_QK_DIM
A_QK = DIFF_HEADS * 2 * DIFF_QK_DIM
A_WIDTH = DIFF_HEADS * DIFF_V_DIM
Q_BLOCK = 128
B_WIDTH = MIX_WIDTH - A_WIDTH
CONV_GROUPS = 8
CONV_WIDTH = 3
EVEN_IN = 2 * A_QK + A_WIDTH + 3 * B_WIDTH
EVEN_SPLITS = (A_QK, 2 * A_QK, 2 * A_QK + A_WIDTH,
               2 * A_QK + A_WIDTH + B_WIDTH, 2 * A_QK + A_WIDTH + 2 * B_WIDTH)
CHUNK = 128
SGU_WIDTH = MIX_WIDTH
SGU_GROUPS = 8
SGU_GROUP_DIM = SGU_WIDTH // SGU_GROUPS
ODD_IN = 2 * SGU_WIDTH
REL_BUCKETS = 32
REL_MAX_DIST = 128
FFN_HIDDEN = -(-8 * D_MODEL // (3 * 256)) * 256

N_EVEN = (DEPTH + 1) // 2
N_ODD = DEPTH // 2
RMS_EPS = 1e-6

kernel_name = "hybrid_diffattn_shortconv_sgu_block"


def rms_norm(x, g, eps=RMS_EPS):
    xf = x.astype(jnp.float32)
    y = xf * lax.rsqrt(jnp.mean(xf * xf, axis=-1, keepdims=True) + eps)
    return (y * g.astype(jnp.float32)).astype(x.dtype)


def layer_norm(x, g, b, eps=1e-5):
    xf = x.astype(jnp.float32)
    mu = jnp.mean(xf, axis=-1, keepdims=True)
    xc = xf - mu
    y = xc * lax.rsqrt(jnp.mean(xc * xc, axis=-1, keepdims=True) + eps)
    return (y * g.astype(jnp.float32) + b.astype(jnp.float32)).astype(x.dtype)


def t5_bucket(q_pos, k_pos):
    n = jnp.maximum(q_pos[:, None] - k_pos[None, :], 0)
    max_exact = REL_BUCKETS // 2
    nf = jnp.maximum(n, 1).astype(jnp.float32)
    large = max_exact + (jnp.log(nf / max_exact) / math.log(REL_MAX_DIST / max_exact)
                         * (REL_BUCKETS - max_exact)).astype(jnp.int32)
    large = jnp.minimum(large, REL_BUCKETS - 1)
    return jnp.where(n < max_exact, n, large)


def diff_attention(q, k, v, rel_bias, lam, subln_g, lambda_init):
    bsz, s = q.shape[0], q.shape[1]
    nblk = s // Q_BLOCK
    lam = lam.astype(jnp.float32)
    lam_full = (jnp.exp(jnp.sum(lam[0] * lam[1])) - jnp.exp(jnp.sum(lam[2] * lam[3]))
                + lambda_init)
    scale = DIFF_QK_DIM ** -0.5
    k_pos = jnp.arange(s, dtype=jnp.int32)
    table = rel_bias.astype(jnp.float32)
    qb = q.reshape(bsz, nblk, Q_BLOCK, DIFF_HEADS, 2, DIFF_QK_DIM)
    qb = jnp.moveaxis(qb, 1, 0)
    starts = jnp.arange(nblk, dtype=jnp.int32) * Q_BLOCK

    def block(args):
        q_blk, start = args
        q_pos = start + jnp.arange(Q_BLOCK, dtype=jnp.int32)
        logits = jnp.einsum('bqhmd,bkhmd->bhmqk', q_blk, k).astype(jnp.float32) * scale
        bias = jnp.transpose(table[t5_bucket(q_pos, k_pos)], (2, 0, 1))
        logits = logits + bias[:, None]
        causal = q_pos[:, None] >= k_pos[None, :]
        logits = jnp.where(causal, logits, -jnp.inf)
        p = jax.nn.softmax(logits, axis=-1)
        attn = p[:, :, 0] - lam_full * p[:, :, 1]
        return jnp.einsum('bhqk,bkhe->bqhe', attn.astype(v.dtype), v)

    o = lax.map(block, (qb, starts))
    o = jnp.moveaxis(o, 0, 1).reshape(bsz, s, DIFF_HEADS, DIFF_V_DIM)
    o = rms_norm(o, subln_g, eps=1e-5) * (1.0 - lambda_init)
    return o.reshape(bsz, s, A_WIDTH)


def short_gated_conv(b_gate, c_gate, h, conv_w):
    z = c_gate * h
    y = lax.conv_general_dilated(
        z, conv_w[:, None, :].astype(z.dtype), window_strides=(1,),
        padding=((CONV_WIDTH - 1, 0),), dimension_numbers=('NWC', 'WIO', 'NWC'),
        feature_group_count=z.shape[-1])
    return b_gate * y


def spatial_gating(xn, w_in, ln_g, ln_b, sgu_w, sgu_b, w_out):
    z = jax.nn.gelu(xn @ w_in, approximate=False)
    u, v = jnp.split(z, 2, axis=-1)
    v = layer_norm(v, ln_g, ln_b)
    bsz, s = v.shape[0], v.shape[1]
    vc = v.reshape(bsz, s // CHUNK, CHUNK, SGU_GROUPS, SGU_GROUP_DIM)
    mask = jnp.tril(jnp.ones((CHUNK, CHUNK), dtype=bool))
    ws = jnp.where(mask[None], sgu_w, jnp.zeros_like(sgu_w))
    mixed = jnp.einsum('gts,bnsgc->bntgc', ws.astype(v.dtype), vc)
    mixed = mixed + jnp.transpose(sgu_b)[:, :, None].astype(v.dtype)
    mixed = mixed.reshape(bsz, s, SGU_WIDTH)
    return (u * mixed) @ w_out


def setup_inputs(seed: int = 0) -> dict:
    key = jax.random.key(seed)
    ks = jax.random.split(key, 20)
    n = jax.random.normal
    f32 = jnp.float32
    return {
        "x": n(ks[0], (BATCH, SEQ, D_MODEL), f32),
        "rel_bias": 0.5 * n(ks[1], (REL_BUCKETS, DIFF_HEADS), f32),
        "w_in_even": n(ks[2], (N_EVEN, D_MODEL, EVEN_IN), f32) * D_MODEL ** -0.5,
        "diff_lambda": 0.1 * n(ks[3], (N_EVEN, 4, DIFF_QK_DIM), f32),
        "diff_subln_g": 1.0 + 0.02 * n(ks[4], (N_EVEN, DIFF_V_DIM), f32),
        "conv_w": n(ks[5], (N_EVEN, CONV_WIDTH, B_WIDTH), f32) * CONV_WIDTH ** -0.5,
        "w_out_even": n(ks[6], (N_EVEN, MIX_WIDTH, D_MODEL), f32) * MIX_WIDTH ** -0.5,
        "w_in_odd": n(ks[7], (N_ODD, D_MODEL, ODD_IN), f32) * D_MODEL ** -0.5,
        "sgu_ln_g": 1.0 + 0.02 * n(ks[8], (N_ODD, SGU_WIDTH), f32),
        "sgu_ln_b": 0.02 * n(ks[9], (N_ODD, SGU_WIDTH), f32),
        "sgu_w": n(ks[10], (N_ODD, SGU_GROUPS, CHUNK, CHUNK), f32) * CHUNK ** -0.5,
        "sgu_b": 1.0 + 0.02 * n(ks[11], (N_ODD, SGU_GROUPS, CHUNK), f32),
        "w_out_odd": n(ks[12], (N_ODD, SGU_WIDTH, D_MODEL), f32) * SGU_WIDTH ** -0.5,
        "norm_g": 1.0 + 0.02 * n(ks[13], (DEPTH, 4, D_MODEL), f32),
        "w_gate": n(ks[14], (DEPTH, D_MODEL, FFN_HIDDEN), f32) * D_MODEL ** -0.5,
        "w_up": n(ks[15], (DEPTH, D_MODEL, FFN_HIDDEN), f32) * D_MODEL ** -0.5,
        "w_down": n(ks[16], (DEPTH, FFN_HIDDEN, D_MODEL), f32) * FFN_HIDDEN ** -0.5,
    }


def reference(x, rel_bias, w_in_even, diff_lambda, diff_subln_g, conv_w, w_out_even,
              w_in_odd, sgu_ln_g, sgu_ln_b, sgu_w, sgu_b, w_out_odd, norm_g,
              w_gate, w_up, w_down):
    bsz, s = x.shape[0], x.shape[1]
    for i in range(DEPTH):
        j = i // 2
        h = rms_norm(x, norm_g[i, 0])
        if i % 2 == 0:
            lambda_init = 0.8 - 0.6 * math.exp(-0.3 * i)
            p = h @ w_in_even[j]
            q, k, v, b_gate, c_gate, hc = jnp.split(p, EVEN_SPLITS, axis=-1)
            q = q.reshape(bsz, s, DIFF_HEADS, 2, DIFF_QK_DIM)
            k = k.reshape(bsz, s, DIFF_HEADS, 2, DIFF_QK_DIM)
            v = v.reshape(bsz, s, DIFF_HEADS, DIFF_V_DIM)
            a_out = diff_attention(q, k, v, rel_bias, diff_lambda[j], diff_subln_g[j],
                                   lambda_init)
            b_out = short_gated_conv(b_gate, c_gate, hc, conv_w[j])
            mix = jnp.concatenate([a_out, b_out], axis=-1) @ w_out_even[j]
        else:
            mix = spatial_gating(h, w_in_odd[j], sgu_ln_g[j], sgu_ln_b[j], sgu_w[j],
                                 sgu_b[j], w_out_odd[j])
        x = x + rms_norm(mix, norm_g[i, 1])
        h = rms_norm(x, norm_g[i, 2])
        f = (jax.nn.silu(h @ w_gate[i]) * (h @ w_up[i])) @ w_down[i]
        x = x + rms_norm(f, norm_g[i, 3])
    return x
```

```python
import functools
import math

import jax
import jax.numpy as jnp
from jax import lax
from jax.experimental import pallas as pl
from jax.experimental.pallas import tpu as pltpu

F32 = jnp.float32
BF16 = jnp.bfloat16

DIFF_HEADS = 4
DIFF_QK_DIM = 64
HEAD_WIDTH = 2 * DIFF_QK_DIM
A_QK = DIFF_HEADS * HEAD_WIDTH
CONV_WIDTH = 3
SGU_GROUPS = 8
CHUNK = 128
REL_BUCKETS = 32
REL_MAX_DIST = 128
RMS_EPS = 1e-6
SUBLN_EPS = 1e-5
LN_EPS = 1e-5
MASK_VALUE = -1e30

SUBLANES = 8
TOKEN_TILE = 512
ATTN_BLOCK = 256
VMEM_LIMIT_BYTES = 56 * 1024 * 1024


def _rms(x, g, eps=RMS_EPS):
    return x * lax.rsqrt(jnp.mean(x * x, axis=-1, keepdims=True) + eps) * g


def _const_spec(shape):
    return pl.BlockSpec(shape, lambda *_: (0,) * len(shape), pipeline_mode=pl.Buffered(1))


def _even_in_kernel(x_ref, g_ref, w_ref, wvt_ref, cw_ref,
                    q1_ref, q2_ref, k_ref, vt_ref, bo_ref, carry_ref, *, tiles_per_seq, attn_block):
    tm = x_ref.shape[0]
    bw = bo_ref.shape[1]

    @pl.when(pl.program_id(0) % tiles_per_seq == 0)
    def _():
        carry_ref[...] = jnp.zeros_like(carry_ref)

    h = _rms(x_ref[...], g_ref[...]).astype(BF16)

    q = jnp.dot(h, w_ref[:, 0:A_QK], preferred_element_type=F32) * (DIFF_QK_DIM ** -0.5)
    first_map = (lax.broadcasted_iota(jnp.int32, q.shape, 1) % HEAD_WIDTH) < DIFF_QK_DIM
    q1_ref[...] = jnp.where(first_map, q, 0.0).astype(BF16)
    q2_ref[...] = jnp.where(first_map, 0.0, q).astype(BF16)
    k_ref[...] = jnp.dot(h, w_ref[:, A_QK:2 * A_QK], preferred_element_type=F32).astype(BF16)

    vt = lax.dot_general(wvt_ref[...], h, (((1,), (1,)), ((), ())), preferred_element_type=F32)
    for c in range(tm // attn_block):
        vt_ref[c] = vt[:, c * attn_block:(c + 1) * attn_block].astype(BF16)

    c0 = 3 * A_QK
    b_gate = jnp.dot(h, w_ref[:, c0:c0 + bw], preferred_element_type=F32)
    c_gate = jnp.dot(h, w_ref[:, c0 + bw:c0 + 2 * bw], preferred_element_type=F32)
    hc = jnp.dot(h, w_ref[:, c0 + 2 * bw:c0 + 3 * bw], preferred_element_type=F32)
    z = c_gate * hc
    row = lax.broadcasted_iota(jnp.int32, z.shape, 0)
    prev = carry_ref[...]
    zm1 = jnp.where(row == 0, prev[SUBLANES - 1:SUBLANES], pltpu.roll(z, 1, 0))
    zm2 = jnp.where(row == 0, prev[SUBLANES - 2:SUBLANES - 1],
                    jnp.where(row == 1, prev[SUBLANES - 1:SUBLANES], pltpu.roll(z, 2, 0)))
    cw = cw_ref[...]
    y = cw[0:1] * zm2 + cw[1:2] * zm1 + cw[2:3] * z
    bo_ref[...] = (b_gate * y).astype(BF16)
    carry_ref[...] = z[tm - SUBLANES:, :]


def _even_in(x2d, g, w_bf, wvt_bf, conv_w, seq):
    n, d = x2d.shape
    tm = TOKEN_TILE
    t = ATTN_BLOCK
    bw = conv_w.shape[1]
    kern = functools.partial(_even_in_kernel, tiles_per_seq=seq // tm, attn_block=t)
    tok = lambda w: pl.BlockSpec((tm, w), lambda i: (i, 0))
    return pl.pallas_call(
        kern,
        grid=(n // tm,),
        in_specs=[tok(d), _const_spec((1, d)), _const_spec(w_bf.shape), _const_spec(wvt_bf.shape),
                  _const_spec(conv_w.shape)],
        out_specs=[tok(A_QK), tok(A_QK), tok(A_QK),
                   pl.BlockSpec((tm // t, A_QK, t), lambda i: (i, 0, 0)),
                   tok(bw)],
        out_shape=[jax.ShapeDtypeStruct((n, A_QK), BF16)] * 3
        + [jax.ShapeDtypeStruct((n // t, A_QK, t), BF16), jax.ShapeDtypeStruct((n, bw), BF16)],
        scratch_shapes=[pltpu.VMEM((SUBLANES, bw), F32)],
        compiler_params=pltpu.CompilerParams(dimension_semantics=("arbitrary",),
                                             vmem_limit_bytes=VMEM_LIMIT_BYTES),
        name="even_in_proj",
    )(x2d, g, w_bf, wvt_bf, conv_w)


def _attn_kernel(lam_ref, q1_ref, q2_ref, k_ref, vt_ref, bias_ref, cfar_ref, g_ref, o_ref,
                 m_ref, l_ref, acc_ref, *, lambda_init):
    i = pl.program_id(2)
    m_ref[...] = jnp.full_like(m_ref, MASK_VALUE)
    l_ref[...] = jnp.zeros_like(l_ref)
    acc_ref[...] = jnp.zeros_like(acc_ref)
    qs = (q1_ref[...], q2_ref[...])

    def update(j, bias):
        kb = k_ref[j]
        vb = vt_ref[j]
        for mp in range(2):
            s = lax.dot_general(kb, qs[mp], (((1,), (1,)), ((), ())), preferred_element_type=F32)
            m_old = m_ref[mp:mp + 1, :]
            if bias is None:
                m_new = jnp.maximum(m_old, jnp.max(s, axis=0, keepdims=True) + cfar_ref[...])
                e = jnp.exp(s - (m_new - cfar_ref[...]))
            else:
                s = s + bias
                m_new = jnp.maximum(m_old, jnp.max(s, axis=0, keepdims=True))
                e = jnp.exp(s - m_new)
            alpha = jnp.exp(m_old - m_new)
            l_ref[mp:mp + 1, :] = alpha * l_ref[mp:mp + 1, :] + jnp.sum(e, axis=0, keepdims=True)
            m_ref[mp:mp + 1, :] = m_new
            acc_ref[mp] = alpha * acc_ref[mp] + jnp.dot(vb, e.astype(BF16), preferred_element_type=F32)

    def far_body(j, carry):
        update(j, None)
        return carry

    lax.fori_loop(0, jnp.maximum(i - 1, 0), far_body, 0)

    @pl.when(i >= 1)
    def _():
        update(i - 1, bias_ref[1])

    update(i, bias_ref[0])

    lam = lam_ref[...]
    lam_full = (jnp.exp(jnp.sum(lam[0:1] * lam[1:2], axis=1, keepdims=True))
                - jnp.exp(jnp.sum(lam[2:3] * lam[3:4], axis=1, keepdims=True)) + lambda_init)
    o = acc_ref[0] * (1.0 / l_ref[0:1, :]) - lam_full * (acc_ref[1] * (1.0 / l_ref[1:2, :]))
    y = o * lax.rsqrt(jnp.mean(o * o, axis=0, keepdims=True) + SUBLN_EPS)
    o_ref[...] = (y.T * (g_ref[...] * (1.0 - lambda_init))).astype(BF16)


def _diff_attention(lam, q1, q2, k4, vt4, bias_tiles, cfar, subln_g, lambda_init):
    bsz, nblk, t, _ = k4.shape
    seq = nblk * t
    hw = HEAD_WIDTH
    kern = functools.partial(_attn_kernel, lambda_init=lambda_init)
    qspec = pl.BlockSpec((None, t, hw), lambda b, h, i: (b, i, h))
    return pl.pallas_call(
        kern,
        grid=(bsz, DIFF_HEADS, nblk),
        in_specs=[
            _const_spec(lam.shape),
            qspec, qspec,
            pl.BlockSpec((None, nblk, t, hw), lambda b, h, i: (b, 0, 0, h)),
            pl.BlockSpec((None, nblk, hw, t), lambda b, h, i: (b, 0, h, 0)),
            pl.BlockSpec((None, 2, t, t), lambda b, h, i: (h, 0, 0, 0)),
            pl.BlockSpec((None, 1, t), lambda b, h, i: (h, 0, 0)),
            _const_spec(subln_g.shape),
        ],
        out_specs=qspec,
        out_shape=jax.ShapeDtypeStruct((bsz, seq, A_QK), BF16),
        scratch_shapes=[pltpu.VMEM((SUBLANES, t), F32), pltpu.VMEM((SUBLANES, t), F32),
                        pltpu.VMEM((2, hw, t), F32)],
        compiler_params=pltpu.CompilerParams(
            dimension_semantics=("arbitrary", "arbitrary", "arbitrary"),
            vmem_limit_bytes=VMEM_LIMIT_BYTES),
        name="diff_attention",
    )(lam, q1, q2, k4, vt4, bias_tiles, cfar, subln_g)


def _residual_ffn(x, mix, ng_ref, wg_ref, wu_ref, wd_ref):
    x1 = x + _rms(mix, ng_ref[1:2, :])
    h2 = _rms(x1, ng_ref[2:3, :]).astype(BF16)
    gate = jnp.dot(h2, wg_ref[...], preferred_element_type=F32)
    up = jnp.dot(h2, wu_ref[...], preferred_element_type=F32)
    act = (gate * jax.nn.sigmoid(gate) * up).astype(BF16)
    f = jnp.dot(act, wd_ref[...], preferred_element_type=F32)
    return x1 + _rms(f, ng_ref[3:4, :])


def _even_out_kernel(x_ref, a_ref, bo_ref, wo_ref, ng_ref, wg_ref, wu_ref, wd_ref, o_ref):
    aw = a_ref.shape[1]
    mix = (jnp.dot(a_ref[...], wo_ref[0:aw, :], preferred_element_type=F32)
           + jnp.dot(bo_ref[...], wo_ref[aw:, :], preferred_element_type=F32))
    o_ref[...] = _residual_ffn(x_ref[...], mix, ng_ref, wg_ref, wu_ref, wd_ref)


def _even_out(x2d, a2d, bo2d, wo_bf, ng, wg_bf, wu_bf, wd_bf):
    n, d = x2d.shape
    tm = TOKEN_TILE
    tok = lambda w: pl.BlockSpec((tm, w), lambda i: (i, 0))
    return pl.pallas_call(
        _even_out_kernel,
        grid=(n // tm,),
        in_specs=[tok(d), tok(a2d.shape[1]), tok(bo2d.shape[1]), _const_spec(wo_bf.shape),
                  _const_spec(ng.shape), _const_spec(wg_bf.shape), _const_spec(wu_bf.shape),
                  _const_spec(wd_bf.shape)],
        out_specs=tok(d),
        out_shape=jax.ShapeDtypeStruct((n, d), F32),
        compiler_params=pltpu.CompilerParams(dimension_semantics=("arbitrary",),
                                             vmem_limit_bytes=VMEM_LIMIT_BYTES),
        name="even_out_ffn",
    )(x2d, a2d, bo2d, wo_bf, ng, wg_bf, wu_bf, wd_bf)


def _odd_kernel(x_ref, wi_ref, lng_ref, lnb_ref, ws_ref, sb_ref, wo_ref, ng_ref,
                wg_ref, wu_ref, wd_ref, o_ref):
    x = x_ref[...]
    tm = x.shape[0]
    sw = wo_ref.shape[0]
    gd = sw // SGU_GROUPS
    h = _rms(x, ng_ref[0:1, :]).astype(BF16)

    def gelu(z):
        return 0.5 * z * (1.0 + lax.erf(z * math.sqrt(0.5)))

    u = gelu(jnp.dot(h, wi_ref[:, 0:sw], preferred_element_type=F32))
    v = gelu(jnp.dot(h, wi_ref[:, sw:], preferred_element_type=F32))
    mu = jnp.mean(v, axis=-1, keepdims=True)
    vc = v - mu
    v = (vc * lax.rsqrt(jnp.mean(vc * vc, axis=-1, keepdims=True) + LN_EPS) * lng_ref[...]
         + lnb_ref[...]).astype(BF16)

    nch = tm // CHUNK
    tiles = [[None] * SGU_GROUPS for _ in range(nch)]
    for g in range(SGU_GROUPS):
        rhs = jnp.concatenate([v[n * CHUNK:(n + 1) * CHUNK, g * gd:(g + 1) * gd] for n in range(nch)],
                              axis=1)
        mixed = jnp.dot(ws_ref[g], rhs, preferred_element_type=F32) + sb_ref[g]
        for n in range(nch):
            tiles[n][g] = (u[n * CHUNK:(n + 1) * CHUNK, g * gd:(g + 1) * gd]
                           * mixed[:, n * gd:(n + 1) * gd])
    gated = jnp.concatenate([jnp.concatenate(r, axis=1) for r in tiles], axis=0).astype(BF16)
    mix = jnp.dot(gated, wo_ref[...], preferred_element_type=F32)
    o_ref[...] = _residual_ffn(x, mix, ng_ref, wg_ref, wu_ref, wd_ref)


def _odd_layer(x2d, wi_bf, ln_g, ln_b, ws_bf, sb, wo_bf, ng, wg_bf, wu_bf, wd_bf):
    n, d = x2d.shape
    tm = TOKEN_TILE
    tok = pl.BlockSpec((tm, d), lambda i: (i, 0))
    consts = (wi_bf, ln_g, ln_b, ws_bf, sb, wo_bf, ng, wg_bf, wu_bf, wd_bf)
    return pl.pallas_call(
        _odd_kernel,
        grid=(n // tm,),
        in_specs=[tok] + [_const_spec(a.shape) for a in consts],
        out_specs=tok,
        out_shape=jax.ShapeDtypeStruct((n, d), F32),
        compiler_params=pltpu.CompilerParams(dimension_semantics=("arbitrary",),
                                             vmem_limit_bytes=VMEM_LIMIT_BYTES),
        name="odd_sgu_ffn",
    )(x2d, *consts)


def _t5_bucket_of_distance(n):
    max_exact = REL_BUCKETS // 2
    nf = jnp.maximum(n, 1).astype(F32)
    large = max_exact + (jnp.log(nf / max_exact) / math.log(REL_MAX_DIST / max_exact)
                         * (REL_BUCKETS - max_exact)).astype(jnp.int32)
    large = jnp.minimum(large, REL_BUCKETS - 1)
    return jnp.where(n < max_exact, n, large)


def _bias_tiles(rel_bias, t):
    assert t >= REL_MAX_DIST
    by_dist = rel_bias.astype(F32)[_t5_bucket_of_distance(jnp.arange(2 * t, dtype=jnp.int32))]
    key = jnp.arange(t, dtype=jnp.int32)[:, None]
    qry = jnp.arange(t, dtype=jnp.int32)[None, :]
    diag = jnp.where((qry >= key)[..., None], by_dist[jnp.maximum(qry - key, 0)], MASK_VALUE)
    prev = by_dist[t + qry - key]
    tiles = jnp.transpose(jnp.stack([diag, prev]), (3, 0, 1, 2))
    cfar = jnp.broadcast_to(by_dist[2 * t - 1][:, None, None], (rel_bias.shape[1], 1, t))
    return tiles, cfar


def kernel(x, rel_bias, w_in_even, diff_lambda, diff_subln_g, conv_w, w_out_even, w_in_odd,
           sgu_ln_g, sgu_ln_b, sgu_w, sgu_b, w_out_odd, norm_g, w_gate, w_up, w_down):
    bsz, seq, d = x.shape
    depth = norm_g.shape[0]
    t = ATTN_BLOCK
    assert seq % TOKEN_TILE == 0 and TOKEN_TILE % t == 0 and TOKEN_TILE % CHUNK == 0
    x2d = x.reshape(bsz * seq, d)
    bias_tiles, cfar = _bias_tiles(rel_bias, t)
    tril = jnp.tril(jnp.ones((CHUNK, CHUNK), dtype=bool))
    for i in range(depth):
        j = i // 2
        ng = norm_g[i]
        wg, wu, wd = w_gate[i].astype(BF16), w_up[i].astype(BF16), w_down[i].astype(BF16)
        if i % 2 == 0:
            lambda_init = 0.8 - 0.6 * math.exp(-0.3 * i)
            w_in = w_in_even[j]
            q1, q2, k, vt, bo = _even_in(
                x2d, ng[0:1], w_in.astype(BF16), w_in[:, 2 * A_QK:3 * A_QK].T.astype(BF16), conv_w[j], seq)
            a = _diff_attention(
                diff_lambda[j], q1.reshape(bsz, seq, A_QK), q2.reshape(bsz, seq, A_QK),
                k.reshape(bsz, seq // t, t, A_QK), vt.reshape(bsz, seq // t, A_QK, t),
                bias_tiles, cfar, diff_subln_g[j][None, :], lambda_init)
            x2d = _even_out(x2d, a.reshape(bsz * seq, A_QK), bo, w_out_even[j].astype(BF16), ng, wg, wu, wd)
        else:
            ws = jnp.where(tril[None], sgu_w[j], 0.0).astype(BF16)
            x2d = _odd_layer(x2d, w_in_odd[j].astype(BF16), sgu_ln_g[j][None, :], sgu_ln_b[j][None, :],
                             ws, sgu_b[j][:, :, None], w_out_odd[j].astype(BF16), ng, wg, wu, wd)
    return x2d.reshape(bsz, seq, d)
```

```python
import functools
import math

import jax
import jax.numpy as jnp
from jax import lax
from jax.experimental import pallas as pl
from jax.experimental.pallas import tpu as pltpu

F32 = jnp.float32
BF16 = jnp.bfloat16

DIFF_HEADS = 4
DIFF_QK_DIM = 64
HEAD_WIDTH = 2 * DIFF_QK_DIM
A_QK = DIFF_HEADS * HEAD_WIDTH
CONV_WIDTH = 3
SGU_GROUPS = 8
CHUNK = 128
REL_BUCKETS = 32
REL_MAX_DIST = 128
RMS_EPS = 1e-6
SUBLN_EPS = 1e-5
LN_EPS = 1e-5
MASK_VALUE = -1e30

SUBLANES = 8
TOKEN_TILE = 512
ATTN_BLOCK = 1024
BIAS_TILE = 256
VMEM_LIMIT_BYTES = 56 * 1024 * 1024

_NT = (((1,), (1,)), ((), ()))


def _rms(x, g, eps=RMS_EPS):
    return x * lax.rsqrt(jnp.mean(x * x, axis=-1, keepdims=True) + eps) * g


def _const_spec(shape):
    return pl.BlockSpec(shape, lambda *_: (0,) * len(shape), pipeline_mode=pl.Buffered(1))


def _even_in_kernel(x_ref, g_ref, w_ref, wvt_ref, cw_ref,
                    q1_ref, q2_ref, k_ref, vt_ref, bo_ref, carry_ref, *, tiles_per_seq):
    tm = x_ref.shape[0]
    bw = bo_ref.shape[1]

    @pl.when(pl.program_id(0) % tiles_per_seq == 0)
    def _():
        carry_ref[...] = jnp.zeros_like(carry_ref)

    h = _rms(x_ref[...], g_ref[...]).astype(BF16)

    q = jnp.dot(h, w_ref[:, 0:A_QK], preferred_element_type=F32) * (DIFF_QK_DIM ** -0.5)
    first_map = (lax.broadcasted_iota(jnp.int32, q.shape, 1) % HEAD_WIDTH) < DIFF_QK_DIM
    q1_ref[...] = jnp.where(first_map, q, 0.0).astype(BF16)
    q2_ref[...] = jnp.where(first_map, 0.0, q).astype(BF16)
    k_ref[...] = jnp.dot(h, w_ref[:, A_QK:2 * A_QK], preferred_element_type=F32).astype(BF16)

    vt_ref[...] = lax.dot_general(wvt_ref[...], h, _NT, preferred_element_type=F32).astype(BF16)

    c0 = 3 * A_QK
    b_gate = jnp.dot(h, w_ref[:, c0:c0 + bw], preferred_element_type=F32)
    c_gate = jnp.dot(h, w_ref[:, c0 + bw:c0 + 2 * bw], preferred_element_type=F32)
    hc = jnp.dot(h, w_ref[:, c0 + 2 * bw:c0 + 3 * bw], preferred_element_type=F32)
    z = c_gate * hc
    row = lax.broadcasted_iota(jnp.int32, z.shape, 0)
    prev = carry_ref[...]
    zm1 = jnp.where(row == 0, prev[SUBLANES - 1:SUBLANES], pltpu.roll(z, 1, 0))
    zm2 = jnp.where(row == 0, prev[SUBLANES - 2:SUBLANES - 1],
                    jnp.where(row == 1, prev[SUBLANES - 1:SUBLANES], pltpu.roll(z, 2, 0)))
    cw = cw_ref[...]
    y = cw[0:1] * zm2 + cw[1:2] * zm1 + cw[2:3] * z
    bo_ref[...] = (b_gate * y).astype(BF16)
    carry_ref[...] = z[tm - SUBLANES:, :]


def _even_in(x2d, g, w_bf, wvt_bf, conv_w, seq):
    n, d = x2d.shape
    tm = TOKEN_TILE
    t = ATTN_BLOCK
    per_blk = t // tm
    bw = conv_w.shape[1]
    kern = functools.partial(_even_in_kernel, tiles_per_seq=seq // tm)
    tok = lambda w: pl.BlockSpec((tm, w), lambda i: (i, 0))
    return pl.pallas_call(
        kern,
        grid=(n // tm,),
        in_specs=[tok(d), _const_spec((1, d)), _const_spec(w_bf.shape), _const_spec(wvt_bf.shape),
                  _const_spec(conv_w.shape)],
        out_specs=[tok(A_QK), tok(A_QK), tok(A_QK),
                   pl.BlockSpec((None, A_QK, tm), lambda i: (i // per_blk, 0, i % per_blk)),
                   tok(bw)],
        out_shape=[jax.ShapeDtypeStruct((n, A_QK), BF16)] * 3
        + [jax.ShapeDtypeStruct((n // t, A_QK, t), BF16), jax.ShapeDtypeStruct((n, bw), BF16)],
        scratch_shapes=[pltpu.VMEM((SUBLANES, bw), F32)],
        compiler_params=pltpu.CompilerParams(dimension_semantics=("arbitrary",),
                                             vmem_limit_bytes=VMEM_LIMIT_BYTES),
        name="even_in_proj",
    )(x2d, g, w_bf, wvt_bf, conv_w)


_DIAG, _PREV = 0, 1


def _patched(s, tile_ref, patches):
    sb = tile_ref.shape[-1]
    for r, c, idx in patches:
        band = s[r:r + sb, :]
        cols = [band[:, :c]] if c else []
        cols.append(band[:, c:c + sb] + tile_ref[idx])
        if c + sb < s.shape[1]:
            cols.append(band[:, c + sb:])
        rows = [s[:r]] if r else []
        rows.append(jnp.concatenate(cols, axis=1) if len(cols) > 1 else cols[0])
        if r + sb < s.shape[0]:
            rows.append(s[r + sb:])
        s = jnp.concatenate(rows, axis=0) if len(rows) > 1 else rows[0]
    return s


def _attn_kernel(lam_ref, q1_ref, q2_ref, k_ref, vt_ref, dt_ref, cfar_ref, g_ref, o_ref,
                 m_ref, l_ref, acc_ref, *, lambda_init):
    i = pl.program_id(2)
    t = q1_ref.shape[0]
    sb = dt_ref.shape[-1]
    m_ref[...] = jnp.full_like(m_ref, MASK_VALUE)
    l_ref[...] = jnp.zeros_like(l_ref)
    acc_ref[...] = jnp.zeros_like(acc_ref)
    q_refs = (q1_ref, q2_ref)
    cfar = cfar_ref[...]

    def step(j, pieces):
        for mp in range(2):
            m_old = m_ref[mp:mp + 1, :]
            scores, smax = [], None
            for k_lo, k_hi, q_lo, patches in pieces:
                s = lax.dot_general(k_ref[j, k_lo:k_hi, :], q_refs[mp][q_lo:, :], _NT,
                                    preferred_element_type=F32)
                s = _patched(s, dt_ref, patches)
                scores.append(s)
                pm = jnp.max(s, axis=0, keepdims=True)
                if q_lo:
                    pm = jnp.concatenate([jnp.full((1, q_lo), MASK_VALUE, F32), pm], axis=1)
                smax = pm if smax is None else jnp.maximum(smax, pm)
            m_new = jnp.maximum(m_old, smax + cfar)
            shift = m_new - cfar
            alpha = jnp.exp(m_old - m_new)
            lsum = None
            for n, ((k_lo, k_hi, q_lo, _), s) in enumerate(zip(pieces, scores)):
                e = jnp.exp(s - shift[:, q_lo:])
                ps = jnp.sum(e, axis=0, keepdims=True)
                if q_lo:
                    ps = jnp.concatenate([jnp.zeros((1, q_lo), F32), ps], axis=1)
                lsum = ps if lsum is None else lsum + ps
                pv = jnp.dot(vt_ref[j, :, k_lo:k_hi], e.astype(BF16), preferred_element_type=F32)
                if len(pieces) == 1:
                    acc_ref[mp] = alpha * acc_ref[mp] + pv
                else:
                    if n == 0:
                        acc_ref[mp] = alpha * acc_ref[mp]
                    acc_ref[mp, :, q_lo:] += pv
            l_ref[mp:mp + 1, :] = alpha * l_ref[mp:mp + 1, :] + lsum
            m_ref[mp:mp + 1, :] = m_new

    def far_body(j, carry):
        step(j, [(0, t, 0, [])])
        return carry

    lax.fori_loop(0, jnp.maximum(i - 1, 0), far_body, 0)

    @pl.when(i >= 1)
    def _():
        step(i - 1, [(0, t, 0, [(t - sb, 0, _PREV)])])

    ns = t // sb
    step(i, [(a * sb, (a + 1) * sb, a * sb,
              [(0, 0, _DIAG)] + ([(0, sb, _PREV)] if a + 1 < ns else []))
             for a in range(ns)])

    lam = lam_ref[...]
    lam_full = (jnp.exp(jnp.sum(lam[0:1] * lam[1:2], axis=1, keepdims=True))
                - jnp.exp(jnp.sum(lam[2:3] * lam[3:4], axis=1, keepdims=True)) + lambda_init)
    o = acc_ref[0] * (1.0 / l_ref[0:1, :]) - lam_full * (acc_ref[1] * (1.0 / l_ref[1:2, :]))
    y = o * lax.rsqrt(jnp.mean(o * o, axis=0, keepdims=True) + SUBLN_EPS)
    o_ref[...] = (y.T * (g_ref[...] * (1.0 - lambda_init))).astype(BF16)


def _diff_attention(lam, q1, q2, k4, vt4, dtiles, cfar, subln_g, lambda_init):
    bsz, nblk, t, _ = k4.shape
    seq = nblk * t
    hw = HEAD_WIDTH
    sb = dtiles.shape[-1]
    kern = functools.partial(_attn_kernel, lambda_init=lambda_init)
    qspec = pl.BlockSpec((None, t, hw), lambda b, h, i: (b, i, h))
    return pl.pallas_call(
        kern,
        grid=(bsz, DIFF_HEADS, nblk),
        in_specs=[
            _const_spec(lam.shape),
            qspec, qspec,
            pl.BlockSpec((None, nblk, t, hw), lambda b, h, i: (b, 0, 0, h)),
            pl.BlockSpec((None, nblk, hw, t), lambda b, h, i: (b, 0, h, 0)),
            pl.BlockSpec((None, 2, sb, sb), lambda b, h, i: (h, 0, 0, 0)),
            pl.BlockSpec((None, 1, t), lambda b, h, i: (h, 0, 0)),
            _const_spec(subln_g.shape),
        ],
        out_specs=qspec,
        out_shape=jax.ShapeDtypeStruct((bsz, seq, A_QK), BF16),
        scratch_shapes=[pltpu.VMEM((SUBLANES, t), F32), pltpu.VMEM((SUBLANES, t), F32),
                        pltpu.VMEM((2, hw, t), F32)],
        compiler_params=pltpu.CompilerParams(
            dimension_semantics=("arbitrary", "arbitrary", "arbitrary"),
            vmem_limit_bytes=VMEM_LIMIT_BYTES),
        name="diff_attention",
    )(lam, q1, q2, k4, vt4, dtiles, cfar, subln_g)


def _residual_ffn(x, mix, ng_ref, wg_ref, wu_ref, wd_ref):
    x1 = x + _rms(mix, ng_ref[1:2, :])
    h2 = _rms(x1, ng_ref[2:3, :]).astype(BF16)
    gate = jnp.dot(h2, wg_ref[...], preferred_element_type=F32)
    up = jnp.dot(h2, wu_ref[...], preferred_element_type=F32)
    act = (gate * jax.nn.sigmoid(gate) * up).astype(BF16)
    f = jnp.dot(act, wd_ref[...], preferred_element_type=F32)
    return x1 + _rms(f, ng_ref[3:4, :])


def _even_out_kernel(x_ref, a_ref, bo_ref, wo_ref, ng_ref, wg_ref, wu_ref, wd_ref, o_ref):
    aw = a_ref.shape[1]
    mix = (jnp.dot(a_ref[...], wo_ref[0:aw, :], preferred_element_type=F32)
           + jnp.dot(bo_ref[...], wo_ref[aw:, :], preferred_element_type=F32))
    o_ref[...] = _residual_ffn(x_ref[...], mix, ng_ref, wg_ref, wu_ref, wd_ref)


def _even_out(x2d, a2d, bo2d, wo_bf, ng, wg_bf, wu_bf, wd_bf):
    n, d = x2d.shape
    tm = TOKEN_TILE
    tok = lambda w: pl.BlockSpec((tm, w), lambda i: (i, 0))
    return pl.pallas_call(
        _even_out_kernel,
        grid=(n // tm,),
        in_specs=[tok(d), tok(a2d.shape[1]), tok(bo2d.shape[1]), _const_spec(wo_bf.shape),
                  _const_spec(ng.shape), _const_spec(wg_bf.shape), _const_spec(wu_bf.shape),
                  _const_spec(wd_bf.shape)],
        out_specs=tok(d),
        out_shape=jax.ShapeDtypeStruct((n, d), F32),
        compiler_params=pltpu.CompilerParams(dimension_semantics=("arbitrary",),
                                             vmem_limit_bytes=VMEM_LIMIT_BYTES),
        name="even_out_ffn",
    )(x2d, a2d, bo2d, wo_bf, ng, wg_bf, wu_bf, wd_bf)


def _odd_kernel(x_ref, wi_ref, lng_ref, lnb_ref, ws_ref, sb_ref, wo_ref, ng_ref,
                wg_ref, wu_ref, wd_ref, o_ref):
    x = x_ref[...]
    tm = x.shape[0]
    sw = wo_ref.shape[0]
    gd = sw // SGU_GROUPS
    h = _rms(x, ng_ref[0:1, :]).astype(BF16)

    def gelu(z):
        return 0.5 * z * (1.0 + lax.erf(z * math.sqrt(0.5)))

    u = gelu(jnp.dot(h, wi_ref[:, 0:sw], preferred_element_type=F32))
    v = gelu(jnp.dot(h, wi_ref[:, sw:], preferred_element_type=F32))
    mu = jnp.mean(v, axis=-1, keepdims=True)
    vc = v - mu
    v = (vc * lax.rsqrt(jnp.mean(vc * vc, axis=-1, keepdims=True) + LN_EPS) * lng_ref[...]
         + lnb_ref[...]).astype(BF16)

    nch = tm // CHUNK
    tiles = [[None] * SGU_GROUPS for _ in range(nch)]
    for g in range(SGU_GROUPS):
        rhs = jnp.concatenate([v[n * CHUNK:(n + 1) * CHUNK, g * gd:(g + 1) * gd] for n in range(nch)],
                              axis=1)
        mixed = jnp.dot(ws_ref[g], rhs, preferred_element_type=F32) + sb_ref[g]
        for n in range(nch):
            tiles[n][g] = (u[n * CHUNK:(n + 1) * CHUNK, g * gd:(g + 1) * gd]
                           * mixed[:, n * gd:(n + 1) * gd])
    gated = jnp.concatenate([jnp.concatenate(r, axis=1) for r in tiles], axis=0).astype(BF16)
    mix = jnp.dot(gated, wo_ref[...], preferred_element_type=F32)
    o_ref[...] = _residual_ffn(x, mix, ng_ref, wg_ref, wu_ref, wd_ref)


def _odd_layer(x2d, wi_bf, ln_g, ln_b, ws_bf, sb, wo_bf, ng, wg_bf, wu_bf, wd_bf):
    n, d = x2d.shape
    tm = TOKEN_TILE
    tok = pl.BlockSpec((tm, d), lambda i: (i, 0))
    consts = (wi_bf, ln_g, ln_b, ws_bf, sb, wo_bf, ng, wg_bf, wu_bf, wd_bf)
    return pl.pallas_call(
        _odd_kernel,
        grid=(n // tm,),
        in_specs=[tok] + [_const_spec(a.shape) for a in consts],
        out_specs=tok,
        out_shape=jax.ShapeDtypeStruct((n, d), F32),
        compiler_params=pltpu.CompilerParams(dimension_semantics=("arbitrary",),
                                             vmem_limit_bytes=VMEM_LIMIT_BYTES),
        name="odd_sgu_ffn",
    )(x2d, *consts)


def _t5_bucket_of_distance(n):
    max_exact = REL_BUCKETS // 2
    nf = jnp.maximum(n, 1).astype(F32)
    large = max_exact + (jnp.log(nf / max_exact) / math.log(REL_MAX_DIST / max_exact)
                         * (REL_BUCKETS - max_exact)).astype(jnp.int32)
    large = jnp.minimum(large, REL_BUCKETS - 1)
    return jnp.where(n < max_exact, n, large)


def _bias_tiles(rel_bias, sb, t):
    assert sb >= REL_MAX_DIST
    table = rel_bias.astype(F32)
    key = jnp.arange(sb, dtype=jnp.int32)[:, None]
    qry = jnp.arange(sb, dtype=jnp.int32)[None, :]

    def lookup(dist):
        hit = _t5_bucket_of_distance(dist)[None, :, :, None] == jnp.arange(REL_BUCKETS)[:, None, None, None]
        return jnp.sum(jnp.where(hit, table[:, None, None, :], 0.0), axis=0)

    far = table[REL_BUCKETS - 1]
    diag = jnp.where((qry >= key)[..., None], lookup(jnp.maximum(qry - key, 0)) - far, MASK_VALUE)
    prev = lookup(sb + qry - key) - far
    tiles = jnp.transpose(jnp.stack([diag, prev]), (3, 0, 1, 2))
    cfar = jnp.broadcast_to(far[:, None, None], (rel_bias.shape[1], 1, t))
    return tiles, cfar


def kernel(x, rel_bias, w_in_even, diff_lambda, diff_subln_g, conv_w, w_out_even, w_in_odd,
           sgu_ln_g, sgu_ln_b, sgu_w, sgu_b, w_out_odd, norm_g, w_gate, w_up, w_down):
    bsz, seq, d = x.shape
    depth = norm_g.shape[0]
    t = ATTN_BLOCK
    assert seq % t == 0 and t % TOKEN_TILE == 0 and t % BIAS_TILE == 0 and TOKEN_TILE % CHUNK == 0
    x2d = x.reshape(bsz * seq, d)
    dtiles, cfar = _bias_tiles(rel_bias, BIAS_TILE, t)
    tril = jnp.tril(jnp.ones((CHUNK, CHUNK), dtype=bool))
    for i in range(depth):
        j = i // 2
        ng = norm_g[i]
        wg, wu, wd = w_gate[i].astype(BF16), w_up[i].astype(BF16), w_down[i].astype(BF16)
        if i % 2 == 0:
            lambda_init = 0.8 - 0.6 * math.exp(-0.3 * i)
            w_in = w_in_even[j]
            q1, q2, k, vt, bo = _even_in(
                x2d, ng[0:1], w_in.astype(BF16), w_in[:, 2 * A_QK:3 * A_QK].T.astype(BF16), conv_w[j], seq)
            a = _diff_attention(
                diff_lambda[j], q1.reshape(bsz, seq, A_QK), q2.reshape(bsz, seq, A_QK),
                k.reshape(bsz, seq // t, t, A_QK), vt.reshape(bsz, seq // t, A_QK, t),
                dtiles, cfar, diff_subln_g[j][None, :], lambda_init)
            x2d = _even_out(x2d, a.reshape(bsz * seq, A_QK), bo, w_out_even[j].astype(BF16), ng, wg, wu, wd)
        else:
            ws = jnp.where(tril[None], sgu_w[j], 0.0).astype(BF16)
            x2d = _odd_layer(x2d, w_in_odd[j].astype(BF16), sgu_ln_g[j][None, :], sgu_ln_b[j][None, :],
                             ws, sgu_b[j][:, :, None], w_out_odd[j].astype(BF16), ng, wg, wu, wd)
    return x2d.reshape(bsz, seq, d)
```

```python
import functools
import math

import jax
import jax.numpy as jnp
from jax import lax
from jax.experimental import pallas as pl
from jax.experimental.pallas import tpu as pltpu

F32 = jnp.float32
BF16 = jnp.bfloat16

DIFF_HEADS = 4
DIFF_QK_DIM = 64
HEAD_WIDTH = 2 * DIFF_QK_DIM
A_QK = DIFF_HEADS * HEAD_WIDTH
CONV_WIDTH = 3
SGU_GROUPS = 8
CHUNK = 128
REL_BUCKETS = 32
REL_MAX_DIST = 128
RMS_EPS = 1e-6
SUBLN_EPS = 1e-5
LN_EPS = 1e-5
MASK_VALUE = -1e30
LOG2_E = math.log2(math.e)

SUBLANES = 8
TOKEN_TILE = 512
ATTN_BLOCK = 1024
BIAS_TILE = 256
VMEM_LIMIT_BYTES = 56 * 1024 * 1024

_NT = (((1,), (1,)), ((), ()))


def _rms(x, g, eps=RMS_EPS):
    return x * lax.rsqrt(jnp.mean(x * x, axis=-1, keepdims=True) + eps) * g


def _const_spec(shape):
    return pl.BlockSpec(shape, lambda *_: (0,) * len(shape), pipeline_mode=pl.Buffered(1))


def _even_in_kernel(x_ref, g_ref, w_ref, wvt_ref, cw_ref,
                    q1_ref, q2_ref, k_ref, vt_ref, bo_ref, carry_ref, *, tiles_per_seq):
    tm = x_ref.shape[0]
    bw = bo_ref.shape[1]

    @pl.when(pl.program_id(0) % tiles_per_seq == 0)
    def _():
        carry_ref[...] = jnp.zeros_like(carry_ref)

    h = _rms(x_ref[...], g_ref[...]).astype(BF16)

    q = jnp.dot(h, w_ref[:, 0:A_QK], preferred_element_type=F32) * (DIFF_QK_DIM ** -0.5 * LOG2_E)
    first_map = (lax.broadcasted_iota(jnp.int32, q.shape, 1) % HEAD_WIDTH) < DIFF_QK_DIM
    q1_ref[...] = jnp.where(first_map, q, 0.0).astype(BF16)
    q2_ref[...] = jnp.where(first_map, 0.0, q).astype(BF16)
    k_ref[...] = jnp.dot(h, w_ref[:, A_QK:2 * A_QK], preferred_element_type=F32).astype(BF16)

    vt_ref[...] = lax.dot_general(wvt_ref[...], h, _NT, preferred_element_type=F32).astype(BF16)

    c0 = 3 * A_QK
    b_gate = jnp.dot(h, w_ref[:, c0:c0 + bw], preferred_element_type=F32)
    c_gate = jnp.dot(h, w_ref[:, c0 + bw:c0 + 2 * bw], preferred_element_type=F32)
    hc = jnp.dot(h, w_ref[:, c0 + 2 * bw:c0 + 3 * bw], preferred_element_type=F32)
    z = c_gate * hc
    row = lax.broadcasted_iota(jnp.int32, z.shape, 0)
    prev = carry_ref[...]
    zm1 = jnp.where(row == 0, prev[SUBLANES - 1:SUBLANES], pltpu.roll(z, 1, 0))
    zm2 = jnp.where(row == 0, prev[SUBLANES - 2:SUBLANES - 1],
                    jnp.where(row == 1, prev[SUBLANES - 1:SUBLANES], pltpu.roll(z, 2, 0)))
    cw = cw_ref[...]
    y = cw[0:1] * zm2 + cw[1:2] * zm1 + cw[2:3] * z
    bo_ref[...] = (b_gate * y).astype(BF16)
    carry_ref[...] = z[tm - SUBLANES:, :]


def _even_in(x2d, g, w_bf, wvt_bf, conv_w, seq):
    n, d = x2d.shape
    tm = TOKEN_TILE
    t = ATTN_BLOCK
    per_blk = t // tm
    bw = conv_w.shape[1]
    kern = functools.partial(_even_in_kernel, tiles_per_seq=seq // tm)
    tok = lambda w: pl.BlockSpec((tm, w), lambda i: (i, 0))
    return pl.pallas_call(
        kern,
        grid=(n // tm,),
        in_specs=[tok(d), _const_spec((1, d)), _const_spec(w_bf.shape), _const_spec(wvt_bf.shape),
                  _const_spec(conv_w.shape)],
        out_specs=[tok(A_QK), tok(A_QK), tok(A_QK),
                   pl.BlockSpec((None, A_QK, tm), lambda i: (i // per_blk, 0, i % per_blk)),
                   tok(bw)],
        out_shape=[jax.ShapeDtypeStruct((n, A_QK), BF16)] * 3
        + [jax.ShapeDtypeStruct((n // t, A_QK, t), BF16), jax.ShapeDtypeStruct((n, bw), BF16)],
        scratch_shapes=[pltpu.VMEM((SUBLANES, bw), F32)],
        compiler_params=pltpu.CompilerParams(dimension_semantics=("arbitrary",),
                                             vmem_limit_bytes=VMEM_LIMIT_BYTES),
        name="even_in_proj",
    )(x2d, g, w_bf, wvt_bf, conv_w)


_DIAG, _PREV = 0, 1


def _patched(s, tile_ref, patches):
    sb = tile_ref.shape[-1]
    for r, c, idx, scale in patches:
        band = s[r:r + sb, :]
        cols = [band[:, :c]] if c else []
        tile = tile_ref[idx] if scale is None else tile_ref[idx] * scale
        cols.append(band[:, c:c + sb] + tile)
        if c + sb < s.shape[1]:
            cols.append(band[:, c + sb:])
        rows = [s[:r]] if r else []
        rows.append(jnp.concatenate(cols, axis=1) if len(cols) > 1 else cols[0])
        if r + sb < s.shape[0]:
            rows.append(s[r + sb:])
        s = jnp.concatenate(rows, axis=0) if len(rows) > 1 else rows[0]
    return s


def _fold8(x, op):
    return op(x.reshape(x.shape[0] // SUBLANES, SUBLANES, x.shape[1]), axis=0)


def _lpad(x, width, value):
    if not width:
        return x
    return jnp.concatenate([jnp.full((x.shape[0], width), value, x.dtype), x], axis=1)


def _attn_kernel(lam_ref, q1_ref, q2_ref, k_ref, vt_ref, dt_ref, cfar_ref, g_ref, o_ref,
                 sa_ref, sb_ref, ca_ref, cb_ref, m_ref, l_ref, acc_ref, *, lambda_init):
    i = pl.program_id(2)
    t = q1_ref.shape[0]
    sb = dt_ref.shape[-1]
    ns = t // sb
    q_refs = (q1_ref, q2_ref)
    cfar = cfar_ref[...]
    rows = [slice(a * sb, (a + 1) * sb) for a in range(ns)]

    def fill_piece(blk, a, mp, s_ref, cmax):
        s = lax.dot_general(k_ref[blk, rows[a], :], q_refs[mp][...], _NT, preferred_element_type=F32)
        if a == ns - 1:
            is_prev = jnp.where(blk == i - 1, 1.0, 0.0).astype(F32)
            s = _patched(s, dt_ref, [(0, 0, _PREV, is_prev)])
        s_ref[mp, rows[a], :] = s
        pm = _fold8(s, jnp.max)
        return pm if cmax is None else jnp.maximum(cmax, pm)

    def past_step(blk, cur_s, cur_c, nxt_s, nxt_c):
        shift, alpha = [], []
        for mp in range(2):
            m_old = m_ref[mp]
            m_new = jnp.maximum(m_old, cur_c[mp] + cfar)
            m_ref[mp] = m_new
            shift.append(m_new - cfar)
            alpha.append(jnp.exp2(m_old - m_new))
        cmax, lsum, pv = [None, None], [None, None], [None, None]
        for a in range(ns):
            if nxt_s is not None:
                for mp in range(2):
                    cmax[mp] = fill_piece(blk + 1, a, mp, nxt_s, cmax[mp])
            for mp in range(2):
                e = jnp.exp2(cur_s[mp, rows[a], :] - shift[mp])
                ps = _fold8(e, jnp.sum)
                lsum[mp] = ps if lsum[mp] is None else lsum[mp] + ps
                d = jnp.dot(vt_ref[blk, :, rows[a]], e.astype(BF16), preferred_element_type=F32)
                pv[mp] = d if pv[mp] is None else pv[mp] + d
        for mp in range(2):
            if nxt_s is not None:
                nxt_c[mp] = jnp.max(cmax[mp], axis=0, keepdims=True)
            acc_ref[mp] = alpha[mp] * acc_ref[mp] + pv[mp]
            l_ref[mp] = alpha[mp] * l_ref[mp] + lsum[mp]

    def diag_step(fill_s, fill_c):
        strips, smax, cmax = [[None] * ns, [None] * ns], [None, None], [None, None]
        for a in range(ns):
            q_lo = a * sb
            patches = [(0, 0, _DIAG, None)] + ([(0, sb, _PREV, None)] if a + 1 < ns else [])
            for mp in range(2):
                s = lax.dot_general(k_ref[i, rows[a], :], q_refs[mp][q_lo:, :], _NT,
                                    preferred_element_type=F32)
                s = _patched(s, dt_ref, patches)
                strips[mp][a] = s
                pm = _lpad(jnp.max(s, axis=0, keepdims=True), q_lo, MASK_VALUE)
                smax[mp] = pm if smax[mp] is None else jnp.maximum(smax[mp], pm)
            for mp in range(2):
                cmax[mp] = fill_piece(0, a, mp, fill_s, cmax[mp])
        for mp in range(2):
            fill_c[mp] = jnp.max(cmax[mp], axis=0, keepdims=True)
            m_ref[mp] = smax[mp] + cfar
            lsum = None
            for a in range(ns):
                q_lo = a * sb
                e = jnp.exp2(strips[mp][a] - smax[mp][:, q_lo:])
                ps = _lpad(_fold8(e, jnp.sum), q_lo, 0.0)
                lsum = ps if lsum is None else lsum + ps
                d = jnp.dot(vt_ref[i, :, rows[a]], e.astype(BF16), preferred_element_type=F32)
                if a == 0:
                    acc_ref[mp] = d
                else:
                    acc_ref[mp, :, q_lo:] += d
            l_ref[mp] = lsum

    diag_step(sa_ref, ca_ref)

    def pair_body(p, carry):
        past_step(2 * p, sa_ref, ca_ref, sb_ref, cb_ref)
        past_step(2 * p + 1, sb_ref, cb_ref, sa_ref, ca_ref)
        return carry

    lax.fori_loop(0, jnp.maximum(i - 1, 0) // 2, pair_body, 0)

    @pl.when(jnp.logical_and(i >= 2, i % 2 == 0))
    def _():
        past_step(i - 2, sa_ref, ca_ref, sb_ref, cb_ref)
        past_step(i - 1, sb_ref, cb_ref, None, None)

    @pl.when(i % 2 == 1)
    def _():
        past_step(i - 1, sa_ref, ca_ref, None, None)

    lam = lam_ref[...]
    lam_full = (jnp.exp(jnp.sum(lam[0:1] * lam[1:2], axis=1, keepdims=True))
                - jnp.exp(jnp.sum(lam[2:3] * lam[3:4], axis=1, keepdims=True)) + lambda_init)
    inv_l = [1.0 / jnp.sum(l_ref[mp], axis=0, keepdims=True) for mp in range(2)]
    o = acc_ref[0] * inv_l[0] - lam_full * (acc_ref[1] * inv_l[1])
    y = o * lax.rsqrt(jnp.mean(o * o, axis=0, keepdims=True) + SUBLN_EPS)
    o_ref[...] = (y.T * (g_ref[...] * (1.0 - lambda_init))).astype(BF16)


def _diff_attention(lam, q1, q2, k4, vt4, dtiles, cfar, subln_g, lambda_init):
    bsz, nblk, t, _ = k4.shape
    seq = nblk * t
    hw = HEAD_WIDTH
    sb = dtiles.shape[-1]
    kern = functools.partial(_attn_kernel, lambda_init=lambda_init)
    qspec = pl.BlockSpec((None, t, hw), lambda b, h, i: (b, i, h))
    return pl.pallas_call(
        kern,
        grid=(bsz, DIFF_HEADS, nblk),
        in_specs=[
            _const_spec(lam.shape),
            qspec, qspec,
            pl.BlockSpec((None, nblk, t, hw), lambda b, h, i: (b, 0, 0, h)),
            pl.BlockSpec((None, nblk, hw, t), lambda b, h, i: (b, 0, h, 0)),
            pl.BlockSpec((None, 2, sb, sb), lambda b, h, i: (h, 0, 0, 0)),
            pl.BlockSpec((None, 1, t), lambda b, h, i: (h, 0, 0)),
            _const_spec(subln_g.shape),
        ],
        out_specs=qspec,
        out_shape=jax.ShapeDtypeStruct((bsz, seq, A_QK), BF16),
        scratch_shapes=[pltpu.VMEM((2, t, t), F32), pltpu.VMEM((2, t, t), F32),
                        pltpu.VMEM((2, 1, t), F32), pltpu.VMEM((2, 1, t), F32),
                        pltpu.VMEM((2, 1, t), F32),
                        pltpu.VMEM((2, SUBLANES, t), F32),
                        pltpu.VMEM((2, hw, t), F32)],
        compiler_params=pltpu.CompilerParams(
            dimension_semantics=("arbitrary", "arbitrary", "arbitrary"),
            vmem_limit_bytes=VMEM_LIMIT_BYTES),
        name="diff_attention",
    )(lam, q1, q2, k4, vt4, dtiles, cfar, subln_g)


def _residual_ffn(x, mix, ng_ref, wg_ref, wu_ref, wd_ref):
    x1 = x + _rms(mix, ng_ref[1:2, :])
    h2 = _rms(x1, ng_ref[2:3, :]).astype(BF16)
    gate = jnp.dot(h2, wg_ref[...], preferred_element_type=F32)
    up = jnp.dot(h2, wu_ref[...], preferred_element_type=F32)
    act = (gate * jax.nn.sigmoid(gate) * up).astype(BF16)
    f = jnp.dot(act, wd_ref[...], preferred_element_type=F32)
    return x1 + _rms(f, ng_ref[3:4, :])


def _even_out_kernel(x_ref, a_ref, bo_ref, wo_ref, ng_ref, wg_ref, wu_ref, wd_ref, o_ref):
    aw = a_ref.shape[1]
    mix = (jnp.dot(a_ref[...], wo_ref[0:aw, :], preferred_element_type=F32)
           + jnp.dot(bo_ref[...], wo_ref[aw:, :], preferred_element_type=F32))
    o_ref[...] = _residual_ffn(x_ref[...], mix, ng_ref, wg_ref, wu_ref, wd_ref)


def _even_out(x2d, a2d, bo2d, wo_bf, ng, wg_bf, wu_bf, wd_bf):
    n, d = x2d.shape
    tm = TOKEN_TILE
    tok = lambda w: pl.BlockSpec((tm, w), lambda i: (i, 0))
    return pl.pallas_call(
        _even_out_kernel,
        grid=(n // tm,),
        in_specs=[tok(d), tok(a2d.shape[1]), tok(bo2d.shape[1]), _const_spec(wo_bf.shape),
                  _const_spec(ng.shape), _const_spec(wg_bf.shape), _const_spec(wu_bf.shape),
                  _const_spec(wd_bf.shape)],
        out_specs=tok(d),
        out_shape=jax.ShapeDtypeStruct((n, d), F32),
        compiler_params=pltpu.CompilerParams(dimension_semantics=("arbitrary",),
                                             vmem_limit_bytes=VMEM_LIMIT_BYTES),
        name="even_out_ffn",
    )(x2d, a2d, bo2d, wo_bf, ng, wg_bf, wu_bf, wd_bf)


def _odd_kernel(x_ref, wi_ref, lng_ref, lnb_ref, ws_ref, sb_ref, wo_ref, ng_ref,
                wg_ref, wu_ref, wd_ref, o_ref):
    x = x_ref[...]
    tm = x.shape[0]
    sw = wo_ref.shape[0]
    gd = sw // SGU_GROUPS
    h = _rms(x, ng_ref[0:1, :]).astype(BF16)

    def gelu(z):
        return 0.5 * z * (1.0 + lax.erf(z * math.sqrt(0.5)))

    u = gelu(jnp.dot(h, wi_ref[:, 0:sw], preferred_element_type=F32))
    v = gelu(jnp.dot(h, wi_ref[:, sw:], preferred_element_type=F32))
    mu = jnp.mean(v, axis=-1, keepdims=True)
    vc = v - mu
    v = (vc * lax.rsqrt(jnp.mean(vc * vc, axis=-1, keepdims=True) + LN_EPS) * lng_ref[...]
         + lnb_ref[...]).astype(BF16)

    nch = tm // CHUNK
    tiles = [[None] * SGU_GROUPS for _ in range(nch)]
    for g in range(SGU_GROUPS):
        rhs = jnp.concatenate([v[n * CHUNK:(n + 1) * CHUNK, g * gd:(g + 1) * gd] for n in range(nch)],
                              axis=1)
        mixed = jnp.dot(ws_ref[g], rhs, preferred_element_type=F32) + sb_ref[g]
        for n in range(nch):
            tiles[n][g] = (u[n * CHUNK:(n + 1) * CHUNK, g * gd:(g + 1) * gd]
                           * mixed[:, n * gd:(n + 1) * gd])
    gated = jnp.concatenate([jnp.concatenate(r, axis=1) for r in tiles], axis=0).astype(BF16)
    mix = jnp.dot(gated, wo_ref[...], preferred_element_type=F32)
    o_ref[...] = _residual_ffn(x, mix, ng_ref, wg_ref, wu_ref, wd_ref)


def _odd_layer(x2d, wi_bf, ln_g, ln_b, ws_bf, sb, wo_bf, ng, wg_bf, wu_bf, wd_bf):
    n, d = x2d.shape
    tm = TOKEN_TILE
    tok = pl.BlockSpec((tm, d), lambda i: (i, 0))
    consts = (wi_bf, ln_g, ln_b, ws_bf, sb, wo_bf, ng, wg_bf, wu_bf, wd_bf)
    return pl.pallas_call(
        _odd_kernel,
        grid=(n // tm,),
        in_specs=[tok] + [_const_spec(a.shape) for a in consts],
        out_specs=tok,
        out_shape=jax.ShapeDtypeStruct((n, d), F32),
        compiler_params=pltpu.CompilerParams(dimension_semantics=("arbitrary",),
                                             vmem_limit_bytes=VMEM_LIMIT_BYTES),
        name="odd_sgu_ffn",
    )(x2d, *consts)


def _t5_bucket_of_distance(n):
    max_exact = REL_BUCKETS // 2
    nf = jnp.maximum(n, 1).astype(F32)
    large = max_exact + (jnp.log(nf / max_exact) / math.log(REL_MAX_DIST / max_exact)
                         * (REL_BUCKETS - max_exact)).astype(jnp.int32)
    large = jnp.minimum(large, REL_BUCKETS - 1)
    return jnp.where(n < max_exact, n, large)


def _bias_tiles(rel_bias, sb, t):
    assert sb >= REL_MAX_DIST
    table = rel_bias.astype(F32)
    key = jnp.arange(sb, dtype=jnp.int32)[:, None]
    qry = jnp.arange(sb, dtype=jnp.int32)[None, :]

    def lookup(dist):
        hit = _t5_bucket_of_distance(dist)[None, :, :, None] == jnp.arange(REL_BUCKETS)[:, None, None, None]
        return jnp.sum(jnp.where(hit, table[:, None, None, :], 0.0), axis=0)

    far = table[REL_BUCKETS - 1]
    diag = jnp.where((qry >= key)[..., None], (lookup(jnp.maximum(qry - key, 0)) - far) * LOG2_E, MASK_VALUE)
    prev = (lookup(sb + qry - key) - far) * LOG2_E
    tiles = jnp.transpose(jnp.stack([diag, prev]), (3, 0, 1, 2))
    cfar = jnp.broadcast_to((far * LOG2_E)[:, None, None], (rel_bias.shape[1], 1, t))
    return tiles, cfar


def kernel(x, rel_bias, w_in_even, diff_lambda, diff_subln_g, conv_w, w_out_even, w_in_odd,
           sgu_ln_g, sgu_ln_b, sgu_w, sgu_b, w_out_odd, norm_g, w_gate, w_up, w_down):
    bsz, seq, d = x.shape
    depth = norm_g.shape[0]
    t = ATTN_BLOCK
    assert seq % t == 0 and t % TOKEN_TILE == 0 and t % BIAS_TILE == 0 and TOKEN_TILE % CHUNK == 0
    x2d = x.reshape(bsz * seq, d)
    dtiles, cfar = _bias_tiles(rel_bias, BIAS_TILE, t)
    tril = jnp.tril(jnp.ones((CHUNK, CHUNK), dtype=bool))
    for i in range(depth):
        j = i // 2
        ng = norm_g[i]
        wg, wu, wd = w_gate[i].astype(BF16), w_up[i].astype(BF16), w_down[i].astype(BF16)
        if i % 2 == 0:
            lambda_init = 0.8 - 0.6 * math.exp(-0.3 * i)
            w_in = w_in_even[j]
            q1, q2, k, vt, bo = _even_in(
                x2d, ng[0:1], w_in.astype(BF16), w_in[:, 2 * A_QK:3 * A_QK].T.astype(BF16), conv_w[j], seq)
            a = _diff_attention(
                diff_lambda[j], q1.reshape(bsz, seq, A_QK), q2.reshape(bsz, seq, A_QK),
                k.reshape(bsz, seq // t, t, A_QK), vt.reshape(bsz, seq // t, A_QK, t),
                dtiles, cfar, diff_subln_g[j][None, :], lambda_init)
            x2d = _even_out(x2d, a.reshape(bsz * seq, A_QK), bo, w_out_even[j].astype(BF16), ng, wg, wu, wd)
        else:
            ws = jnp.where(tril[None], sgu_w[j], 0.0).astype(BF16)
            x2d = _odd_layer(x2d, w_in_odd[j].astype(BF16), sgu_ln_g[j][None, :], sgu_ln_b[j][None, :],
                             ws, sgu_b[j][:, :, None], w_out_odd[j].astype(BF16), ng, wg, wu, wd)
    return x2d.reshape(bsz, seq, d)
```

```python
import functools
import math

import jax
import jax.numpy as jnp
from jax import lax
from jax.experimental import pallas as pl
from jax.experimental.pallas import tpu as pltpu

F32 = jnp.float32
BF16 = jnp.bfloat16

DIFF_HEADS = 4
DIFF_QK_DIM = 64
HEAD_WIDTH = 2 * DIFF_QK_DIM
A_QK = DIFF_HEADS * HEAD_WIDTH
CONV_WIDTH = 3
SGU_GROUPS = 8
CHUNK = 128
REL_BUCKETS = 32
REL_MAX_DIST = 128
RMS_EPS = 1e-6
SUBLN_EPS = 1e-5
LN_EPS = 1e-5
MASK_VALUE = -1e30
LOG2_E = math.log2(math.e)

SUBLANES = 8
TOKEN_TILE = 512
ATTN_BLOCK = 1024
BIAS_TILE = 256
VMEM_LIMIT_BYTES = 56 * 1024 * 1024

_NT = (((1,), (1,)), ((), ()))


def _rms(x, g, eps=RMS_EPS):
    return x * lax.rsqrt(jnp.mean(x * x, axis=-1, keepdims=True) + eps) * g


def _const_spec(shape):
    return pl.BlockSpec(shape, lambda *_: (0,) * len(shape), pipeline_mode=pl.Buffered(1))


def _layer_spec(stacked, layer):
    rest = stacked.shape[1:]
    return pl.BlockSpec((None,) + rest, lambda *_: (layer,) + (0,) * len(rest),
                        pipeline_mode=pl.Buffered(1))


def _even_in_kernel(x_ref, ng_ref, w_ref, cw_ref,
                    q1_ref, q2_ref, k_ref, vt_ref, bo_ref, carry_ref, *, tiles_per_seq):
    tm = x_ref.shape[0]
    bw = bo_ref.shape[1]

    @pl.when(pl.program_id(0) % tiles_per_seq == 0)
    def _():
        carry_ref[...] = jnp.zeros_like(carry_ref)

    h = _rms(x_ref[...], ng_ref[0:1, :]).astype(BF16)

    q = jnp.dot(h, w_ref[:, 0:A_QK], preferred_element_type=F32) * (DIFF_QK_DIM ** -0.5 * LOG2_E)
    first_map = (lax.broadcasted_iota(jnp.int32, q.shape, 1) % HEAD_WIDTH) < DIFF_QK_DIM
    q1_ref[...] = jnp.where(first_map, q, 0.0).astype(BF16)
    q2_ref[...] = jnp.where(first_map, 0.0, q).astype(BF16)
    k_ref[...] = jnp.dot(h, w_ref[:, A_QK:2 * A_QK], preferred_element_type=F32).astype(BF16)

    v = jnp.dot(h, w_ref[:, 2 * A_QK:3 * A_QK], preferred_element_type=F32)
    vt_ref[...] = v.T.astype(BF16)

    c0 = 3 * A_QK
    b_gate = jnp.dot(h, w_ref[:, c0:c0 + bw], preferred_element_type=F32)
    c_gate = jnp.dot(h, w_ref[:, c0 + bw:c0 + 2 * bw], preferred_element_type=F32)
    hc = jnp.dot(h, w_ref[:, c0 + 2 * bw:c0 + 3 * bw], preferred_element_type=F32)
    z = c_gate * hc
    row = lax.broadcasted_iota(jnp.int32, z.shape, 0)
    prev = carry_ref[...]
    zm1 = jnp.where(row == 0, prev[SUBLANES - 1:SUBLANES], pltpu.roll(z, 1, 0))
    zm2 = jnp.where(row == 0, prev[SUBLANES - 2:SUBLANES - 1],
                    jnp.where(row == 1, prev[SUBLANES - 1:SUBLANES], pltpu.roll(z, 2, 0)))
    cw = cw_ref[...]
    y = cw[0:1] * zm2 + cw[1:2] * zm1 + cw[2:3] * z
    bo_ref[...] = (b_gate * y).astype(BF16)
    carry_ref[...] = z[tm - SUBLANES:, :]


def _even_in(x2d, norm_g, w_bf, conv_w, seq, layer):
    n, d = x2d.shape
    tm = TOKEN_TILE
    t = ATTN_BLOCK
    per_blk = t // tm
    bw = conv_w.shape[-1]
    kern = functools.partial(_even_in_kernel, tiles_per_seq=seq // tm)
    tok = lambda w: pl.BlockSpec((tm, w), lambda i: (i, 0))
    return pl.pallas_call(
        kern,
        grid=(n // tm,),
        in_specs=[tok(d), _layer_spec(norm_g, layer), _layer_spec(w_bf, layer // 2),
                  _layer_spec(conv_w, layer // 2)],
        out_specs=[tok(A_QK), tok(A_QK), tok(A_QK),
                   pl.BlockSpec((None, A_QK, tm), lambda i: (i // per_blk, 0, i % per_blk)),
                   tok(bw)],
        out_shape=[jax.ShapeDtypeStruct((n, A_QK), BF16)] * 3
        + [jax.ShapeDtypeStruct((n // t, A_QK, t), BF16), jax.ShapeDtypeStruct((n, bw), BF16)],
        scratch_shapes=[pltpu.VMEM((SUBLANES, bw), F32)],
        compiler_params=pltpu.CompilerParams(dimension_semantics=("arbitrary",),
                                             vmem_limit_bytes=VMEM_LIMIT_BYTES),
        name="even_in_proj",
    )(x2d, norm_g, w_bf, conv_w)


_DIAG, _PREV = 0, 1


def _patched(s, tile_ref, patches):
    sb = tile_ref.shape[-1]
    for r, c, idx, scale in patches:
        band = s[r:r + sb, :]
        cols = [band[:, :c]] if c else []
        tile = tile_ref[idx] if scale is None else tile_ref[idx] * scale
        cols.append(band[:, c:c + sb] + tile)
        if c + sb < s.shape[1]:
            cols.append(band[:, c + sb:])
        rows = [s[:r]] if r else []
        rows.append(jnp.concatenate(cols, axis=1) if len(cols) > 1 else cols[0])
        if r + sb < s.shape[0]:
            rows.append(s[r + sb:])
        s = jnp.concatenate(rows, axis=0) if len(rows) > 1 else rows[0]
    return s


def _fold8(x, op):
    return op(x.reshape(x.shape[0] // SUBLANES, SUBLANES, x.shape[1]), axis=0)


def _lpad(x, width, value):
    if not width:
        return x
    return jnp.concatenate([jnp.full((x.shape[0], width), value, x.dtype), x], axis=1)


def _attn_kernel(lam_ref, q1_ref, q2_ref, k_ref, vt_ref, dt_ref, cfar_ref, g_ref, o_ref,
                 sa_ref, sb_ref, ca_ref, cb_ref, m_ref, l_ref, acc_ref, *, lambda_init):
    i = pl.program_id(2)
    t = q1_ref.shape[0]
    sb = dt_ref.shape[-1]
    ns = t // sb
    q_refs = (q1_ref, q2_ref)
    cfar = cfar_ref[...]
    rows = [slice(a * sb, (a + 1) * sb) for a in range(ns)]

    def fill_piece(blk, a, mp, s_ref, cmax):
        s = lax.dot_general(k_ref[blk, rows[a], :], q_refs[mp][...], _NT, preferred_element_type=F32)
        if a == ns - 1:
            is_prev = jnp.where(blk == i - 1, 1.0, 0.0).astype(F32)
            s = _patched(s, dt_ref, [(0, 0, _PREV, is_prev)])
        s_ref[mp, rows[a], :] = s
        pm = _fold8(s, jnp.max)
        return pm if cmax is None else jnp.maximum(cmax, pm)

    def past_step(blk, cur_s, cur_c, nxt_s, nxt_c):
        shift, alpha = [], []
        for mp in range(2):
            m_old = m_ref[mp]
            m_new = jnp.maximum(m_old, cur_c[mp] + cfar)
            m_ref[mp] = m_new
            shift.append(m_new - cfar)
            alpha.append(jnp.exp2(m_old - m_new))
        cmax, lsum, pv = [None, None], [None, None], [None, None]
        for a in range(ns):
            if nxt_s is not None:
                for mp in range(2):
                    cmax[mp] = fill_piece(blk + 1, a, mp, nxt_s, cmax[mp])
            for mp in range(2):
                e = jnp.exp2(cur_s[mp, rows[a], :] - shift[mp])
                ps = _fold8(e, jnp.sum)
                lsum[mp] = ps if lsum[mp] is None else lsum[mp] + ps
                d = jnp.dot(vt_ref[blk, :, rows[a]], e.astype(BF16), preferred_element_type=F32)
                pv[mp] = d if pv[mp] is None else pv[mp] + d
        for mp in range(2):
            if nxt_s is not None:
                nxt_c[mp] = jnp.max(cmax[mp], axis=0, keepdims=True)
            acc_ref[mp] = alpha[mp] * acc_ref[mp] + pv[mp]
            l_ref[mp] = alpha[mp] * l_ref[mp] + lsum[mp]

    def diag_step(fill_s, fill_c):
        strips, smax, cmax = [[None] * ns, [None] * ns], [None, None], [None, None]
        for a in range(ns):
            q_lo = a * sb
            patches = [(0, 0, _DIAG, None)] + ([(0, sb, _PREV, None)] if a + 1 < ns else [])
            for mp in range(2):
                s = lax.dot_general(k_ref[i, rows[a], :], q_refs[mp][q_lo:, :], _NT,
                                    preferred_element_type=F32)
                s = _patched(s, dt_ref, patches)
                strips[mp][a] = s
                pm = _lpad(jnp.max(s, axis=0, keepdims=True), q_lo, MASK_VALUE)
                smax[mp] = pm if smax[mp] is None else jnp.maximum(smax[mp], pm)
            for mp in range(2):
                cmax[mp] = fill_piece(0, a, mp, fill_s, cmax[mp])
        for mp in range(2):
            fill_c[mp] = jnp.max(cmax[mp], axis=0, keepdims=True)
            m_ref[mp] = smax[mp] + cfar
            lsum = None
            for a in range(ns):
                q_lo = a * sb
                e = jnp.exp2(strips[mp][a] - smax[mp][:, q_lo:])
                ps = _lpad(_fold8(e, jnp.sum), q_lo, 0.0)
                lsum = ps if lsum is None else lsum + ps
                d = jnp.dot(vt_ref[i, :, rows[a]], e.astype(BF16), preferred_element_type=F32)
                if a == 0:
                    acc_ref[mp] = d
                else:
                    acc_ref[mp, :, q_lo:] += d
            l_ref[mp] = lsum

    diag_step(sa_ref, ca_ref)

    def pair_body(p, carry):
        past_step(2 * p, sa_ref, ca_ref, sb_ref, cb_ref)
        past_step(2 * p + 1, sb_ref, cb_ref, sa_ref, ca_ref)
        return carry

    lax.fori_loop(0, jnp.maximum(i - 1, 0) // 2, pair_body, 0)

    @pl.when(jnp.logical_and(i >= 2, i % 2 == 0))
    def _():
        past_step(i - 2, sa_ref, ca_ref, sb_ref, cb_ref)
        past_step(i - 1, sb_ref, cb_ref, None, None)

    @pl.when(i % 2 == 1)
    def _():
        past_step(i - 1, sa_ref, ca_ref, None, None)

    lam = lam_ref[...]
    lam_full = (jnp.exp(jnp.sum(lam[0:1] * lam[1:2], axis=1, keepdims=True))
                - jnp.exp(jnp.sum(lam[2:3] * lam[3:4], axis=1, keepdims=True)) + lambda_init)
    inv_l = [1.0 / jnp.sum(l_ref[mp], axis=0, keepdims=True) for mp in range(2)]
    o = acc_ref[0] * inv_l[0] - lam_full * (acc_ref[1] * inv_l[1])
    y = o * lax.rsqrt(jnp.mean(o * o, axis=0, keepdims=True) + SUBLN_EPS)
    o_ref[...] = (y.T * (g_ref[...] * (1.0 - lambda_init))).astype(BF16)


def _diff_attention(lam, q1, q2, k4, vt4, dtiles, cfar, subln_g, lambda_init, layer):
    bsz, nblk, t, _ = k4.shape
    seq = nblk * t
    hw = HEAD_WIDTH
    sb = dtiles.shape[-1]
    kern = functools.partial(_attn_kernel, lambda_init=lambda_init)
    qspec = pl.BlockSpec((None, t, hw), lambda b, h, i: (b, i, h))
    return pl.pallas_call(
        kern,
        grid=(bsz, DIFF_HEADS, nblk),
        in_specs=[
            _layer_spec(lam, layer // 2),
            qspec, qspec,
            pl.BlockSpec((None, nblk, t, hw), lambda b, h, i: (b, 0, 0, h)),
            pl.BlockSpec((None, nblk, hw, t), lambda b, h, i: (b, 0, h, 0)),
            pl.BlockSpec((None, 2, sb, sb), lambda b, h, i: (h, 0, 0, 0)),
            pl.BlockSpec((None, 1, t), lambda b, h, i: (h, 0, 0)),
            _layer_spec(subln_g, layer // 2),
        ],
        out_specs=qspec,
        out_shape=jax.ShapeDtypeStruct((bsz, seq, A_QK), BF16),
        scratch_shapes=[pltpu.VMEM((2, t, t), F32), pltpu.VMEM((2, t, t), F32),
                        pltpu.VMEM((2, 1, t), F32), pltpu.VMEM((2, 1, t), F32),
                        pltpu.VMEM((2, 1, t), F32),
                        pltpu.VMEM((2, SUBLANES, t), F32),
                        pltpu.VMEM((2, hw, t), F32)],
        compiler_params=pltpu.CompilerParams(
            dimension_semantics=("arbitrary", "arbitrary", "arbitrary"),
            vmem_limit_bytes=VMEM_LIMIT_BYTES),
        name="diff_attention",
    )(lam, q1, q2, k4, vt4, dtiles, cfar, subln_g)


def _residual_ffn(x, mix, ng_ref, wg_ref, wu_ref, wd_ref):
    x1 = x + _rms(mix, ng_ref[1:2, :])
    h2 = _rms(x1, ng_ref[2:3, :]).astype(BF16)
    gate = jnp.dot(h2, wg_ref[...], preferred_element_type=F32)
    up = jnp.dot(h2, wu_ref[...], preferred_element_type=F32)
    act = (gate * jax.nn.sigmoid(gate) * up).astype(BF16)
    f = jnp.dot(act, wd_ref[...], preferred_element_type=F32)
    return x1 + _rms(f, ng_ref[3:4, :])


def _even_out_kernel(x_ref, a_ref, bo_ref, wo_ref, ng_ref, wg_ref, wu_ref, wd_ref, o_ref):
    aw = a_ref.shape[1]
    mix = (jnp.dot(a_ref[...], wo_ref[0:aw, :], preferred_element_type=F32)
           + jnp.dot(bo_ref[...], wo_ref[aw:, :], preferred_element_type=F32))
    o_ref[...] = _residual_ffn(x_ref[...], mix, ng_ref, wg_ref, wu_ref, wd_ref)


def _even_out(x2d, a2d, bo2d, wo_bf, ng, wg_bf, wu_bf, wd_bf, layer):
    n, d = x2d.shape
    tm = TOKEN_TILE
    tok = lambda w: pl.BlockSpec((tm, w), lambda i: (i, 0))
    return pl.pallas_call(
        _even_out_kernel,
        grid=(n // tm,),
        in_specs=[tok(d), tok(a2d.shape[1]), tok(bo2d.shape[1]), _layer_spec(wo_bf, layer // 2),
                  _layer_spec(ng, layer), _layer_spec(wg_bf, layer), _layer_spec(wu_bf, layer),
                  _layer_spec(wd_bf, layer)],
        out_specs=tok(d),
        out_shape=jax.ShapeDtypeStruct((n, d), F32),
        compiler_params=pltpu.CompilerParams(dimension_semantics=("arbitrary",),
                                             vmem_limit_bytes=VMEM_LIMIT_BYTES),
        name="even_out_ffn",
    )(x2d, a2d, bo2d, wo_bf, ng, wg_bf, wu_bf, wd_bf)


def _odd_kernel(x_ref, wi_ref, lng_ref, lnb_ref, ws_ref, sb_ref, wo_ref, ng_ref,
                wg_ref, wu_ref, wd_ref, o_ref):
    x = x_ref[...]
    tm = x.shape[0]
    sw = wo_ref.shape[0]
    gd = sw // SGU_GROUPS
    h = _rms(x, ng_ref[0:1, :]).astype(BF16)

    def gelu(z):
        return 0.5 * z * (1.0 + lax.erf(z * math.sqrt(0.5)))

    u = gelu(jnp.dot(h, wi_ref[:, 0:sw], preferred_element_type=F32))
    v = gelu(jnp.dot(h, wi_ref[:, sw:], preferred_element_type=F32))
    mu = jnp.mean(v, axis=-1, keepdims=True)
    vc = v - mu
    v = (vc * lax.rsqrt(jnp.mean(vc * vc, axis=-1, keepdims=True) + LN_EPS) * lng_ref[...]
         + lnb_ref[...]).astype(BF16)

    nch = tm // CHUNK
    tiles = [[None] * SGU_GROUPS for _ in range(nch)]
    for g in range(SGU_GROUPS):
        rhs = jnp.concatenate([v[n * CHUNK:(n + 1) * CHUNK, g * gd:(g + 1) * gd] for n in range(nch)],
                              axis=1)
        mixed = jnp.dot(ws_ref[g], rhs, preferred_element_type=F32) + sb_ref[g]
        for n in range(nch):
            tiles[n][g] = (u[n * CHUNK:(n + 1) * CHUNK, g * gd:(g + 1) * gd]
                           * mixed[:, n * gd:(n + 1) * gd])
    gated = jnp.concatenate([jnp.concatenate(r, axis=1) for r in tiles], axis=0).astype(BF16)
    mix = jnp.dot(gated, wo_ref[...], preferred_element_type=F32)
    o_ref[...] = _residual_ffn(x, mix, ng_ref, wg_ref, wu_ref, wd_ref)


def _odd_layer(x2d, wi_bf, ln_g, ln_b, ws_bf, sb, wo_bf, ng, wg_bf, wu_bf, wd_bf, layer):
    n, d = x2d.shape
    tm = TOKEN_TILE
    tok = pl.BlockSpec((tm, d), lambda i: (i, 0))
    per_odd = (wi_bf, ln_g, ln_b, ws_bf, sb, wo_bf)
    per_layer = (ng, wg_bf, wu_bf, wd_bf)
    consts = per_odd + per_layer
    return pl.pallas_call(
        _odd_kernel,
        grid=(n // tm,),
        in_specs=[tok] + [_layer_spec(a, layer // 2) for a in per_odd]
        + [_layer_spec(a, layer) for a in per_layer],
        out_specs=tok,
        out_shape=jax.ShapeDtypeStruct((n, d), F32),
        compiler_params=pltpu.CompilerParams(dimension_semantics=("arbitrary",),
                                             vmem_limit_bytes=VMEM_LIMIT_BYTES),
        name="odd_sgu_ffn",
    )(x2d, *consts)


def _t5_bucket_of_distance(n):
    max_exact = REL_BUCKETS // 2
    nf = jnp.maximum(n, 1).astype(F32)
    large = max_exact + (jnp.log(nf / max_exact) / math.log(REL_MAX_DIST / max_exact)
                         * (REL_BUCKETS - max_exact)).astype(jnp.int32)
    large = jnp.minimum(large, REL_BUCKETS - 1)
    return jnp.where(n < max_exact, n, large)


def _bias_tiles(rel_bias, sb, t):
    assert sb >= REL_MAX_DIST
    table = rel_bias.astype(F32)
    key = jnp.arange(sb, dtype=jnp.int32)[:, None]
    qry = jnp.arange(sb, dtype=jnp.int32)[None, :]

    def lookup(dist):
        hit = _t5_bucket_of_distance(dist)[None, :, :, None] == jnp.arange(REL_BUCKETS)[:, None, None, None]
        return jnp.sum(jnp.where(hit, table[:, None, None, :], 0.0), axis=0)

    far = table[REL_BUCKETS - 1]
    diag = jnp.where((qry >= key)[..., None], (lookup(jnp.maximum(qry - key, 0)) - far) * LOG2_E, MASK_VALUE)
    prev = (lookup(sb + qry - key) - far) * LOG2_E
    tiles = jnp.transpose(jnp.stack([diag, prev]), (3, 0, 1, 2))
    cfar = jnp.broadcast_to((far * LOG2_E)[:, None, None], (rel_bias.shape[1], 1, t))
    return tiles, cfar


def kernel(x, rel_bias, w_in_even, diff_lambda, diff_subln_g, conv_w, w_out_even, w_in_odd,
           sgu_ln_g, sgu_ln_b, sgu_w, sgu_b, w_out_odd, norm_g, w_gate, w_up, w_down):
    bsz, seq, d = x.shape
    depth = norm_g.shape[0]
    t = ATTN_BLOCK
    assert seq % t == 0 and t % TOKEN_TILE == 0 and t % BIAS_TILE == 0 and TOKEN_TILE % CHUNK == 0
    x2d = x.reshape(bsz * seq, d)
    dtiles, cfar = _bias_tiles(rel_bias, BIAS_TILE, t)
    tril = jnp.tril(jnp.ones((CHUNK, CHUNK), dtype=bool))
    wg, wu, wd = w_gate.astype(BF16), w_up.astype(BF16), w_down.astype(BF16)
    w_in_even_bf, w_out_even_bf = w_in_even.astype(BF16), w_out_even.astype(BF16)
    w_in_odd_bf, w_out_odd_bf = w_in_odd.astype(BF16), w_out_odd.astype(BF16)
    ws = jnp.where(tril, sgu_w, 0.0).astype(BF16)
    for i in range(depth):
        if i % 2 == 0:
            lambda_init = 0.8 - 0.6 * math.exp(-0.3 * i)
            q1, q2, k, vt, bo = _even_in(x2d, norm_g, w_in_even_bf, conv_w, seq, i)
            a = _diff_attention(
                diff_lambda, q1.reshape(bsz, seq, A_QK), q2.reshape(bsz, seq, A_QK),
                k.reshape(bsz, seq // t, t, A_QK), vt.reshape(bsz, seq // t, A_QK, t),
                dtiles, cfar, diff_subln_g[:, None, :], lambda_init, i)
            x2d = _even_out(x2d, a.reshape(bsz * seq, A_QK), bo, w_out_even_bf, norm_g, wg, wu, wd, i)
        else:
            x2d = _odd_layer(x2d, w_in_odd_bf, sgu_ln_g[:, None, :], sgu_ln_b[:, None, :], ws,
                             sgu_b[..., None], w_out_odd_bf, norm_g, wg, wu, wd, i)
    return x2d.reshape(bsz, seq, d)
```

```python
import functools
import math

import jax
import jax.numpy as jnp
from jax import lax
from jax.experimental import pallas as pl
from jax.experimental.pallas import tpu as pltpu

F32 = jnp.float32
BF16 = jnp.bfloat16

DIFF_HEADS = 4
DIFF_QK_DIM = 64
HEAD_WIDTH = 2 * DIFF_QK_DIM
A_QK = DIFF_HEADS * HEAD_WIDTH
CONV_WIDTH = 3
SGU_GROUPS = 8
CHUNK = 128
REL_BUCKETS = 32
REL_MAX_DIST = 128
RMS_EPS = 1e-6
SUBLN_EPS = 1e-5
LN_EPS = 1e-5
MASK_VALUE = -1e30
LOG2_E = math.log2(math.e)

SUBLANES = 8
TOKEN_TILE = 512
ROW_PARTS = 2
ATTN_BLOCK = 1024
BIAS_TILE = 256
VMEM_LIMIT_BYTES = 56 * 1024 * 1024

_NT = (((1,), (1,)), ((), ()))


def _rms(x, g, eps=RMS_EPS):
    return x * lax.rsqrt(jnp.mean(x * x, axis=-1, keepdims=True) + eps) * g


def _const_spec(shape):
    return pl.BlockSpec(shape, lambda *_: (0,) * len(shape), pipeline_mode=pl.Buffered(1))


def _layer_spec(stacked, layer):
    rest = stacked.shape[1:]
    return pl.BlockSpec((None,) + rest, lambda *_: (layer,) + (0,) * len(rest),
                        pipeline_mode=pl.Buffered(1))


def _even_in_kernel(x_ref, ng_ref, w_ref, cw_ref,
                    q1_ref, q2_ref, k_ref, vt_ref, bo_ref, carry_ref, *, tiles_per_seq):
    tm = x_ref.shape[0]
    bw = bo_ref.shape[1]

    @pl.when(pl.program_id(0) % tiles_per_seq == 0)
    def _():
        carry_ref[...] = jnp.zeros_like(carry_ref)

    def proj(h, lo, width):
        return jnp.dot(h, w_ref[:, lo:lo + width], preferred_element_type=F32)

    parts = _row_parts(tm)
    hs = [_rms(x_ref[p, :], ng_ref[0:1, :]).astype(BF16) for p in parts]
    c0 = 3 * A_QK
    b_parts, z_parts = [], []
    for p, h in zip(parts, hs):
        q = proj(h, 0, A_QK) * (DIFF_QK_DIM ** -0.5 * LOG2_E)
        first_map = (lax.broadcasted_iota(jnp.int32, q.shape, 1) % HEAD_WIDTH) < DIFF_QK_DIM
        q1_ref[p, :] = jnp.where(first_map, q, 0.0).astype(BF16)
        q2_ref[p, :] = jnp.where(first_map, 0.0, q).astype(BF16)
        k_ref[p, :] = proj(h, A_QK, A_QK).astype(BF16)
        vt_ref[:, p] = proj(h, 2 * A_QK, A_QK).T.astype(BF16)
        b_parts.append(proj(h, c0, bw))
        z_parts.append(proj(h, c0 + bw, bw) * proj(h, c0 + 2 * bw, bw))
    b_gate = jnp.concatenate(b_parts, axis=0)
    z = jnp.concatenate(z_parts, axis=0)
    row = lax.broadcasted_iota(jnp.int32, z.shape, 0)
    prev = carry_ref[...]
    zm1 = jnp.where(row == 0, prev[SUBLANES - 1:SUBLANES], pltpu.roll(z, 1, 0))
    zm2 = jnp.where(row == 0, prev[SUBLANES - 2:SUBLANES - 1],
                    jnp.where(row == 1, prev[SUBLANES - 1:SUBLANES], pltpu.roll(z, 2, 0)))
    cw = cw_ref[...]
    y = cw[0:1] * zm2 + cw[1:2] * zm1 + cw[2:3] * z
    bo_ref[...] = (b_gate * y).astype(BF16)
    carry_ref[...] = z[tm - SUBLANES:, :]


def _even_in(x2d, norm_g, w_bf, conv_w, seq, layer):
    n, d = x2d.shape
    tm = TOKEN_TILE
    t = ATTN_BLOCK
    per_blk = t // tm
    bw = conv_w.shape[-1]
    kern = functools.partial(_even_in_kernel, tiles_per_seq=seq // tm)
    tok = lambda w: pl.BlockSpec((tm, w), lambda i: (i, 0))
    return pl.pallas_call(
        kern,
        grid=(n // tm,),
        in_specs=[tok(d), _layer_spec(norm_g, layer), _layer_spec(w_bf, layer // 2),
                  _layer_spec(conv_w, layer // 2)],
        out_specs=[tok(A_QK), tok(A_QK), tok(A_QK),
                   pl.BlockSpec((None, A_QK, tm), lambda i: (i // per_blk, 0, i % per_blk)),
                   tok(bw)],
        out_shape=[jax.ShapeDtypeStruct((n, A_QK), BF16)] * 3
        + [jax.ShapeDtypeStruct((n // t, A_QK, t), BF16), jax.ShapeDtypeStruct((n, bw), BF16)],
        scratch_shapes=[pltpu.VMEM((SUBLANES, bw), F32)],
        compiler_params=pltpu.CompilerParams(dimension_semantics=("arbitrary",),
                                             vmem_limit_bytes=VMEM_LIMIT_BYTES),
        name="even_in_proj",
    )(x2d, norm_g, w_bf, conv_w)


_DIAG, _PREV = 0, 1


def _patched(s, tile_ref, patches):
    sb = tile_ref.shape[-1]
    for r, c, idx, scale in patches:
        band = s[r:r + sb, :]
        cols = [band[:, :c]] if c else []
        tile = tile_ref[idx] if scale is None else tile_ref[idx] * scale
        cols.append(band[:, c:c + sb] + tile)
        if c + sb < s.shape[1]:
            cols.append(band[:, c + sb:])
        rows = [s[:r]] if r else []
        rows.append(jnp.concatenate(cols, axis=1) if len(cols) > 1 else cols[0])
        if r + sb < s.shape[0]:
            rows.append(s[r + sb:])
        s = jnp.concatenate(rows, axis=0) if len(rows) > 1 else rows[0]
    return s


def _fold8(x, op):
    return op(x.reshape(x.shape[0] // SUBLANES, SUBLANES, x.shape[1]), axis=0)


def _lpad(x, width, value):
    if not width:
        return x
    return jnp.concatenate([jnp.full((x.shape[0], width), value, x.dtype), x], axis=1)


def _attn_kernel(lam_ref, q1_ref, q2_ref, k_ref, vt_ref, dt_ref, cfar_ref, g_ref, o_ref,
                 sa_ref, sb_ref, ca_ref, cb_ref, m_ref, l_ref, acc_ref, *, lambda_init):
    i = pl.program_id(2)
    t = q1_ref.shape[0]
    sb = dt_ref.shape[-1]
    ns = t // sb
    q_refs = (q1_ref, q2_ref)
    cfar = cfar_ref[...]
    rows = [slice(a * sb, (a + 1) * sb) for a in range(ns)]

    def fill_piece(blk, a, mp, s_ref, cmax):
        s = lax.dot_general(k_ref[blk, rows[a], :], q_refs[mp][...], _NT, preferred_element_type=F32)
        if a == ns - 1:
            is_prev = jnp.where(blk == i - 1, 1.0, 0.0).astype(F32)
            s = _patched(s, dt_ref, [(0, 0, _PREV, is_prev)])
        s_ref[mp, rows[a], :] = s
        pm = _fold8(s, jnp.max)
        return pm if cmax is None else jnp.maximum(cmax, pm)

    def past_step(blk, cur_s, cur_c, nxt_s, nxt_c):
        shift, alpha = [], []
        for mp in range(2):
            m_old = m_ref[mp]
            m_new = jnp.maximum(m_old, cur_c[mp] + cfar)
            m_ref[mp] = m_new
            shift.append(m_new - cfar)
            alpha.append(jnp.exp2(m_old - m_new))
        cmax, lsum, pv = [None, None], [None, None], [None, None]
        for a in range(ns):
            if nxt_s is not None:
                for mp in range(2):
                    cmax[mp] = fill_piece(blk + 1, a, mp, nxt_s, cmax[mp])
            for mp in range(2):
                e = jnp.exp2(cur_s[mp, rows[a], :] - shift[mp])
                ps = _fold8(e, jnp.sum)
                lsum[mp] = ps if lsum[mp] is None else lsum[mp] + ps
                d = jnp.dot(vt_ref[blk, :, rows[a]], e.astype(BF16), preferred_element_type=F32)
                pv[mp] = d if pv[mp] is None else pv[mp] + d
        for mp in range(2):
            if nxt_s is not None:
                nxt_c[mp] = jnp.max(cmax[mp], axis=0, keepdims=True)
            acc_ref[mp] = alpha[mp] * acc_ref[mp] + pv[mp]
            l_ref[mp] = alpha[mp] * l_ref[mp] + lsum[mp]

    def diag_step(fill_s, fill_c):
        strips, smax, cmax = [[None] * ns, [None] * ns], [None, None], [None, None]
        for a in range(ns):
            q_lo = a * sb
            patches = [(0, 0, _DIAG, None)] + ([(0, sb, _PREV, None)] if a + 1 < ns else [])
            for mp in range(2):
                s = lax.dot_general(k_ref[i, rows[a], :], q_refs[mp][q_lo:, :], _NT,
                                    preferred_element_type=F32)
                s = _patched(s, dt_ref, patches)
                strips[mp][a] = s
                pm = _lpad(jnp.max(s, axis=0, keepdims=True), q_lo, MASK_VALUE)
                smax[mp] = pm if smax[mp] is None else jnp.maximum(smax[mp], pm)
            for mp in range(2):
                cmax[mp] = fill_piece(0, a, mp, fill_s, cmax[mp])
        for mp in range(2):
            fill_c[mp] = jnp.max(cmax[mp], axis=0, keepdims=True)
            m_ref[mp] = smax[mp] + cfar
            lsum = None
            for a in range(ns):
                q_lo = a * sb
                e = jnp.exp2(strips[mp][a] - smax[mp][:, q_lo:])
                ps = _lpad(_fold8(e, jnp.sum), q_lo, 0.0)
                lsum = ps if lsum is None else lsum + ps
                d = jnp.dot(vt_ref[i, :, rows[a]], e.astype(BF16), preferred_element_type=F32)
                if a == 0:
                    acc_ref[mp] = d
                else:
                    acc_ref[mp, :, q_lo:] += d
            l_ref[mp] = lsum

    diag_step(sa_ref, ca_ref)

    def pair_body(p, carry):
        past_step(2 * p, sa_ref, ca_ref, sb_ref, cb_ref)
        past_step(2 * p + 1, sb_ref, cb_ref, sa_ref, ca_ref)
        return carry

    lax.fori_loop(0, jnp.maximum(i - 1, 0) // 2, pair_body, 0)

    @pl.when(jnp.logical_and(i >= 2, i % 2 == 0))
    def _():
        past_step(i - 2, sa_ref, ca_ref, sb_ref, cb_ref)
        past_step(i - 1, sb_ref, cb_ref, None, None)

    @pl.when(i % 2 == 1)
    def _():
        past_step(i - 1, sa_ref, ca_ref, None, None)

    lam = lam_ref[...]
    lam_full = (jnp.exp(jnp.sum(lam[0:1] * lam[1:2], axis=1, keepdims=True))
                - jnp.exp(jnp.sum(lam[2:3] * lam[3:4], axis=1, keepdims=True)) + lambda_init)
    inv_l = [1.0 / jnp.sum(l_ref[mp], axis=0, keepdims=True) for mp in range(2)]
    o = acc_ref[0] * inv_l[0] - lam_full * (acc_ref[1] * inv_l[1])
    y = o * lax.rsqrt(jnp.mean(o * o, axis=0, keepdims=True) + SUBLN_EPS)
    o_ref[...] = (y.T * (g_ref[...] * (1.0 - lambda_init))).astype(BF16)


def _diff_attention(lam, q1, q2, k4, vt4, dtiles, cfar, subln_g, lambda_init, layer):
    bsz, nblk, t, _ = k4.shape
    seq = nblk * t
    hw = HEAD_WIDTH
    sb = dtiles.shape[-1]
    kern = functools.partial(_attn_kernel, lambda_init=lambda_init)
    qspec = pl.BlockSpec((None, t, hw), lambda b, h, i: (b, i, h))
    return pl.pallas_call(
        kern,
        grid=(bsz, DIFF_HEADS, nblk),
        in_specs=[
            _layer_spec(lam, layer // 2),
            qspec, qspec,
            pl.BlockSpec((None, nblk, t, hw), lambda b, h, i: (b, 0, 0, h)),
            pl.BlockSpec((None, nblk, hw, t), lambda b, h, i: (b, 0, h, 0)),
            pl.BlockSpec((None, 2, sb, sb), lambda b, h, i: (h, 0, 0, 0)),
            pl.BlockSpec((None, 1, t), lambda b, h, i: (h, 0, 0)),
            _layer_spec(subln_g, layer // 2),
        ],
        out_specs=qspec,
        out_shape=jax.ShapeDtypeStruct((bsz, seq, A_QK), BF16),
        scratch_shapes=[pltpu.VMEM((2, t, t), F32), pltpu.VMEM((2, t, t), F32),
                        pltpu.VMEM((2, 1, t), F32), pltpu.VMEM((2, 1, t), F32),
                        pltpu.VMEM((2, 1, t), F32),
                        pltpu.VMEM((2, SUBLANES, t), F32),
                        pltpu.VMEM((2, hw, t), F32)],
        compiler_params=pltpu.CompilerParams(
            dimension_semantics=("arbitrary", "arbitrary", "arbitrary"),
            vmem_limit_bytes=VMEM_LIMIT_BYTES),
        name="diff_attention",
    )(lam, q1, q2, k4, vt4, dtiles, cfar, subln_g)


def _row_parts(tm):
    step = tm // ROW_PARTS
    return [slice(p * step, (p + 1) * step) for p in range(ROW_PARTS)]


def _residual_ffn(x_ref, mixes, parts, ng_ref, wg_ref, wu_ref, wd_ref, o_ref):
    x1s, h2s, acts = [], [], []
    for p, mix in zip(parts, mixes):
        x1 = x_ref[p, :] + _rms(mix, ng_ref[1:2, :])
        x1s.append(x1)
        h2s.append(_rms(x1, ng_ref[2:3, :]).astype(BF16))
    for h2 in h2s:
        gate = jnp.dot(h2, wg_ref[...], preferred_element_type=F32)
        up = jnp.dot(h2, wu_ref[...], preferred_element_type=F32)
        acts.append((gate * jax.nn.sigmoid(gate) * up).astype(BF16))
    for p, x1, act in zip(parts, x1s, acts):
        f = jnp.dot(act, wd_ref[...], preferred_element_type=F32)
        o_ref[p, :] = x1 + _rms(f, ng_ref[3:4, :])


def _even_out_kernel(x_ref, a_ref, bo_ref, wo_ref, ng_ref, wg_ref, wu_ref, wd_ref, o_ref):
    aw = a_ref.shape[1]
    parts = _row_parts(x_ref.shape[0])
    mixes = [jnp.dot(a_ref[p, :], wo_ref[0:aw, :], preferred_element_type=F32)
             + jnp.dot(bo_ref[p, :], wo_ref[aw:, :], preferred_element_type=F32) for p in parts]
    _residual_ffn(x_ref, mixes, parts, ng_ref, wg_ref, wu_ref, wd_ref, o_ref)


def _even_out(x2d, a2d, bo2d, wo_bf, ng, wg_bf, wu_bf, wd_bf, layer):
    n, d = x2d.shape
    tm = TOKEN_TILE
    tok = lambda w: pl.BlockSpec((tm, w), lambda i: (i, 0))
    return pl.pallas_call(
        _even_out_kernel,
        grid=(n // tm,),
        in_specs=[tok(d), tok(a2d.shape[1]), tok(bo2d.shape[1]), _layer_spec(wo_bf, layer // 2),
                  _layer_spec(ng, layer), _layer_spec(wg_bf, layer), _layer_spec(wu_bf, layer),
                  _layer_spec(wd_bf, layer)],
        out_specs=tok(d),
        out_shape=jax.ShapeDtypeStruct((n, d), F32),
        compiler_params=pltpu.CompilerParams(dimension_semantics=("arbitrary",),
                                             vmem_limit_bytes=VMEM_LIMIT_BYTES),
        name="even_out_ffn",
    )(x2d, a2d, bo2d, wo_bf, ng, wg_bf, wu_bf, wd_bf)


def _odd_kernel(x_ref, wi_ref, lng_ref, lnb_ref, ws_ref, sb_ref, wo_ref, ng_ref,
                wg_ref, wu_ref, wd_ref, o_ref):
    sw = wo_ref.shape[0]
    gd = sw // SGU_GROUPS
    parts = _row_parts(x_ref.shape[0])
    nch = (parts[0].stop - parts[0].start) // CHUNK

    def gelu(z):
        return 0.5 * z * (1.0 + lax.erf(z * math.sqrt(0.5)))

    def gate_inputs(h):
        u = gelu(jnp.dot(h, wi_ref[:, 0:sw], preferred_element_type=F32))
        v = gelu(jnp.dot(h, wi_ref[:, sw:], preferred_element_type=F32))
        mu = jnp.mean(v, axis=-1, keepdims=True)
        vc = v - mu
        v = (vc * lax.rsqrt(jnp.mean(vc * vc, axis=-1, keepdims=True) + LN_EPS) * lng_ref[...]
             + lnb_ref[...]).astype(BF16)
        return u, v

    def spatial_gate(u, v):
        tiles = [[None] * SGU_GROUPS for _ in range(nch)]
        for g in range(SGU_GROUPS):
            rhs = jnp.concatenate([v[n * CHUNK:(n + 1) * CHUNK, g * gd:(g + 1) * gd] for n in range(nch)],
                                  axis=1)
            mixed = jnp.dot(ws_ref[g], rhs, preferred_element_type=F32) + sb_ref[g]
            for n in range(nch):
                tiles[n][g] = (u[n * CHUNK:(n + 1) * CHUNK, g * gd:(g + 1) * gd]
                               * mixed[:, n * gd:(n + 1) * gd])
        gated = jnp.concatenate([jnp.concatenate(r, axis=1) for r in tiles], axis=0).astype(BF16)
        return jnp.dot(gated, wo_ref[...], preferred_element_type=F32)

    hs = [_rms(x_ref[p, :], ng_ref[0:1, :]).astype(BF16) for p in parts]
    uvs = [gate_inputs(h) for h in hs]
    mixes = [spatial_gate(u, v) for u, v in uvs]
    _residual_ffn(x_ref, mixes, parts, ng_ref, wg_ref, wu_ref, wd_ref, o_ref)


def _odd_layer(x2d, wi_bf, ln_g, ln_b, ws_bf, sb, wo_bf, ng, wg_bf, wu_bf, wd_bf, layer):
    n, d = x2d.shape
    tm = TOKEN_TILE
    tok = pl.BlockSpec((tm, d), lambda i: (i, 0))
    per_odd = (wi_bf, ln_g, ln_b, ws_bf, sb, wo_bf)
    per_layer = (ng, wg_bf, wu_bf, wd_bf)
    consts = per_odd + per_layer
    return pl.pallas_call(
        _odd_kernel,
        grid=(n // tm,),
        in_specs=[tok] + [_layer_spec(a, layer // 2) for a in per_odd]
        + [_layer_spec(a, layer) for a in per_layer],
        out_specs=tok,
        out_shape=jax.ShapeDtypeStruct((n, d), F32),
        compiler_params=pltpu.CompilerParams(dimension_semantics=("arbitrary",),
                                             vmem_limit_bytes=VMEM_LIMIT_BYTES),
        name="odd_sgu_ffn",
    )(x2d, *consts)


def _t5_bucket_of_distance(n):
    max_exact = REL_BUCKETS // 2
    nf = jnp.maximum(n, 1).astype(F32)
    large = max_exact + (jnp.log(nf / max_exact) / math.log(REL_MAX_DIST / max_exact)
                         * (REL_BUCKETS - max_exact)).astype(jnp.int32)
    large = jnp.minimum(large, REL_BUCKETS - 1)
    return jnp.where(n < max_exact, n, large)


def _bias_tiles(rel_bias, sb, t):
    assert sb >= REL_MAX_DIST
    table = rel_bias.astype(F32)
    key = jnp.arange(sb, dtype=jnp.int32)[:, None]
    qry = jnp.arange(sb, dtype=jnp.int32)[None, :]

    def lookup(dist):
        hit = _t5_bucket_of_distance(dist)[None, :, :, None] == jnp.arange(REL_BUCKETS)[:, None, None, None]
        return jnp.sum(jnp.where(hit, table[:, None, None, :], 0.0), axis=0)

    far = table[REL_BUCKETS - 1]
    diag = jnp.where((qry >= key)[..., None], (lookup(jnp.maximum(qry - key, 0)) - far) * LOG2_E, MASK_VALUE)
    prev = (lookup(sb + qry - key) - far) * LOG2_E
    tiles = jnp.transpose(jnp.stack([diag, prev]), (3, 0, 1, 2))
    cfar = jnp.broadcast_to((far * LOG2_E)[:, None, None], (rel_bias.shape[1], 1, t))
    return tiles, cfar


def kernel(x, rel_bias, w_in_even, diff_lambda, diff_subln_g, conv_w, w_out_even, w_in_odd,
           sgu_ln_g, sgu_ln_b, sgu_w, sgu_b, w_out_odd, norm_g, w_gate, w_up, w_down):
    bsz, seq, d = x.shape
    depth = norm_g.shape[0]
    t = ATTN_BLOCK
    assert seq % t == 0 and t % TOKEN_TILE == 0 and t % BIAS_TILE == 0
    assert TOKEN_TILE % (ROW_PARTS * CHUNK) == 0
    x2d = x.reshape(bsz * seq, d)
    dtiles, cfar = _bias_tiles(rel_bias, BIAS_TILE, t)
    tril = jnp.tril(jnp.ones((CHUNK, CHUNK), dtype=bool))
    wg, wu, wd = w_gate.astype(BF16), w_up.astype(BF16), w_down.astype(BF16)
    w_in_even_bf, w_out_even_bf = w_in_even.astype(BF16), w_out_even.astype(BF16)
    w_in_odd_bf, w_out_odd_bf = w_in_odd.astype(BF16), w_out_odd.astype(BF16)
    ws = jnp.where(tril, sgu_w, 0.0).astype(BF16)
    for i in range(depth):
        if i % 2 == 0:
            lambda_init = 0.8 - 0.6 * math.exp(-0.3 * i)
            q1, q2, k, vt, bo = _even_in(x2d, norm_g, w_in_even_bf, conv_w, seq, i)
            a = _diff_attention(
                diff_lambda, q1.reshape(bsz, seq, A_QK), q2.reshape(bsz, seq, A_QK),
                k.reshape(bsz, seq // t, t, A_QK), vt.reshape(bsz, seq // t, A_QK, t),
                dtiles, cfar, diff_subln_g[:, None, :], lambda_init, i)
            x2d = _even_out(x2d, a.reshape(bsz * seq, A_QK), bo, w_out_even_bf, norm_g, wg, wu, wd, i)
        else:
            x2d = _odd_layer(x2d, w_in_odd_bf, sgu_ln_g[:, None, :], sgu_ln_b[:, None, :], ws,
                             sgu_b[..., None], w_out_odd_bf, norm_g, wg, wu, wd, i)
    return x2d.reshape(bsz, seq, d)
```

```python
import functools
import math

import jax
import jax.numpy as jnp
from jax import lax
from jax.experimental import pallas as pl
from jax.experimental.pallas import tpu as pltpu

F32 = jnp.float32
BF16 = jnp.bfloat16

DIFF_HEADS = 4
DIFF_QK_DIM = 64
HEAD_WIDTH = 2 * DIFF_QK_DIM
A_QK = DIFF_HEADS * HEAD_WIDTH
CONV_WIDTH = 3
SGU_GROUPS = 8
CHUNK = 128
REL_BUCKETS = 32
REL_MAX_DIST = 128
RMS_EPS = 1e-6
SUBLN_EPS = 1e-5
LN_EPS = 1e-5
MASK_VALUE = -1e30
LOG2_E = math.log2(math.e)

SUBLANES = 8
BF16_SUBLANES = 16
TOKEN_TILE = 512
ROW_PARTS = 2
ATTN_BLOCK = 1024
BIAS_TILE = 256
VMEM_LIMIT_BYTES = 56 * 1024 * 1024

_NT = (((1,), (1,)), ((), ()))


def _rms(x, g, eps=RMS_EPS):
    return x * lax.rsqrt(jnp.mean(x * x, axis=-1, keepdims=True) + eps) * g


def _const_spec(shape):
    return pl.BlockSpec(shape, lambda *_: (0,) * len(shape), pipeline_mode=pl.Buffered(1))


def _layer_spec(stacked, layer):
    rest = stacked.shape[1:]
    return pl.BlockSpec((None,) + rest, lambda *_: (layer,) + (0,) * len(rest),
                        pipeline_mode=pl.Buffered(1))


class _CastPlan:
    def __init__(self, weights, nsteps, step_of):
        self.arrays = [w for w, _ in weights]
        self.in_specs, self.out_specs, self.out_shapes = [], [], []
        for w, layer in weights:
            _, r, c = w.shape
            slab = next(s for s in range(BF16_SUBLANES, r + 1, BF16_SUBLANES)
                        if r % s == 0 and nsteps % (r // s) == 0)
            per_slab = nsteps // (r // slab)
            self.in_specs.append(pl.BlockSpec(
                (None, slab, c), lambda *g, layer=layer, per=per_slab: (layer, step_of(*g) // per, 0)))
            self.out_specs.append(pl.BlockSpec(
                (slab, c), lambda *g, per=per_slab: (step_of(*g) // per, 0)))
            self.out_shapes.append(jax.ShapeDtypeStruct((r, c), BF16))

    def __len__(self):
        return len(self.arrays)


def _cast_slabs(in_refs, out_refs):
    for src, dst in zip(in_refs, out_refs):
        dst[...] = src[...].astype(BF16)


def _even_in_kernel(x_ref, ng_ref, w_ref, cw_ref, *refs, tiles_per_seq, n_cast):
    cast_in, refs = refs[:n_cast], refs[n_cast:]
    q1_ref, q2_ref, k_ref, vt_ref, bo_ref = refs[:5]
    cast_out, (carry_ref,) = refs[5:5 + n_cast], refs[5 + n_cast:]
    _cast_slabs(cast_in, cast_out)
    tm = x_ref.shape[0]
    bw = bo_ref.shape[1]

    @pl.when(pl.program_id(0) % tiles_per_seq == 0)
    def _():
        carry_ref[...] = jnp.zeros_like(carry_ref)

    def proj(h, lo, width):
        return jnp.dot(h, w_ref[:, lo:lo + width], preferred_element_type=F32)

    parts = _row_parts(tm)
    hs = [_rms(x_ref[p, :], ng_ref[0:1, :]).astype(BF16) for p in parts]
    c0 = 3 * A_QK
    b_parts, z_parts = [], []
    for p, h in zip(parts, hs):
        q = proj(h, 0, A_QK) * (DIFF_QK_DIM ** -0.5 * LOG2_E)
        first_map = (lax.broadcasted_iota(jnp.int32, q.shape, 1) % HEAD_WIDTH) < DIFF_QK_DIM
        q1_ref[p, :] = jnp.where(first_map, q, 0.0).astype(BF16)
        q2_ref[p, :] = jnp.where(first_map, 0.0, q).astype(BF16)
        k_ref[p, :] = proj(h, A_QK, A_QK).astype(BF16)
        vt_ref[:, p] = proj(h, 2 * A_QK, A_QK).T.astype(BF16)
        b_parts.append(proj(h, c0, bw))
        z_parts.append(proj(h, c0 + bw, bw) * proj(h, c0 + 2 * bw, bw))
    b_gate = jnp.concatenate(b_parts, axis=0)
    z = jnp.concatenate(z_parts, axis=0)
    row = lax.broadcasted_iota(jnp.int32, z.shape, 0)
    prev = carry_ref[...]
    zm1 = jnp.where(row == 0, prev[SUBLANES - 1:SUBLANES], pltpu.roll(z, 1, 0))
    zm2 = jnp.where(row == 0, prev[SUBLANES - 2:SUBLANES - 1],
                    jnp.where(row == 1, prev[SUBLANES - 1:SUBLANES], pltpu.roll(z, 2, 0)))
    cw = cw_ref[...]
    y = cw[0:1] * zm2 + cw[1:2] * zm1 + cw[2:3] * z
    bo_ref[...] = (b_gate * y).astype(BF16)
    carry_ref[...] = z[tm - SUBLANES:, :]


def _even_in(x2d, norm_g, w_bf, conv_w, seq, layer, to_cast):
    n, d = x2d.shape
    tm = TOKEN_TILE
    t = ATTN_BLOCK
    per_blk = t // tm
    bw = conv_w.shape[-1]
    casts = _CastPlan(to_cast, n // tm, lambda i: i)
    kern = functools.partial(_even_in_kernel, tiles_per_seq=seq // tm, n_cast=len(casts))
    tok = lambda w: pl.BlockSpec((tm, w), lambda i: (i, 0))
    return pl.pallas_call(
        kern,
        grid=(n // tm,),
        in_specs=[tok(d), _layer_spec(norm_g, layer), _layer_spec(w_bf, layer // 2),
                  _layer_spec(conv_w, layer // 2)] + casts.in_specs,
        out_specs=[tok(A_QK), tok(A_QK), tok(A_QK),
                   pl.BlockSpec((None, A_QK, tm), lambda i: (i // per_blk, 0, i % per_blk)),
                   tok(bw)] + casts.out_specs,
        out_shape=[jax.ShapeDtypeStruct((n, A_QK), BF16)] * 3
        + [jax.ShapeDtypeStruct((n // t, A_QK, t), BF16), jax.ShapeDtypeStruct((n, bw), BF16)]
        + casts.out_shapes,
        scratch_shapes=[pltpu.VMEM((SUBLANES, bw), F32)],
        compiler_params=pltpu.CompilerParams(dimension_semantics=("arbitrary",),
                                             vmem_limit_bytes=VMEM_LIMIT_BYTES),
        name="even_in_proj",
    )(x2d, norm_g, w_bf, conv_w, *casts.arrays)


_DIAG, _PREV = 0, 1


def _patched(s, tile_ref, patches):
    sb = tile_ref.shape[-1]
    for r, c, idx, scale in patches:
        band = s[r:r + sb, :]
        cols = [band[:, :c]] if c else []
        tile = tile_ref[idx] if scale is None else tile_ref[idx] * scale
        cols.append(band[:, c:c + sb] + tile)
        if c + sb < s.shape[1]:
            cols.append(band[:, c + sb:])
        rows = [s[:r]] if r else []
        rows.append(jnp.concatenate(cols, axis=1) if len(cols) > 1 else cols[0])
        if r + sb < s.shape[0]:
            rows.append(s[r + sb:])
        s = jnp.concatenate(rows, axis=0) if len(rows) > 1 else rows[0]
    return s


def _fold8(x, op):
    return op(x.reshape(x.shape[0] // SUBLANES, SUBLANES, x.shape[1]), axis=0)


def _lpad(x, width, value):
    if not width:
        return x
    return jnp.concatenate([jnp.full((x.shape[0], width), value, x.dtype), x], axis=1)


def _attn_kernel(lam_ref, q1_ref, q2_ref, k_ref, vt_ref, dt_ref, cfar_ref, g_ref, *refs,
                 lambda_init, n_cast):
    cast_in, o_ref, cast_out = refs[:n_cast], refs[n_cast], refs[n_cast + 1:2 * n_cast + 1]
    sa_ref, sb_ref, ca_ref, cb_ref, m_ref, l_ref, acc_ref = refs[2 * n_cast + 1:]
    _cast_slabs(cast_in, cast_out)
    i = pl.program_id(2)
    t = q1_ref.shape[0]
    sb = dt_ref.shape[-1]
    ns = t // sb
    q_refs = (q1_ref, q2_ref)
    cfar = cfar_ref[...]
    rows = [slice(a * sb, (a + 1) * sb) for a in range(ns)]

    def fill_piece(blk, a, mp, s_ref, cmax):
        s = lax.dot_general(k_ref[blk, rows[a], :], q_refs[mp][...], _NT, preferred_element_type=F32)
        if a == ns - 1:
            is_prev = jnp.where(blk == i - 1, 1.0, 0.0).astype(F32)
            s = _patched(s, dt_ref, [(0, 0, _PREV, is_prev)])
        s_ref[mp, rows[a], :] = s
        pm = _fold8(s, jnp.max)
        return pm if cmax is None else jnp.maximum(cmax, pm)

    def past_step(blk, cur_s, cur_c, nxt_s, nxt_c):
        shift, alpha = [], []
        for mp in range(2):
            m_old = m_ref[mp]
            m_new = jnp.maximum(m_old, cur_c[mp] + cfar)
            m_ref[mp] = m_new
            shift.append(m_new - cfar)
            alpha.append(jnp.exp2(m_old - m_new))
        cmax, lsum, pv = [None, None], [None, None], [None, None]
        for a in range(ns):
            for mp in range(2):
                e = jnp.exp2(cur_s[mp, rows[a], :] - shift[mp])
                ps = _fold8(e, jnp.sum)
                lsum[mp] = ps if lsum[mp] is None else lsum[mp] + ps
                if nxt_s is not None:
                    cmax[mp] = fill_piece(blk + 1, a, mp, nxt_s, cmax[mp])
                d = jnp.dot(vt_ref[blk, :, rows[a]], e.astype(BF16), preferred_element_type=F32)
                pv[mp] = d if pv[mp] is None else pv[mp] + d
        for mp in range(2):
            if nxt_s is not None:
                nxt_c[mp] = jnp.max(cmax[mp], axis=0, keepdims=True)
            acc_ref[mp] = alpha[mp] * acc_ref[mp] + pv[mp]
            l_ref[mp] = alpha[mp] * l_ref[mp] + lsum[mp]

    def diag_step(fill_s, fill_c):
        strips, smax, cmax = [[None] * ns, [None] * ns], [None, None], [None, None]
        for a in range(ns):
            q_lo = a * sb
            patches = [(0, 0, _DIAG, None)] + ([(0, sb, _PREV, None)] if a + 1 < ns else [])
            for mp in range(2):
                s = lax.dot_general(k_ref[i, rows[a], :], q_refs[mp][q_lo:, :], _NT,
                                    preferred_element_type=F32)
                s = _patched(s, dt_ref, patches)
                strips[mp][a] = s
                pm = _lpad(jnp.max(s, axis=0, keepdims=True), q_lo, MASK_VALUE)
                smax[mp] = pm if smax[mp] is None else jnp.maximum(smax[mp], pm)
        lsum = [None, None]
        for a in range(ns):
            q_lo = a * sb
            for mp in range(2):
                e = jnp.exp2(strips[mp][a] - smax[mp][:, q_lo:])
                ps = _lpad(_fold8(e, jnp.sum), q_lo, 0.0)
                lsum[mp] = ps if lsum[mp] is None else lsum[mp] + ps
                if fill_s is not None:
                    cmax[mp] = fill_piece(0, a, mp, fill_s, cmax[mp])
                d = jnp.dot(vt_ref[i, :, rows[a]], e.astype(BF16), preferred_element_type=F32)
                if a == 0:
                    acc_ref[mp] = d
                else:
                    acc_ref[mp, :, q_lo:] += d
        for mp in range(2):
            if fill_s is not None:
                fill_c[mp] = jnp.max(cmax[mp], axis=0, keepdims=True)
            m_ref[mp] = smax[mp] + cfar
            l_ref[mp] = lsum[mp]

    @pl.when(i == 0)
    def _():
        diag_step(None, None)

    @pl.when(i > 0)
    def _():
        diag_step(sa_ref, ca_ref)

    def pair_body(p, carry):
        past_step(2 * p, sa_ref, ca_ref, sb_ref, cb_ref)
        past_step(2 * p + 1, sb_ref, cb_ref, sa_ref, ca_ref)
        return carry

    lax.fori_loop(0, jnp.maximum(i - 1, 0) // 2, pair_body, 0)

    @pl.when(jnp.logical_and(i >= 2, i % 2 == 0))
    def _():
        past_step(i - 2, sa_ref, ca_ref, sb_ref, cb_ref)
        past_step(i - 1, sb_ref, cb_ref, None, None)

    @pl.when(i % 2 == 1)
    def _():
        past_step(i - 1, sa_ref, ca_ref, None, None)

    lam = lam_ref[...]
    lam_full = (jnp.exp(jnp.sum(lam[0:1] * lam[1:2], axis=1, keepdims=True))
                - jnp.exp(jnp.sum(lam[2:3] * lam[3:4], axis=1, keepdims=True)) + lambda_init)
    inv_l = [1.0 / jnp.sum(l_ref[mp], axis=0, keepdims=True) for mp in range(2)]
    o = acc_ref[0] * inv_l[0] - lam_full * (acc_ref[1] * inv_l[1])
    y = o * lax.rsqrt(jnp.mean(o * o, axis=0, keepdims=True) + SUBLN_EPS)
    o_ref[...] = (y.T * (g_ref[...] * (1.0 - lambda_init))).astype(BF16)


def _diff_attention(lam, q1, q2, k4, vt4, dtiles, cfar, subln_g, lambda_init, layer, to_cast):
    bsz, nblk, t, _ = k4.shape
    seq = nblk * t
    hw = HEAD_WIDTH
    sb = dtiles.shape[-1]
    casts = _CastPlan(to_cast, bsz * DIFF_HEADS * nblk,
                      lambda b, h, i: (b * DIFF_HEADS + h) * nblk + i)
    kern = functools.partial(_attn_kernel, lambda_init=lambda_init, n_cast=len(casts))
    qspec = pl.BlockSpec((None, t, hw), lambda b, h, i: (b, i, h))
    return pl.pallas_call(
        kern,
        grid=(bsz, DIFF_HEADS, nblk),
        in_specs=[
            _layer_spec(lam, layer // 2),
            qspec, qspec,
            pl.BlockSpec((None, nblk, t, hw), lambda b, h, i: (b, 0, 0, h)),
            pl.BlockSpec((None, nblk, hw, t), lambda b, h, i: (b, 0, h, 0)),
            pl.BlockSpec((None, 2, sb, sb), lambda b, h, i: (h, 0, 0, 0)),
            pl.BlockSpec((None, 1, t), lambda b, h, i: (h, 0, 0)),
            _layer_spec(subln_g, layer // 2),
        ] + casts.in_specs,
        out_specs=[qspec] + casts.out_specs,
        out_shape=[jax.ShapeDtypeStruct((bsz, seq, A_QK), BF16)] + casts.out_shapes,
        scratch_shapes=[pltpu.VMEM((2, t, t), F32), pltpu.VMEM((2, t, t), F32),
                        pltpu.VMEM((2, 1, t), F32), pltpu.VMEM((2, 1, t), F32),
                        pltpu.VMEM((2, 1, t), F32),
                        pltpu.VMEM((2, SUBLANES, t), F32),
                        pltpu.VMEM((2, hw, t), F32)],
        compiler_params=pltpu.CompilerParams(
            dimension_semantics=("arbitrary", "arbitrary", "arbitrary"),
            vmem_limit_bytes=VMEM_LIMIT_BYTES),
        name="diff_attention",
    )(lam, q1, q2, k4, vt4, dtiles, cfar, subln_g, *casts.arrays)


def _row_parts(tm):
    step = tm // ROW_PARTS
    return [slice(p * step, (p + 1) * step) for p in range(ROW_PARTS)]


def _residual_ffn(x_ref, mixes, parts, ng_ref, wg_ref, wu_ref, wd_ref, o_ref):
    x1s, h2s, acts = [], [], []
    for p, mix in zip(parts, mixes):
        x1 = x_ref[p, :] + _rms(mix, ng_ref[1:2, :])
        x1s.append(x1)
        h2s.append(_rms(x1, ng_ref[2:3, :]).astype(BF16))
    for h2 in h2s:
        gate = jnp.dot(h2, wg_ref[...], preferred_element_type=F32)
        up = jnp.dot(h2, wu_ref[...], preferred_element_type=F32)
        acts.append((gate * jax.nn.sigmoid(gate) * up).astype(BF16))
    for p, x1, act in zip(parts, x1s, acts):
        f = jnp.dot(act, wd_ref[...], preferred_element_type=F32)
        o_ref[p, :] = x1 + _rms(f, ng_ref[3:4, :])


def _even_out_kernel(x_ref, a_ref, bo_ref, wo_ref, ng_ref, wg_ref, wu_ref, wd_ref, o_ref):
    aw = a_ref.shape[1]
    parts = _row_parts(x_ref.shape[0])
    mixes = [jnp.dot(a_ref[p, :], wo_ref[0:aw, :], preferred_element_type=F32)
             + jnp.dot(bo_ref[p, :], wo_ref[aw:, :], preferred_element_type=F32) for p in parts]
    _residual_ffn(x_ref, mixes, parts, ng_ref, wg_ref, wu_ref, wd_ref, o_ref)


def _even_out(x2d, a2d, bo2d, wo_bf, ng, wg_bf, wu_bf, wd_bf, layer):
    n, d = x2d.shape
    tm = TOKEN_TILE
    tok = lambda w: pl.BlockSpec((tm, w), lambda i: (i, 0))
    return pl.pallas_call(
        _even_out_kernel,
        grid=(n // tm,),
        in_specs=[tok(d), tok(a2d.shape[1]), tok(bo2d.shape[1]), _const_spec(wo_bf.shape),
                  _layer_spec(ng, layer), _const_spec(wg_bf.shape), _const_spec(wu_bf.shape),
                  _const_spec(wd_bf.shape)],
        out_specs=tok(d),
        out_shape=jax.ShapeDtypeStruct((n, d), F32),
        compiler_params=pltpu.CompilerParams(dimension_semantics=("arbitrary",),
                                             vmem_limit_bytes=VMEM_LIMIT_BYTES),
        name="even_out_ffn",
    )(x2d, a2d, bo2d, wo_bf, ng, wg_bf, wu_bf, wd_bf)


def _odd_kernel(x_ref, wi_ref, lng_ref, lnb_ref, ws_ref, sb_ref, wo_ref, ng_ref,
                wg_ref, wu_ref, wd_ref, o_ref):
    sw = wo_ref.shape[0]
    gd = sw // SGU_GROUPS
    parts = _row_parts(x_ref.shape[0])
    nch = (parts[0].stop - parts[0].start) // CHUNK

    def gelu(z):
        return 0.5 * z * (1.0 + lax.erf(z * math.sqrt(0.5)))

    def gate_inputs(h):
        u = gelu(jnp.dot(h, wi_ref[:, 0:sw], preferred_element_type=F32))
        v = gelu(jnp.dot(h, wi_ref[:, sw:], preferred_element_type=F32))
        mu = jnp.mean(v, axis=-1, keepdims=True)
        vc = v - mu
        v = (vc * lax.rsqrt(jnp.mean(vc * vc, axis=-1, keepdims=True) + LN_EPS) * lng_ref[...]
             + lnb_ref[...]).astype(BF16)
        return u, v

    def spatial_gate(u, v):
        tiles = [[None] * SGU_GROUPS for _ in range(nch)]
        for g in range(SGU_GROUPS):
            rhs = jnp.concatenate([v[n * CHUNK:(n + 1) * CHUNK, g * gd:(g + 1) * gd] for n in range(nch)],
                                  axis=1)
            mixed = jnp.dot(ws_ref[g], rhs, preferred_element_type=F32) + sb_ref[g]
            for n in range(nch):
                tiles[n][g] = (u[n * CHUNK:(n + 1) * CHUNK, g * gd:(g + 1) * gd]
                               * mixed[:, n * gd:(n + 1) * gd])
        gated = jnp.concatenate([jnp.concatenate(r, axis=1) for r in tiles], axis=0).astype(BF16)
        return jnp.dot(gated, wo_ref[...], preferred_element_type=F32)

    hs = [_rms(x_ref[p, :], ng_ref[0:1, :]).astype(BF16) for p in parts]
    uvs = [gate_inputs(h) for h in hs]
    mixes = [spatial_gate(u, v) for u, v in uvs]
    _residual_ffn(x_ref, mixes, parts, ng_ref, wg_ref, wu_ref, wd_ref, o_ref)


def _odd_layer(x2d, wi_bf, ln_g, ln_b, ws_bf, sb, wo_bf, ng, wg_bf, wu_bf, wd_bf, layer):
    n, d = x2d.shape
    tm = TOKEN_TILE
    tok = pl.BlockSpec((tm, d), lambda i: (i, 0))
    consts = (wi_bf, ln_g, ln_b, ws_bf, sb, wo_bf, ng, wg_bf, wu_bf, wd_bf)
    whole = lambda a: _const_spec(a.shape)
    odd = lambda a: _layer_spec(a, layer // 2)
    return pl.pallas_call(
        _odd_kernel,
        grid=(n // tm,),
        in_specs=[tok, whole(wi_bf), odd(ln_g), odd(ln_b), odd(ws_bf), odd(sb), whole(wo_bf),
                  _layer_spec(ng, layer), whole(wg_bf), whole(wu_bf), whole(wd_bf)],
        out_specs=tok,
        out_shape=jax.ShapeDtypeStruct((n, d), F32),
        compiler_params=pltpu.CompilerParams(dimension_semantics=("arbitrary",),
                                             vmem_limit_bytes=VMEM_LIMIT_BYTES),
        name="odd_sgu_ffn",
    )(x2d, *consts)


def _t5_bucket_of_distance(n):
    max_exact = REL_BUCKETS // 2
    nf = jnp.maximum(n, 1).astype(F32)
    large = max_exact + (jnp.log(nf / max_exact) / math.log(REL_MAX_DIST / max_exact)
                         * (REL_BUCKETS - max_exact)).astype(jnp.int32)
    large = jnp.minimum(large, REL_BUCKETS - 1)
    return jnp.where(n < max_exact, n, large)


def _bias_tiles(rel_bias, sb, t):
    assert sb >= REL_MAX_DIST
    table = rel_bias.astype(F32)
    key = jnp.arange(sb, dtype=jnp.int32)[:, None]
    qry = jnp.arange(sb, dtype=jnp.int32)[None, :]

    def lookup(dist):
        hit = _t5_bucket_of_distance(dist)[None, :, :, None] == jnp.arange(REL_BUCKETS)[:, None, None, None]
        return jnp.sum(jnp.where(hit, table[:, None, None, :], 0.0), axis=0)

    far = table[REL_BUCKETS - 1]
    diag = jnp.where((qry >= key)[..., None], (lookup(jnp.maximum(qry - key, 0)) - far) * LOG2_E, MASK_VALUE)
    prev = (lookup(sb + qry - key) - far) * LOG2_E
    tiles = jnp.transpose(jnp.stack([diag, prev]), (3, 0, 1, 2))
    cfar = jnp.broadcast_to((far * LOG2_E)[:, None, None], (rel_bias.shape[1], 1, t))
    return tiles, cfar


def kernel(x, rel_bias, w_in_even, diff_lambda, diff_subln_g, conv_w, w_out_even, w_in_odd,
           sgu_ln_g, sgu_ln_b, sgu_w, sgu_b, w_out_odd, norm_g, w_gate, w_up, w_down):
    bsz, seq, d = x.shape
    depth = norm_g.shape[0]
    t = ATTN_BLOCK
    assert seq % t == 0 and t % TOKEN_TILE == 0 and t % BIAS_TILE == 0
    assert TOKEN_TILE % (ROW_PARTS * CHUNK) == 0
    x2d = x.reshape(bsz * seq, d)
    dtiles, cfar = _bias_tiles(rel_bias, BIAS_TILE, t)
    tril = jnp.tril(jnp.ones((CHUNK, CHUNK), dtype=bool))
    w_in_even_bf = w_in_even.astype(BF16)
    ws = jnp.where(tril, sgu_w, 0.0).astype(BF16)
    ffn = lambda i: [(w_gate, i), (w_up, i), (w_down, i)]
    odd_bf = None
    for i in range(depth):
        if i % 2 == 0:
            lambda_init = 0.8 - 0.6 * math.exp(-0.3 * i)
            q1, q2, k, vt, bo, wo_bf, wg, wu, wd = _even_in(
                x2d, norm_g, w_in_even_bf, conv_w, seq, i, [(w_out_even, i // 2)] + ffn(i))
            nxt = [(w_in_odd, i // 2), (w_out_odd, i // 2)] + ffn(i + 1) if i + 1 < depth else []
            a, *odd_bf = _diff_attention(
                diff_lambda, q1.reshape(bsz, seq, A_QK), q2.reshape(bsz, seq, A_QK),
                k.reshape(bsz, seq // t, t, A_QK), vt.reshape(bsz, seq // t, A_QK, t),
                dtiles, cfar, diff_subln_g[:, None, :], lambda_init, i, nxt)
            x2d = _even_out(x2d, a.reshape(bsz * seq, A_QK), bo, wo_bf, norm_g, wg, wu, wd, i)
        else:
            wi_bf, wo_bf, wg, wu, wd = odd_bf
            x2d = _odd_layer(x2d, wi_bf, sgu_ln_g[:, None, :], sgu_ln_b[:, None, :], ws,
                             sgu_b[..., None], wo_bf, norm_g, wg, wu, wd, i)
    return x2d.reshape(bsz, seq, d)
```

```python
import functools
import math

import jax
import jax.numpy as jnp
from jax import lax
from jax.experimental import pallas as pl
from jax.experimental.pallas import tpu as pltpu

F32 = jnp.float32
BF16 = jnp.bfloat16

DIFF_HEADS = 4
DIFF_QK_DIM = 64
HEAD_WIDTH = 2 * DIFF_QK_DIM
A_QK = DIFF_HEADS * HEAD_WIDTH
CONV_WIDTH = 3
SGU_GROUPS = 8
CHUNK = 128
REL_BUCKETS = 32
REL_MAX_DIST = 128
RMS_EPS = 1e-6
SUBLN_EPS = 1e-5
LN_EPS = 1e-5
MASK_VALUE = -1e30
LOG2_E = math.log2(math.e)

SUBLANES = 8
BF16_SUBLANES = 16
TOKEN_TILE = 512
ROW_PARTS = 2
ATTN_BLOCK = 1024
BIAS_TILE = 256
VMEM_LIMIT_BYTES = 56 * 1024 * 1024

_NT = (((1,), (1,)), ((), ()))


def _rms(x, g, eps=RMS_EPS):
    return x * lax.rsqrt(jnp.mean(x * x, axis=-1, keepdims=True) + eps) * g


def _const_spec(shape):
    return pl.BlockSpec(shape, lambda *_: (0,) * len(shape), pipeline_mode=pl.Buffered(1))


def _layer_spec(stacked, layer):
    rest = stacked.shape[1:]
    return pl.BlockSpec((None,) + rest, lambda *_: (layer,) + (0,) * len(rest),
                        pipeline_mode=pl.Buffered(1))


class _CastPlan:
    def __init__(self, weights, nsteps, step_of):
        self.arrays = [w for w, _ in weights]
        self.in_specs, self.out_specs, self.out_shapes = [], [], []
        for w, layer in weights:
            _, r, c = w.shape
            slab = next(s for s in range(BF16_SUBLANES, r + 1, BF16_SUBLANES)
                        if r % s == 0 and nsteps % (r // s) == 0)
            per_slab = nsteps // (r // slab)
            self.in_specs.append(pl.BlockSpec(
                (None, slab, c), lambda *g, layer=layer, per=per_slab: (layer, step_of(*g) // per, 0)))
            self.out_specs.append(pl.BlockSpec(
                (slab, c), lambda *g, per=per_slab: (step_of(*g) // per, 0)))
            self.out_shapes.append(jax.ShapeDtypeStruct((r, c), BF16))

    def __len__(self):
        return len(self.arrays)


def _cast_slabs(in_refs, out_refs):
    for src, dst in zip(in_refs, out_refs):
        dst[...] = src[...].astype(BF16)


def _even_in_kernel(x_ref, ng_ref, w_ref, cw_ref, *refs, tiles_per_seq, n_cast):
    cast_in, refs = refs[:n_cast], refs[n_cast:]
    q1_ref, q2_ref, k_ref, vt_ref, bo_ref = refs[:5]
    cast_out, (carry_ref,) = refs[5:5 + n_cast], refs[5 + n_cast:]
    _cast_slabs(cast_in, cast_out)
    tm = x_ref.shape[0]
    bw = bo_ref.shape[1]

    @pl.when(pl.program_id(0) % tiles_per_seq == 0)
    def _():
        carry_ref[...] = jnp.zeros_like(carry_ref)

    def proj(h, lo, width):
        return jnp.dot(h, w_ref[:, lo:lo + width], preferred_element_type=F32)

    parts = _row_parts(tm)
    hs = [_rms(x_ref[p, :], ng_ref[0:1, :]).astype(BF16) for p in parts]
    c0 = 3 * A_QK
    cw = cw_ref[...]
    prev = carry_ref[...]
    for p, h in zip(parts, hs):
        q = proj(h, 0, A_QK) * (DIFF_QK_DIM ** -0.5 * LOG2_E)
        first_map = (lax.broadcasted_iota(jnp.int32, q.shape, 1) % HEAD_WIDTH) < DIFF_QK_DIM
        q1_ref[p, :] = jnp.where(first_map, q, 0.0).astype(BF16)
        q2_ref[p, :] = jnp.where(first_map, 0.0, q).astype(BF16)
        k_ref[p, :] = proj(h, A_QK, A_QK).astype(BF16)
        vt_ref[:, p] = proj(h, 2 * A_QK, A_QK).T.astype(BF16)
        b_gate = proj(h, c0, bw)
        z = proj(h, c0 + bw, bw) * proj(h, c0 + 2 * bw, bw)
        row = lax.broadcasted_iota(jnp.int32, z.shape, 0)
        zm1 = jnp.where(row == 0, prev[SUBLANES - 1:SUBLANES], pltpu.roll(z, 1, 0))
        zm2 = jnp.where(row == 0, prev[SUBLANES - 2:SUBLANES - 1],
                        jnp.where(row == 1, prev[SUBLANES - 1:SUBLANES], pltpu.roll(z, 2, 0)))
        y = cw[0:1] * zm2 + cw[1:2] * zm1 + cw[2:3] * z
        bo_ref[p, :] = (b_gate * y).astype(BF16)
        prev = z[z.shape[0] - SUBLANES:, :]
    carry_ref[...] = prev


def _even_in(x2d, norm_g, w_bf, conv_w, seq, layer, to_cast):
    n, d = x2d.shape
    tm = TOKEN_TILE
    t = ATTN_BLOCK
    per_blk = t // tm
    bw = conv_w.shape[-1]
    casts = _CastPlan(to_cast, n // tm, lambda i: i)
    kern = functools.partial(_even_in_kernel, tiles_per_seq=seq // tm, n_cast=len(casts))
    tok = lambda w: pl.BlockSpec((tm, w), lambda i: (i, 0))
    return pl.pallas_call(
        kern,
        grid=(n // tm,),
        in_specs=[tok(d), _layer_spec(norm_g, layer), _layer_spec(w_bf, layer // 2),
                  _layer_spec(conv_w, layer // 2)] + casts.in_specs,
        out_specs=[tok(A_QK), tok(A_QK), tok(A_QK),
                   pl.BlockSpec((None, A_QK, tm), lambda i: (i // per_blk, 0, i % per_blk)),
                   tok(bw)] + casts.out_specs,
        out_shape=[jax.ShapeDtypeStruct((n, A_QK), BF16)] * 3
        + [jax.ShapeDtypeStruct((n // t, A_QK, t), BF16), jax.ShapeDtypeStruct((n, bw), BF16)]
        + casts.out_shapes,
        scratch_shapes=[pltpu.VMEM((SUBLANES, bw), F32)],
        compiler_params=pltpu.CompilerParams(dimension_semantics=("arbitrary",),
                                             vmem_limit_bytes=VMEM_LIMIT_BYTES),
        name="even_in_proj",
    )(x2d, norm_g, w_bf, conv_w, *casts.arrays)


_DIAG, _PREV = 0, 1


def _patched(s, tile_ref, patches):
    sb = tile_ref.shape[-1]
    for r, c, idx, scale in patches:
        band = s[r:r + sb, :]
        cols = [band[:, :c]] if c else []
        tile = tile_ref[idx] if scale is None else tile_ref[idx] * scale
        cols.append(band[:, c:c + sb] + tile)
        if c + sb < s.shape[1]:
            cols.append(band[:, c + sb:])
        rows = [s[:r]] if r else []
        rows.append(jnp.concatenate(cols, axis=1) if len(cols) > 1 else cols[0])
        if r + sb < s.shape[0]:
            rows.append(s[r + sb:])
        s = jnp.concatenate(rows, axis=0) if len(rows) > 1 else rows[0]
    return s


def _fold8(x, op):
    return op(x.reshape(x.shape[0] // SUBLANES, SUBLANES, x.shape[1]), axis=0)


def _lpad(x, width, value):
    if not width:
        return x
    return jnp.concatenate([jnp.full((x.shape[0], width), value, x.dtype), x], axis=1)


def _attn_kernel(lam_ref, q1_ref, q2_ref, q1n_ref, q2n_ref, k_ref, vt_ref, dt_ref, cfar_ref, g_ref,
                 *refs, lambda_init, n_cast):
    cast_in, o_ref, cast_out = refs[:n_cast], refs[n_cast], refs[n_cast + 1:2 * n_cast + 1]
    sa_ref, sb_ref, sd_ref, ca_ref, cb_ref, cd_ref, m_ref, l_ref, acc_ref = refs[2 * n_cast + 1:]
    _cast_slabs(cast_in, cast_out)
    i = pl.program_id(2)
    nblk = k_ref.shape[0]
    t = q1_ref.shape[0]
    sb = dt_ref.shape[-1]
    ns = t // sb
    q_refs = (q1_ref, q2_ref)
    qn_refs = (q1n_ref, q2n_ref)
    cfar = cfar_ref[...]
    rows = [slice(a * sb, (a + 1) * sb) for a in range(ns)]

    def fill_piece(blk, a, mp, s_ref, cmax):
        s = lax.dot_general(k_ref[blk, rows[a], :], q_refs[mp][...], _NT, preferred_element_type=F32)
        if a == ns - 1:
            is_prev = jnp.where(blk == i - 1, 1.0, 0.0).astype(F32)
            s = _patched(s, dt_ref, [(0, 0, _PREV, is_prev)])
        s_ref[mp, rows[a], :] = s
        pm = _fold8(s, jnp.max)
        return pm if cmax is None else jnp.maximum(cmax, pm)

    def past_producer(blk, s_ref, c_ref):
        cmax = [None, None]

        def piece(a, mp):
            cmax[mp] = fill_piece(blk, a, mp, s_ref, cmax[mp])

        def finish():
            for mp in range(2):
                c_ref[mp] = jnp.max(cmax[mp], axis=0, keepdims=True)
        return piece, finish

    def diag_strip(qr, blk, a, mp):
        s = lax.dot_general(k_ref[blk, rows[a], :], qr[mp][a * sb:, :], _NT,
                            preferred_element_type=F32)
        patches = [(0, 0, _DIAG, None)] + ([(0, sb, _PREV, None)] if a + 1 < ns else [])
        return _patched(s, dt_ref, patches)

    def strip_max(s, a, smax):
        pm = _lpad(jnp.max(s, axis=0, keepdims=True), a * sb, MASK_VALUE)
        return pm if smax is None else jnp.maximum(smax, pm)

    def next_diag_producer():
        blk = jnp.minimum(i + 1, nblk - 1)
        smax = [None, None]

        def piece(a, mp):
            s = diag_strip(qn_refs, blk, a, mp)
            sd_ref[mp, rows[a], a * sb:] = s
            smax[mp] = strip_max(s, a, smax[mp])

        def finish():
            for mp in range(2):
                cd_ref[mp] = smax[mp]
        return piece, finish

    def past_step(blk, cur_s, cur_c, producer):
        piece, finish = producer
        shift, alpha = [], []
        for mp in range(2):
            m_old = m_ref[mp]
            m_new = jnp.maximum(m_old, cur_c[mp] + cfar)
            m_ref[mp] = m_new
            shift.append(m_new - cfar)
            alpha.append(jnp.exp2(m_old - m_new))
        lsum, pv = [None, None], [None, None]
        for a in range(ns):
            for mp in range(2):
                e = jnp.exp2(cur_s[mp, rows[a], :] - shift[mp])
                ps = _fold8(e, jnp.sum)
                lsum[mp] = ps if lsum[mp] is None else lsum[mp] + ps
                piece(a, mp)
                d = jnp.dot(vt_ref[blk, :, rows[a]], e.astype(BF16), preferred_element_type=F32)
                pv[mp] = d if pv[mp] is None else pv[mp] + d
        finish()
        for mp in range(2):
            acc_ref[mp] = alpha[mp] * acc_ref[mp] + pv[mp]
            l_ref[mp] = alpha[mp] * l_ref[mp] + lsum[mp]

    def diag_step(strip, smax, producer):
        piece, finish = producer
        lsum = [None, None]
        for a in range(ns):
            q_lo = a * sb
            for mp in range(2):
                e = jnp.exp2(strip(mp, a) - smax[mp][:, q_lo:])
                ps = _lpad(_fold8(e, jnp.sum), q_lo, 0.0)
                lsum[mp] = ps if lsum[mp] is None else lsum[mp] + ps
                piece(a, mp)
                d = jnp.dot(vt_ref[i, :, rows[a]], e.astype(BF16), preferred_element_type=F32)
                if a == 0:
                    acc_ref[mp] = d
                else:
                    acc_ref[mp, :, q_lo:] += d
        finish()
        for mp in range(2):
            m_ref[mp] = smax[mp] + cfar
            l_ref[mp] = lsum[mp]

    @pl.when(i == 0)
    def _():
        strips, smax = [[None] * ns, [None] * ns], [None, None]
        for a in range(ns):
            for mp in range(2):
                strips[mp][a] = diag_strip(q_refs, i, a, mp)
                smax[mp] = strip_max(strips[mp][a], a, smax[mp])
        diag_step(lambda mp, a: strips[mp][a], smax, next_diag_producer())

    @pl.when(i > 0)
    def _():
        diag_step(lambda mp, a: sd_ref[mp, rows[a], a * sb:], [cd_ref[0], cd_ref[1]],
                  past_producer(0, sa_ref, ca_ref))

    def pair_body(p, carry):
        past_step(2 * p, sa_ref, ca_ref, past_producer(2 * p + 1, sb_ref, cb_ref))
        past_step(2 * p + 1, sb_ref, cb_ref, past_producer(2 * p + 2, sa_ref, ca_ref))
        return carry

    lax.fori_loop(0, jnp.maximum(i - 1, 0) // 2, pair_body, 0)

    @pl.when(jnp.logical_and(i >= 2, i % 2 == 0))
    def _():
        past_step(i - 2, sa_ref, ca_ref, past_producer(i - 1, sb_ref, cb_ref))
        past_step(i - 1, sb_ref, cb_ref, next_diag_producer())

    @pl.when(i % 2 == 1)
    def _():
        past_step(i - 1, sa_ref, ca_ref, next_diag_producer())

    lam = lam_ref[...]
    lam_full = (jnp.exp(jnp.sum(lam[0:1] * lam[1:2], axis=1, keepdims=True))
                - jnp.exp(jnp.sum(lam[2:3] * lam[3:4], axis=1, keepdims=True)) + lambda_init)
    inv_l = [1.0 / jnp.sum(l_ref[mp], axis=0, keepdims=True) for mp in range(2)]
    o = acc_ref[0] * inv_l[0] - lam_full * (acc_ref[1] * inv_l[1])
    y = o * lax.rsqrt(jnp.mean(o * o, axis=0, keepdims=True) + SUBLN_EPS)
    o_ref[...] = (y.T * (g_ref[...] * (1.0 - lambda_init))).astype(BF16)


def _diff_attention(lam, q1, q2, k4, vt4, dtiles, cfar, subln_g, lambda_init, layer, to_cast):
    bsz, nblk, t, _ = k4.shape
    seq = nblk * t
    hw = HEAD_WIDTH
    sb = dtiles.shape[-1]
    casts = _CastPlan(to_cast, bsz * DIFF_HEADS * nblk,
                      lambda b, h, i: (b * DIFF_HEADS + h) * nblk + i)
    kern = functools.partial(_attn_kernel, lambda_init=lambda_init, n_cast=len(casts))
    qspec = pl.BlockSpec((None, t, hw), lambda b, h, i: (b, i, h))
    qnext = pl.BlockSpec((None, t, hw), lambda b, h, i: (b, jnp.minimum(i + 1, nblk - 1), h))
    return pl.pallas_call(
        kern,
        grid=(bsz, DIFF_HEADS, nblk),
        in_specs=[
            _layer_spec(lam, layer // 2),
            qspec, qspec, qnext, qnext,
            pl.BlockSpec((None, nblk, t, hw), lambda b, h, i: (b, 0, 0, h)),
            pl.BlockSpec((None, nblk, hw, t), lambda b, h, i: (b, 0, h, 0)),
            pl.BlockSpec((None, 2, sb, sb), lambda b, h, i: (h, 0, 0, 0)),
            pl.BlockSpec((None, 1, t), lambda b, h, i: (h, 0, 0)),
            _layer_spec(subln_g, layer // 2),
        ] + casts.in_specs,
        out_specs=[qspec] + casts.out_specs,
        out_shape=[jax.ShapeDtypeStruct((bsz, seq, A_QK), BF16)] + casts.out_shapes,
        scratch_shapes=[pltpu.VMEM((2, t, t), F32)] * 3
        + [pltpu.VMEM((2, 1, t), F32)] * 3
        + [pltpu.VMEM((2, 1, t), F32),
           pltpu.VMEM((2, SUBLANES, t), F32),
           pltpu.VMEM((2, hw, t), F32)],
        compiler_params=pltpu.CompilerParams(
            dimension_semantics=("arbitrary", "arbitrary", "arbitrary"),
            vmem_limit_bytes=VMEM_LIMIT_BYTES),
        name="diff_attention",
    )(lam, q1, q2, q1, q2, k4, vt4, dtiles, cfar, subln_g, *casts.arrays)


def _row_parts(tm):
    step = tm // ROW_PARTS
    return [slice(p * step, (p + 1) * step) for p in range(ROW_PARTS)]


def _residual_ffn(x_ref, mixes, parts, ng_ref, wg_ref, wu_ref, wd_ref, o_ref):
    x1s, h2s, acts = [], [], []
    for p, mix in zip(parts, mixes):
        x1 = x_ref[p, :] + _rms(mix, ng_ref[1:2, :])
        x1s.append(x1)
        h2s.append(_rms(x1, ng_ref[2:3, :]).astype(BF16))
    for h2 in h2s:
        gate = jnp.dot(h2, wg_ref[...], preferred_element_type=F32)
        up = jnp.dot(h2, wu_ref[...], preferred_element_type=F32)
        acts.append((gate * jax.nn.sigmoid(gate) * up).astype(BF16))
    for p, x1, act in zip(parts, x1s, acts):
        f = jnp.dot(act, wd_ref[...], preferred_element_type=F32)
        o_ref[p, :] = x1 + _rms(f, ng_ref[3:4, :])


def _even_out_kernel(x_ref, a_ref, bo_ref, wo_ref, ng_ref, wg_ref, wu_ref, wd_ref, o_ref):
    aw = a_ref.shape[1]
    parts = _row_parts(x_ref.shape[0])
    mixes = [jnp.dot(a_ref[p, :], wo_ref[0:aw, :], preferred_element_type=F32)
             + jnp.dot(bo_ref[p, :], wo_ref[aw:, :], preferred_element_type=F32) for p in parts]
    _residual_ffn(x_ref, mixes, parts, ng_ref, wg_ref, wu_ref, wd_ref, o_ref)


def _even_out(x2d, a2d, bo2d, wo_bf, ng, wg_bf, wu_bf, wd_bf, layer):
    n, d = x2d.shape
    tm = TOKEN_TILE
    tok = lambda w: pl.BlockSpec((tm, w), lambda i: (i, 0))
    return pl.pallas_call(
        _even_out_kernel,
        grid=(n // tm,),
        in_specs=[tok(d), tok(a2d.shape[1]), tok(bo2d.shape[1]), _const_spec(wo_bf.shape),
                  _layer_spec(ng, layer), _const_spec(wg_bf.shape), _const_spec(wu_bf.shape),
                  _const_spec(wd_bf.shape)],
        out_specs=tok(d),
        out_shape=jax.ShapeDtypeStruct((n, d), F32),
        compiler_params=pltpu.CompilerParams(dimension_semantics=("arbitrary",),
                                             vmem_limit_bytes=VMEM_LIMIT_BYTES),
        name="even_out_ffn",
    )(x2d, a2d, bo2d, wo_bf, ng, wg_bf, wu_bf, wd_bf)


def _odd_kernel(x_ref, wi_ref, lng_ref, lnb_ref, ws_ref, sb_ref, wo_ref, ng_ref,
                wg_ref, wu_ref, wd_ref, o_ref):
    sw = wo_ref.shape[0]
    gd = sw // SGU_GROUPS
    parts = _row_parts(x_ref.shape[0])
    nch = (parts[0].stop - parts[0].start) // CHUNK

    def gelu(z):
        return 0.5 * z * (1.0 + lax.erf(z * math.sqrt(0.5)))

    def gate_inputs(h):
        u = gelu(jnp.dot(h, wi_ref[:, 0:sw], preferred_element_type=F32))
        v = gelu(jnp.dot(h, wi_ref[:, sw:], preferred_element_type=F32))
        mu = jnp.mean(v, axis=-1, keepdims=True)
        vc = v - mu
        v = (vc * lax.rsqrt(jnp.mean(vc * vc, axis=-1, keepdims=True) + LN_EPS) * lng_ref[...]
             + lnb_ref[...]).astype(BF16)
        return u, v

    def spatial_gate(u, v):
        tiles = [[None] * SGU_GROUPS for _ in range(nch)]
        for g in range(SGU_GROUPS):
            rhs = jnp.concatenate([v[n * CHUNK:(n + 1) * CHUNK, g * gd:(g + 1) * gd] for n in range(nch)],
                                  axis=1)
            mixed = jnp.dot(ws_ref[g], rhs, preferred_element_type=F32) + sb_ref[g]
            for n in range(nch):
                tiles[n][g] = (u[n * CHUNK:(n + 1) * CHUNK, g * gd:(g + 1) * gd]
                               * mixed[:, n * gd:(n + 1) * gd])
        gated = jnp.concatenate([jnp.concatenate(r, axis=1) for r in tiles], axis=0).astype(BF16)
        return jnp.dot(gated, wo_ref[...], preferred_element_type=F32)

    hs = [_rms(x_ref[p, :], ng_ref[0:1, :]).astype(BF16) for p in parts]
    uvs = [gate_inputs(h) for h in hs]
    mixes = [spatial_gate(u, v) for u, v in uvs]
    _residual_ffn(x_ref, mixes, parts, ng_ref, wg_ref, wu_ref, wd_ref, o_ref)


def _odd_layer(x2d, wi_bf, ln_g, ln_b, ws_bf, sb, wo_bf, ng, wg_bf, wu_bf, wd_bf, layer):
    n, d = x2d.shape
    tm = TOKEN_TILE
    tok = pl.BlockSpec((tm, d), lambda i: (i, 0))
    consts = (wi_bf, ln_g, ln_b, ws_bf, sb, wo_bf, ng, wg_bf, wu_bf, wd_bf)
    whole = lambda a: _const_spec(a.shape)
    odd = lambda a: _layer_spec(a, layer // 2)
    return pl.pallas_call(
        _odd_kernel,
        grid=(n // tm,),
        in_specs=[tok, whole(wi_bf), odd(ln_g), odd(ln_b), odd(ws_bf), odd(sb), whole(wo_bf),
                  _layer_spec(ng, layer), whole(wg_bf), whole(wu_bf), whole(wd_bf)],
        out_specs=tok,
        out_shape=jax.ShapeDtypeStruct((n, d), F32),
        compiler_params=pltpu.CompilerParams(dimension_semantics=("arbitrary",),
                                             vmem_limit_bytes=VMEM_LIMIT_BYTES),
        name="odd_sgu_ffn",
    )(x2d, *consts)


def _t5_bucket_of_distance(n):
    max_exact = REL_BUCKETS // 2
    nf = jnp.maximum(n, 1).astype(F32)
    large = max_exact + (jnp.log(nf / max_exact) / math.log(REL_MAX_DIST / max_exact)
                         * (REL_BUCKETS - max_exact)).astype(jnp.int32)
    large = jnp.minimum(large, REL_BUCKETS - 1)
    return jnp.where(n < max_exact, n, large)


def _bias_tiles(rel_bias, sb, t):
    assert sb >= REL_MAX_DIST
    heads = rel_bias.shape[1]
    table = rel_bias.astype(F32).T
    span = 2 * sb
    hit = _t5_bucket_of_distance(jnp.arange(span, dtype=jnp.int32))[:, None] == jnp.arange(REL_BUCKETS)
    far = table[:, REL_BUCKETS - 1:]
    by_dist = (jnp.sum(jnp.where(hit[None], table[:, None, :], 0.0), axis=-1) - far) * LOG2_E

    def toeplitz(v):
        rows = jnp.broadcast_to(v[:, None, :], (heads, sb, span))
        skew = jnp.pad(rows, ((0, 0), (0, 0), (0, 1))).reshape(heads, sb * (span + 1))
        return skew[:, :sb * span].reshape(heads, sb, span)[:, :, sb:]

    prev = toeplitz(by_dist)
    diag = toeplitz(jnp.concatenate(
        [jnp.full((heads, sb), MASK_VALUE, F32), by_dist[:, :sb]], axis=1))
    tiles = jnp.stack([diag, prev], axis=1)
    cfar = jnp.broadcast_to((far * LOG2_E)[:, :, None], (heads, 1, t))
    return tiles, cfar


def kernel(x, rel_bias, w_in_even, diff_lambda, diff_subln_g, conv_w, w_out_even, w_in_odd,
           sgu_ln_g, sgu_ln_b, sgu_w, sgu_b, w_out_odd, norm_g, w_gate, w_up, w_down):
    bsz, seq, d = x.shape
    depth = norm_g.shape[0]
    t = ATTN_BLOCK
    assert seq % t == 0 and t % TOKEN_TILE == 0 and t % BIAS_TILE == 0
    assert TOKEN_TILE % (ROW_PARTS * CHUNK) == 0
    x2d = x.reshape(bsz * seq, d)
    dtiles, cfar = _bias_tiles(rel_bias, BIAS_TILE, t)
    tril = jnp.tril(jnp.ones((CHUNK, CHUNK), dtype=bool))
    w_in_even_bf = w_in_even.astype(BF16)
    ws = jnp.where(tril, sgu_w, 0.0).astype(BF16)
    ffn = lambda i: [(w_gate, i), (w_up, i), (w_down, i)]
    odd_bf = None
    for i in range(depth):
        if i % 2 == 0:
            lambda_init = 0.8 - 0.6 * math.exp(-0.3 * i)
            q1, q2, k, vt, bo, wo_bf, wg, wu, wd = _even_in(
                x2d, norm_g, w_in_even_bf, conv_w, seq, i, [(w_out_even, i // 2)] + ffn(i))
            nxt = [(w_in_odd, i // 2), (w_out_odd, i // 2)] + ffn(i + 1) if i + 1 < depth else []
            a, *odd_bf = _diff_attention(
                diff_lambda, q1.reshape(bsz, seq, A_QK), q2.reshape(bsz, seq, A_QK),
                k.reshape(bsz, seq // t, t, A_QK), vt.reshape(bsz, seq // t, A_QK, t),
                dtiles, cfar, diff_subln_g[:, None, :], lambda_init, i, nxt)
            x2d = _even_out(x2d, a.reshape(bsz * seq, A_QK), bo, wo_bf, norm_g, wg, wu, wd, i)
        else:
            wi_bf, wo_bf, wg, wu, wd = odd_bf
            x2d = _odd_layer(x2d, wi_bf, sgu_ln_g[:, None, :], sgu_ln_b[:, None, :], ws,
                             sgu_b[..., None], wo_bf, norm_g, wg, wu, wd, i)
    return x2d.reshape(bsz, seq, d)
```

```python
import functools
import math

import jax
import jax.numpy as jnp
from jax import lax
from jax.experimental import pallas as pl
from jax.experimental.pallas import tpu as pltpu

F32 = jnp.float32
BF16 = jnp.bfloat16

DIFF_HEADS = 4
DIFF_QK_DIM = 64
HEAD_WIDTH = 2 * DIFF_QK_DIM
A_QK = DIFF_HEADS * HEAD_WIDTH
CONV_WIDTH = 3
SGU_GROUPS = 8
CHUNK = 128
REL_BUCKETS = 32
REL_MAX_DIST = 128
RMS_EPS = 1e-6
SUBLN_EPS = 1e-5
LN_EPS = 1e-5
MASK_VALUE = -1e30
LOG2_E = math.log2(math.e)

SUBLANES = 8
BF16_SUBLANES = 16
TOKEN_TILE = 512
ROW_PARTS = 2
ATTN_BLOCK = 1024
BIAS_TILE = 256
VMEM_LIMIT_BYTES = 56 * 1024 * 1024

_NT = (((1,), (1,)), ((), ()))


def _rms(x, g, eps=RMS_EPS):
    return x * lax.rsqrt(jnp.mean(x * x, axis=-1, keepdims=True) + eps) * g


def _const_spec(shape):
    return pl.BlockSpec(shape, lambda *_: (0,) * len(shape), pipeline_mode=pl.Buffered(1))


def _layer_spec(stacked, layer):
    rest = stacked.shape[1:]
    return pl.BlockSpec((None,) + rest, lambda *_: (layer,) + (0,) * len(rest),
                        pipeline_mode=pl.Buffered(1))


class _CastPlan:
    def __init__(self, weights, nsteps, step_of):
        self.arrays = [w for w, _ in weights]
        self.in_specs, self.out_specs, self.out_shapes = [], [], []
        for w, layer in weights:
            _, r, c = w.shape
            slab = next(s for s in range(BF16_SUBLANES, r + 1, BF16_SUBLANES)
                        if r % s == 0 and nsteps % (r // s) == 0)
            per_slab = nsteps // (r // slab)
            self.in_specs.append(pl.BlockSpec(
                (None, slab, c), lambda *g, layer=layer, per=per_slab: (layer, step_of(*g) // per, 0)))
            self.out_specs.append(pl.BlockSpec(
                (slab, c), lambda *g, per=per_slab: (step_of(*g) // per, 0)))
            self.out_shapes.append(jax.ShapeDtypeStruct((r, c), BF16))

    def __len__(self):
        return len(self.arrays)


def _cast_slabs(in_refs, out_refs):
    for src, dst in zip(in_refs, out_refs):
        dst[...] = src[...].astype(BF16)


def _even_in_kernel(x_ref, ng_ref, w_ref, cw_ref, *refs, tiles_per_seq, n_cast):
    cast_in, refs = refs[:n_cast], refs[n_cast:]
    q1_ref, q2_ref, k_ref, vt_ref, bo_ref = refs[:5]
    cast_out, (carry_ref,) = refs[5:5 + n_cast], refs[5 + n_cast:]
    _cast_slabs(cast_in, cast_out)
    tm = x_ref.shape[0]
    bw = bo_ref.shape[1]

    @pl.when(pl.program_id(0) % tiles_per_seq == 0)
    def _():
        carry_ref[...] = jnp.zeros_like(carry_ref)

    def proj(h, lo, width):
        return jnp.dot(h, w_ref[:, lo:lo + width], preferred_element_type=F32)

    parts = _row_parts(tm)
    hs = [_rms(x_ref[p, :], ng_ref[0:1, :]).astype(BF16) for p in parts]
    c0 = 3 * A_QK
    b_parts, z_parts = [], []
    for p, h in zip(parts, hs):
        q = proj(h, 0, A_QK) * (DIFF_QK_DIM ** -0.5 * LOG2_E)
        first_map = (lax.broadcasted_iota(jnp.int32, q.shape, 1) % HEAD_WIDTH) < DIFF_QK_DIM
        q1_ref[p, :] = jnp.where(first_map, q, 0.0).astype(BF16)
        q2_ref[p, :] = jnp.where(first_map, 0.0, q).astype(BF16)
        k_ref[p, :] = proj(h, A_QK, A_QK).astype(BF16)
        vt_ref[:, p] = proj(h, 2 * A_QK, A_QK).T.astype(BF16)
        b_parts.append(proj(h, c0, bw))
        z_parts.append(proj(h, c0 + bw, bw) * proj(h, c0 + 2 * bw, bw))
    b_gate = jnp.concatenate(b_parts, axis=0)
    z = jnp.concatenate(z_parts, axis=0)
    row = lax.broadcasted_iota(jnp.int32, z.shape, 0)
    prev = carry_ref[...]
    zm1 = jnp.where(row == 0, prev[SUBLANES - 1:SUBLANES], pltpu.roll(z, 1, 0))
    zm2 = jnp.where(row == 0, prev[SUBLANES - 2:SUBLANES - 1],
                    jnp.where(row == 1, prev[SUBLANES - 1:SUBLANES], pltpu.roll(z, 2, 0)))
    cw = cw_ref[...]
    y = cw[0:1] * zm2 + cw[1:2] * zm1 + cw[2:3] * z
    bo_ref[...] = (b_gate * y).astype(BF16)
    carry_ref[...] = z[tm - SUBLANES:, :]


def _even_in(x2d, norm_g, w_bf, conv_w, seq, layer, to_cast):
    n, d = x2d.shape
    tm = TOKEN_TILE
    t = ATTN_BLOCK
    per_blk = t // tm
    bw = conv_w.shape[-1]
    casts = _CastPlan(to_cast, n // tm, lambda i: i)
    kern = functools.partial(_even_in_kernel, tiles_per_seq=seq // tm, n_cast=len(casts))
    tok = lambda w: pl.BlockSpec((tm, w), lambda i: (i, 0))
    return pl.pallas_call(
        kern,
        grid=(n // tm,),
        in_specs=[tok(d), _layer_spec(norm_g, layer), _layer_spec(w_bf, layer // 2),
                  _layer_spec(conv_w, layer // 2)] + casts.in_specs,
        out_specs=[tok(A_QK), tok(A_QK), tok(A_QK),
                   pl.BlockSpec((None, A_QK, tm), lambda i: (i // per_blk, 0, i % per_blk)),
                   tok(bw)] + casts.out_specs,
        out_shape=[jax.ShapeDtypeStruct((n, A_QK), BF16)] * 3
        + [jax.ShapeDtypeStruct((n // t, A_QK, t), BF16), jax.ShapeDtypeStruct((n, bw), BF16)]
        + casts.out_shapes,
        scratch_shapes=[pltpu.VMEM((SUBLANES, bw), F32)],
        compiler_params=pltpu.CompilerParams(dimension_semantics=("arbitrary",),
                                             vmem_limit_bytes=VMEM_LIMIT_BYTES),
        name="even_in_proj",
    )(x2d, norm_g, w_bf, conv_w, *casts.arrays)


_DIAG, _PREV = 0, 1


def _patched(s, tile_ref, patches):
    sb = tile_ref.shape[-1]
    for r, c, idx, scale in patches:
        band = s[r:r + sb, :]
        cols = [band[:, :c]] if c else []
        tile = tile_ref[idx] if scale is None else tile_ref[idx] * scale
        cols.append(band[:, c:c + sb] + tile)
        if c + sb < s.shape[1]:
            cols.append(band[:, c + sb:])
        rows = [s[:r]] if r else []
        rows.append(jnp.concatenate(cols, axis=1) if len(cols) > 1 else cols[0])
        if r + sb < s.shape[0]:
            rows.append(s[r + sb:])
        s = jnp.concatenate(rows, axis=0) if len(rows) > 1 else rows[0]
    return s


def _fold8(x, op):
    return op(x.reshape(x.shape[0] // SUBLANES, SUBLANES, x.shape[1]), axis=0)


def _lpad(x, width, value):
    if not width:
        return x
    return jnp.concatenate([jnp.full((x.shape[0], width), value, x.dtype), x], axis=1)


def _attn_kernel(lam_ref, q1_ref, q2_ref, q1n_ref, q2n_ref, k_ref, vt_ref, dt_ref, cfar_ref, g_ref,
                 *refs, lambda_init, n_cast):
    cast_in, o_ref, cast_out = refs[:n_cast], refs[n_cast], refs[n_cast + 1:2 * n_cast + 1]
    sa_ref, sb_ref, sd_ref, ca_ref, cb_ref, cd_ref, m_ref, l_ref, acc_ref = refs[2 * n_cast + 1:]
    _cast_slabs(cast_in, cast_out)
    i = pl.program_id(2)
    nblk = k_ref.shape[0]
    t = q1_ref.shape[0]
    sb = dt_ref.shape[-1]
    ns = t // sb
    q_refs = (q1_ref, q2_ref)
    qn_refs = (q1n_ref, q2n_ref)
    cfar = cfar_ref[...]
    rows = [slice(a * sb, (a + 1) * sb) for a in range(ns)]
    hw = vt_ref.shape[1]

    def value_rows(blk, a):
        return jnp.concatenate([vt_ref[blk, :, rows[a]], jnp.ones((BF16_SUBLANES, sb), BF16)], axis=0)

    def fill_piece(blk, a, mp, s_ref, cmax):
        s = lax.dot_general(k_ref[blk, rows[a], :], q_refs[mp][...], _NT, preferred_element_type=F32)
        if a == ns - 1:
            is_prev = jnp.where(blk == i - 1, 1.0, 0.0).astype(F32)
            s = _patched(s, dt_ref, [(0, 0, _PREV, is_prev)])
        s_ref[mp, rows[a], :] = s
        pm = _fold8(s, jnp.max)
        return pm if cmax is None else jnp.maximum(cmax, pm)

    def past_producer(blk, s_ref, c_ref):
        cmax = [None, None]

        def piece(a, mp):
            cmax[mp] = fill_piece(blk, a, mp, s_ref, cmax[mp])

        def finish():
            for mp in range(2):
                c_ref[mp] = jnp.max(cmax[mp], axis=0, keepdims=True)
        return piece, finish

    def diag_strip(qr, blk, a, mp):
        s = lax.dot_general(k_ref[blk, rows[a], :], qr[mp][a * sb:, :], _NT,
                            preferred_element_type=F32)
        patches = [(0, 0, _DIAG, None)] + ([(0, sb, _PREV, None)] if a + 1 < ns else [])
        return _patched(s, dt_ref, patches)

    def strip_max(s, a, smax):
        pm = _lpad(jnp.max(s, axis=0, keepdims=True), a * sb, MASK_VALUE)
        return pm if smax is None else jnp.maximum(smax, pm)

    def next_diag_producer():
        blk = jnp.minimum(i + 1, nblk - 1)
        smax = [None, None]

        def piece(a, mp):
            s = diag_strip(qn_refs, blk, a, mp)
            sd_ref[mp, rows[a], a * sb:] = s
            smax[mp] = strip_max(s, a, smax[mp])

        def finish():
            for mp in range(2):
                cd_ref[mp] = smax[mp]
        return piece, finish

    def past_step(blk, cur_s, cur_c, producer):
        piece, finish = producer
        shift, alpha = [], []
        for mp in range(2):
            m_old = m_ref[mp]
            m_new = jnp.maximum(m_old, cur_c[mp] + cfar)
            m_ref[mp] = m_new
            shift.append(m_new - cfar)
            alpha.append(jnp.exp2(m_old - m_new))
        pv = [None, None]
        for a in range(ns):
            for mp in range(2):
                e = jnp.exp2(cur_s[mp, rows[a], :] - shift[mp])
                piece(a, mp)
                d = jnp.dot(value_rows(blk, a), e.astype(BF16), preferred_element_type=F32)
                pv[mp] = d if pv[mp] is None else pv[mp] + d
        finish()
        for mp in range(2):
            acc_ref[mp] = alpha[mp] * acc_ref[mp] + pv[mp][:hw]
            l_ref[mp] = alpha[mp] * l_ref[mp] + pv[mp][hw:hw + SUBLANES] * (1.0 / SUBLANES)

    def diag_step(strip, smax, producer):
        piece, finish = producer
        lsum = [None, None]
        for a in range(ns):
            q_lo = a * sb
            for mp in range(2):
                e = jnp.exp2(strip(mp, a) - smax[mp][:, q_lo:])
                piece(a, mp)
                d = jnp.dot(value_rows(i, a), e.astype(BF16), preferred_element_type=F32)
                ps = _lpad(d[hw:hw + SUBLANES], q_lo, 0.0)
                lsum[mp] = ps if lsum[mp] is None else lsum[mp] + ps
                if a == 0:
                    acc_ref[mp] = d[:hw]
                else:
                    acc_ref[mp, :, q_lo:] += d[:hw]
        finish()
        for mp in range(2):
            m_ref[mp] = smax[mp] + cfar
            l_ref[mp] = lsum[mp] * (1.0 / SUBLANES)

    @pl.when(i == 0)
    def _():
        strips, smax = [[None] * ns, [None] * ns], [None, None]
        for a in range(ns):
            for mp in range(2):
                strips[mp][a] = diag_strip(q_refs, i, a, mp)
                smax[mp] = strip_max(strips[mp][a], a, smax[mp])
        diag_step(lambda mp, a: strips[mp][a], smax, next_diag_producer())

    @pl.when(i > 0)
    def _():
        diag_step(lambda mp, a: sd_ref[mp, rows[a], a * sb:], [cd_ref[0], cd_ref[1]],
                  past_producer(0, sa_ref, ca_ref))

    def pair_body(p, carry):
        past_step(2 * p, sa_ref, ca_ref, past_producer(2 * p + 1, sb_ref, cb_ref))
        past_step(2 * p + 1, sb_ref, cb_ref, past_producer(2 * p + 2, sa_ref, ca_ref))
        return carry

    lax.fori_loop(0, jnp.maximum(i - 1, 0) // 2, pair_body, 0)

    @pl.when(jnp.logical_and(i >= 2, i % 2 == 0))
    def _():
        past_step(i - 2, sa_ref, ca_ref, past_producer(i - 1, sb_ref, cb_ref))
        past_step(i - 1, sb_ref, cb_ref, next_diag_producer())

    @pl.when(i % 2 == 1)
    def _():
        past_step(i - 1, sa_ref, ca_ref, next_diag_producer())

    lam = lam_ref[...]
    lam_full = (jnp.exp(jnp.sum(lam[0:1] * lam[1:2], axis=1, keepdims=True))
                - jnp.exp(jnp.sum(lam[2:3] * lam[3:4], axis=1, keepdims=True)) + lambda_init)
    inv_l = [1.0 / jnp.sum(l_ref[mp], axis=0, keepdims=True) for mp in range(2)]
    o = acc_ref[0] * inv_l[0] - lam_full * (acc_ref[1] * inv_l[1])
    y = o * lax.rsqrt(jnp.mean(o * o, axis=0, keepdims=True) + SUBLN_EPS)
    o_ref[...] = (y.T * (g_ref[...] * (1.0 - lambda_init))).astype(BF16)


def _diff_attention(lam, q1, q2, k4, vt4, dtiles, cfar, subln_g, lambda_init, layer, to_cast):
    bsz, nblk, t, _ = k4.shape
    seq = nblk * t
    hw = HEAD_WIDTH
    sb = dtiles.shape[-1]
    casts = _CastPlan(to_cast, bsz * DIFF_HEADS * nblk,
                      lambda b, h, i: (b * DIFF_HEADS + h) * nblk + i)
    kern = functools.partial(_attn_kernel, lambda_init=lambda_init, n_cast=len(casts))
    qspec = pl.BlockSpec((None, t, hw), lambda b, h, i: (b, i, h))
    qnext = pl.BlockSpec((None, t, hw), lambda b, h, i: (b, jnp.minimum(i + 1, nblk - 1), h))
    return pl.pallas_call(
        kern,
        grid=(bsz, DIFF_HEADS, nblk),
        in_specs=[
            _layer_spec(lam, layer // 2),
            qspec, qspec, qnext, qnext,
            pl.BlockSpec((None, nblk, t, hw), lambda b, h, i: (b, 0, 0, h)),
            pl.BlockSpec((None, nblk, hw, t), lambda b, h, i: (b, 0, h, 0)),
            pl.BlockSpec((None, 2, sb, sb), lambda b, h, i: (h, 0, 0, 0)),
            pl.BlockSpec((None, 1, t), lambda b, h, i: (h, 0, 0)),
            _layer_spec(subln_g, layer // 2),
        ] + casts.in_specs,
        out_specs=[qspec] + casts.out_specs,
        out_shape=[jax.ShapeDtypeStruct((bsz, seq, A_QK), BF16)] + casts.out_shapes,
        scratch_shapes=[pltpu.VMEM((2, t, t), F32)] * 3
        + [pltpu.VMEM((2, 1, t), F32)] * 3
        + [pltpu.VMEM((2, 1, t), F32),
           pltpu.VMEM((2, SUBLANES, t), F32),
           pltpu.VMEM((2, hw, t), F32)],
        compiler_params=pltpu.CompilerParams(
            dimension_semantics=("arbitrary", "arbitrary", "arbitrary"),
            vmem_limit_bytes=VMEM_LIMIT_BYTES),
        name="diff_attention",
    )(lam, q1, q2, q1, q2, k4, vt4, dtiles, cfar, subln_g, *casts.arrays)


def _row_parts(tm):
    step = tm // ROW_PARTS
    return [slice(p * step, (p + 1) * step) for p in range(ROW_PARTS)]


def _residual_ffn(x_ref, mixes, parts, ng_ref, wg_ref, wu_ref, wd_ref, o_ref):
    x1s, h2s, acts = [], [], []
    for p, mix in zip(parts, mixes):
        x1 = x_ref[p, :] + _rms(mix, ng_ref[1:2, :])
        x1s.append(x1)
        h2s.append(_rms(x1, ng_ref[2:3, :]).astype(BF16))
    for h2 in h2s:
        gate = jnp.dot(h2, wg_ref[...], preferred_element_type=F32)
        up = jnp.dot(h2, wu_ref[...], preferred_element_type=F32)
        acts.append((gate * jax.nn.sigmoid(gate) * up).astype(BF16))
    for p, x1, act in zip(parts, x1s, acts):
        f = jnp.dot(act, wd_ref[...], preferred_element_type=F32)
        o_ref[p, :] = x1 + _rms(f, ng_ref[3:4, :])


def _even_out_kernel(x_ref, a_ref, bo_ref, wo_ref, ng_ref, wg_ref, wu_ref, wd_ref, o_ref):
    aw = a_ref.shape[1]
    parts = _row_parts(x_ref.shape[0])
    mixes = [jnp.dot(a_ref[p, :], wo_ref[0:aw, :], preferred_element_type=F32)
             + jnp.dot(bo_ref[p, :], wo_ref[aw:, :], preferred_element_type=F32) for p in parts]
    _residual_ffn(x_ref, mixes, parts, ng_ref, wg_ref, wu_ref, wd_ref, o_ref)


def _even_out(x2d, a2d, bo2d, wo_bf, ng, wg_bf, wu_bf, wd_bf, layer):
    n, d = x2d.shape
    tm = TOKEN_TILE
    tok = lambda w: pl.BlockSpec((tm, w), lambda i: (i, 0))
    return pl.pallas_call(
        _even_out_kernel,
        grid=(n // tm,),
        in_specs=[tok(d), tok(a2d.shape[1]), tok(bo2d.shape[1]), _const_spec(wo_bf.shape),
                  _layer_spec(ng, layer), _const_spec(wg_bf.shape), _const_spec(wu_bf.shape),
                  _const_spec(wd_bf.shape)],
        out_specs=tok(d),
        out_shape=jax.ShapeDtypeStruct((n, d), F32),
        compiler_params=pltpu.CompilerParams(dimension_semantics=("arbitrary",),
                                             vmem_limit_bytes=VMEM_LIMIT_BYTES),
        name="even_out_ffn",
    )(x2d, a2d, bo2d, wo_bf, ng, wg_bf, wu_bf, wd_bf)


def _odd_kernel(x_ref, wi_ref, lng_ref, lnb_ref, ws_ref, sb_ref, wo_ref, ng_ref,
                wg_ref, wu_ref, wd_ref, o_ref):
    sw = wo_ref.shape[0]
    gd = sw // SGU_GROUPS
    parts = _row_parts(x_ref.shape[0])
    nch = (parts[0].stop - parts[0].start) // CHUNK

    def gelu(z):
        return 0.5 * z * (1.0 + lax.erf(z * math.sqrt(0.5)))

    def gate_inputs(h):
        u = gelu(jnp.dot(h, wi_ref[:, 0:sw], preferred_element_type=F32))
        v = gelu(jnp.dot(h, wi_ref[:, sw:], preferred_element_type=F32))
        mu = jnp.mean(v, axis=-1, keepdims=True)
        vc = v - mu
        v = (vc * lax.rsqrt(jnp.mean(vc * vc, axis=-1, keepdims=True) + LN_EPS) * lng_ref[...]
             + lnb_ref[...]).astype(BF16)
        return u, v

    def spatial_gate(u, v):
        tiles = [[None] * SGU_GROUPS for _ in range(nch)]
        for g in range(SGU_GROUPS):
            rhs = jnp.concatenate([v[n * CHUNK:(n + 1) * CHUNK, g * gd:(g + 1) * gd] for n in range(nch)],
                                  axis=1)
            mixed = jnp.dot(ws_ref[g], rhs, preferred_element_type=F32) + sb_ref[g]
            for n in range(nch):
                tiles[n][g] = (u[n * CHUNK:(n + 1) * CHUNK, g * gd:(g + 1) * gd]
                               * mixed[:, n * gd:(n + 1) * gd])
        gated = jnp.concatenate([jnp.concatenate(r, axis=1) for r in tiles], axis=0).astype(BF16)
        return jnp.dot(gated, wo_ref[...], preferred_element_type=F32)

    hs = [_rms(x_ref[p, :], ng_ref[0:1, :]).astype(BF16) for p in parts]
    uvs = [gate_inputs(h) for h in hs]
    mixes = [spatial_gate(u, v) for u, v in uvs]
    _residual_ffn(x_ref, mixes, parts, ng_ref, wg_ref, wu_ref, wd_ref, o_ref)


def _odd_layer(x2d, wi_bf, ln_g, ln_b, ws_bf, sb, wo_bf, ng, wg_bf, wu_bf, wd_bf, layer):
    n, d = x2d.shape
    tm = TOKEN_TILE
    tok = pl.BlockSpec((tm, d), lambda i: (i, 0))
    consts = (wi_bf, ln_g, ln_b, ws_bf, sb, wo_bf, ng, wg_bf, wu_bf, wd_bf)
    whole = lambda a: _const_spec(a.shape)
    odd = lambda a: _layer_spec(a, layer // 2)
    return pl.pallas_call(
        _odd_kernel,
        grid=(n // tm,),
        in_specs=[tok, whole(wi_bf), odd(ln_g), odd(ln_b), odd(ws_bf), odd(sb), whole(wo_bf),
                  _layer_spec(ng, layer), whole(wg_bf), whole(wu_bf), whole(wd_bf)],
        out_specs=tok,
        out_shape=jax.ShapeDtypeStruct((n, d), F32),
        compiler_params=pltpu.CompilerParams(dimension_semantics=("arbitrary",),
                                             vmem_limit_bytes=VMEM_LIMIT_BYTES),
        name="odd_sgu_ffn",
    )(x2d, *consts)


def _t5_bucket_of_distance(n):
    max_exact = REL_BUCKETS // 2
    nf = jnp.maximum(n, 1).astype(F32)
    large = max_exact + (jnp.log(nf / max_exact) / math.log(REL_MAX_DIST / max_exact)
                         * (REL_BUCKETS - max_exact)).astype(jnp.int32)
    large = jnp.minimum(large, REL_BUCKETS - 1)
    return jnp.where(n < max_exact, n, large)


def _bias_tiles(rel_bias, sb, t):
    assert sb >= REL_MAX_DIST
    heads = rel_bias.shape[1]
    table = rel_bias.astype(F32).T
    span = 2 * sb
    hit = _t5_bucket_of_distance(jnp.arange(span, dtype=jnp.int32))[:, None] == jnp.arange(REL_BUCKETS)
    far = table[:, REL_BUCKETS - 1:]
    by_dist = (jnp.sum(jnp.where(hit[None], table[:, None, :], 0.0), axis=-1) - far) * LOG2_E

    def toeplitz(v):
        rows = jnp.broadcast_to(v[:, None, :], (heads, sb, span))
        skew = jnp.pad(rows, ((0, 0), (0, 0), (0, 1))).reshape(heads, sb * (span + 1))
        return skew[:, :sb * span].reshape(heads, sb, span)[:, :, sb:]

    prev = toeplitz(by_dist)
    diag = toeplitz(jnp.concatenate(
        [jnp.full((heads, sb), MASK_VALUE, F32), by_dist[:, :sb]], axis=1))
    tiles = jnp.stack([diag, prev], axis=1)
    cfar = jnp.broadcast_to((far * LOG2_E)[:, :, None], (heads, 1, t))
    return tiles, cfar


def kernel(x, rel_bias, w_in_even, diff_lambda, diff_subln_g, conv_w, w_out_even, w_in_odd,
           sgu_ln_g, sgu_ln_b, sgu_w, sgu_b, w_out_odd, norm_g, w_gate, w_up, w_down):
    bsz, seq, d = x.shape
    depth = norm_g.shape[0]
    t = ATTN_BLOCK
    assert seq % t == 0 and t % TOKEN_TILE == 0 and t % BIAS_TILE == 0
    assert TOKEN_TILE % (ROW_PARTS * CHUNK) == 0
    x2d = x.reshape(bsz * seq, d)
    dtiles, cfar = _bias_tiles(rel_bias, BIAS_TILE, t)
    tril = jnp.tril(jnp.ones((CHUNK, CHUNK), dtype=bool))
    w_in_even_bf = w_in_even.astype(BF16)
    ws = jnp.where(tril, sgu_w, 0.0).astype(BF16)
    ffn = lambda i: [(w_gate, i), (w_up, i), (w_down, i)]
    odd_bf = None
    for i in range(depth):
        if i % 2 == 0:
            lambda_init = 0.8 - 0.6 * math.exp(-0.3 * i)
            q1, q2, k, vt, bo, wo_bf, wg, wu, wd = _even_in(
                x2d, norm_g, w_in_even_bf, conv_w, seq, i, [(w_out_even, i // 2)] + ffn(i))
            nxt = [(w_in_odd, i // 2), (w_out_odd, i // 2)] + ffn(i + 1) if i + 1 < depth else []
            a, *odd_bf = _diff_attention(
                diff_lambda, q1.reshape(bsz, seq, A_QK), q2.reshape(bsz, seq, A_QK),
                k.reshape(bsz, seq // t, t, A_QK), vt.reshape(bsz, seq // t, A_QK, t),
                dtiles, cfar, diff_subln_g[:, None, :], lambda_init, i, nxt)
            x2d = _even_out(x2d, a.reshape(bsz * seq, A_QK), bo, wo_bf, norm_g, wg, wu, wd, i)
        else:
            wi_bf, wo_bf, wg, wu, wd = odd_bf
            x2d = _odd_layer(x2d, wi_bf, sgu_ln_g[:, None, :], sgu_ln_b[:, None, :], ws,
                             sgu_b[..., None], wo_bf, norm_g, wg, wu, wd, i)
    return x2d.reshape(bsz, seq, d)
```

```python
import functools
import math

import jax
import jax.numpy as jnp
from jax import lax
from jax.experimental import pallas as pl
from jax.experimental.pallas import tpu as pltpu

F32 = jnp.float32
BF16 = jnp.bfloat16

DIFF_HEADS = 4
DIFF_QK_DIM = 64
HEAD_WIDTH = 2 * DIFF_QK_DIM
A_QK = DIFF_HEADS * HEAD_WIDTH
CONV_WIDTH = 3
SGU_GROUPS = 8
CHUNK = 128
REL_BUCKETS = 32
REL_MAX_DIST = 128
RMS_EPS = 1e-6
SUBLN_EPS = 1e-5
LN_EPS = 1e-5
MASK_VALUE = -1e30
LOG2_E = math.log2(math.e)

SUBLANES = 8
BF16_SUBLANES = 16
TOKEN_TILE = 1024
ROW_PARTS = 4
ATTN_BLOCK = 1024
BIAS_TILE = 256
VMEM_LIMIT_BYTES = 56 * 1024 * 1024

_NT = (((1,), (1,)), ((), ()))


def _rms(x, g, eps=RMS_EPS):
    return x * lax.rsqrt(jnp.mean(x * x, axis=-1, keepdims=True) + eps) * g


def _const_spec(shape):
    return pl.BlockSpec(shape, lambda *_: (0,) * len(shape), pipeline_mode=pl.Buffered(1))


def _layer_spec(stacked, layer):
    rest = stacked.shape[1:]
    return pl.BlockSpec((None,) + rest, lambda *_: (layer,) + (0,) * len(rest),
                        pipeline_mode=pl.Buffered(1))


class _CastPlan:
    def __init__(self, weights, nsteps, step_of):
        self.arrays = [w for w, _ in weights]
        self.in_specs, self.out_specs, self.out_shapes = [], [], []
        for w, layer in weights:
            _, r, c = w.shape
            slab = next(s for s in range(BF16_SUBLANES, r + 1, BF16_SUBLANES)
                        if r % s == 0 and nsteps % (r // s) == 0)
            per_slab = nsteps // (r // slab)
            self.in_specs.append(pl.BlockSpec(
                (None, slab, c), lambda *g, layer=layer, per=per_slab: (layer, step_of(*g) // per, 0)))
            self.out_specs.append(pl.BlockSpec(
                (slab, c), lambda *g, per=per_slab: (step_of(*g) // per, 0)))
            self.out_shapes.append(jax.ShapeDtypeStruct((r, c), BF16))

    def __len__(self):
        return len(self.arrays)


def _cast_slabs(in_refs, out_refs):
    for src, dst in zip(in_refs, out_refs):
        dst[...] = src[...].astype(BF16)


def _even_in_kernel(x_ref, ng_ref, w_ref, cw_ref, *refs, tiles_per_seq, n_cast):
    cast_in, refs = refs[:n_cast], refs[n_cast:]
    q1_ref, q2_ref, k_ref, vt_ref, bo_ref = refs[:5]
    cast_out, (carry_ref,) = refs[5:5 + n_cast], refs[5 + n_cast:]
    _cast_slabs(cast_in, cast_out)
    tm = x_ref.shape[0]
    bw = bo_ref.shape[1]

    @pl.when(pl.program_id(0) % tiles_per_seq == 0)
    def _():
        carry_ref[...] = jnp.zeros_like(carry_ref)

    def proj(h, lo, width):
        return jnp.dot(h, w_ref[:, lo:lo + width], preferred_element_type=F32)

    parts = _row_parts(tm)
    hs = [_rms(x_ref[p, :], ng_ref[0:1, :]).astype(BF16) for p in parts]
    c0 = 3 * A_QK
    b_parts, z_parts = [], []
    for p, h in zip(parts, hs):
        q = proj(h, 0, A_QK) * (DIFF_QK_DIM ** -0.5 * LOG2_E)
        first_map = (lax.broadcasted_iota(jnp.int32, q.shape, 1) % HEAD_WIDTH) < DIFF_QK_DIM
        q1_ref[p, :] = jnp.where(first_map, q, 0.0).astype(BF16)
        q2_ref[p, :] = jnp.where(first_map, 0.0, q).astype(BF16)
        k_ref[p, :] = proj(h, A_QK, A_QK).astype(BF16)
        vt_ref[:, p] = proj(h, 2 * A_QK, A_QK).T.astype(BF16)
        b_parts.append(proj(h, c0, bw))
        z_parts.append(proj(h, c0 + bw, bw) * proj(h, c0 + 2 * bw, bw))
    b_gate = jnp.concatenate(b_parts, axis=0)
    z = jnp.concatenate(z_parts, axis=0)
    row = lax.broadcasted_iota(jnp.int32, z.shape, 0)
    prev = carry_ref[...]
    zm1 = jnp.where(row == 0, prev[SUBLANES - 1:SUBLANES], pltpu.roll(z, 1, 0))
    zm2 = jnp.where(row == 0, prev[SUBLANES - 2:SUBLANES - 1],
                    jnp.where(row == 1, prev[SUBLANES - 1:SUBLANES], pltpu.roll(z, 2, 0)))
    cw = cw_ref[...]
    y = cw[0:1] * zm2 + cw[1:2] * zm1 + cw[2:3] * z
    bo_ref[...] = (b_gate * y).astype(BF16)
    carry_ref[...] = z[tm - SUBLANES:, :]


def _even_in(x2d, norm_g, w_bf, conv_w, seq, layer, to_cast):
    n, d = x2d.shape
    tm = TOKEN_TILE
    t = ATTN_BLOCK
    per_blk = t // tm
    bw = conv_w.shape[-1]
    casts = _CastPlan(to_cast, n // tm, lambda i: i)
    kern = functools.partial(_even_in_kernel, tiles_per_seq=seq // tm, n_cast=len(casts))
    tok = lambda w: pl.BlockSpec((tm, w), lambda i: (i, 0))
    return pl.pallas_call(
        kern,
        grid=(n // tm,),
        in_specs=[tok(d), _layer_spec(norm_g, layer), _layer_spec(w_bf, layer // 2),
                  _layer_spec(conv_w, layer // 2)] + casts.in_specs,
        out_specs=[tok(A_QK), tok(A_QK), tok(A_QK),
                   pl.BlockSpec((None, A_QK, tm), lambda i: (i // per_blk, 0, i % per_blk)),
                   tok(bw)] + casts.out_specs,
        out_shape=[jax.ShapeDtypeStruct((n, A_QK), BF16)] * 3
        + [jax.ShapeDtypeStruct((n // t, A_QK, t), BF16), jax.ShapeDtypeStruct((n, bw), BF16)]
        + casts.out_shapes,
        scratch_shapes=[pltpu.VMEM((SUBLANES, bw), F32)],
        compiler_params=pltpu.CompilerParams(dimension_semantics=("arbitrary",),
                                             vmem_limit_bytes=VMEM_LIMIT_BYTES),
        name="even_in_proj",
    )(x2d, norm_g, w_bf, conv_w, *casts.arrays)


_DIAG, _PREV = 0, 1


def _patched(s, tile_ref, patches):
    sb = tile_ref.shape[-1]
    for r, c, idx, scale in patches:
        band = s[r:r + sb, :]
        cols = [band[:, :c]] if c else []
        tile = tile_ref[idx] if scale is None else tile_ref[idx] * scale
        cols.append(band[:, c:c + sb] + tile)
        if c + sb < s.shape[1]:
            cols.append(band[:, c + sb:])
        rows = [s[:r]] if r else []
        rows.append(jnp.concatenate(cols, axis=1) if len(cols) > 1 else cols[0])
        if r + sb < s.shape[0]:
            rows.append(s[r + sb:])
        s = jnp.concatenate(rows, axis=0) if len(rows) > 1 else rows[0]
    return s


def _fold8(x, op):
    return op(x.reshape(x.shape[0] // SUBLANES, SUBLANES, x.shape[1]), axis=0)


def _lpad(x, width, value):
    if not width:
        return x
    return jnp.concatenate([jnp.full((x.shape[0], width), value, x.dtype), x], axis=1)


def _attn_kernel(lam_ref, q1_ref, q2_ref, q1n_ref, q2n_ref, k_ref, vt_ref, dt_ref, cfar_ref, g_ref,
                 *refs, lambda_init, n_cast):
    cast_in, o_ref, cast_out = refs[:n_cast], refs[n_cast], refs[n_cast + 1:2 * n_cast + 1]
    sa_ref, sb_ref, sd_ref, ca_ref, cb_ref, cd_ref, m_ref, l_ref, acc_ref = refs[2 * n_cast + 1:]
    _cast_slabs(cast_in, cast_out)
    i = pl.program_id(2)
    nblk = k_ref.shape[0]
    t = q1_ref.shape[0]
    sb = dt_ref.shape[-1]
    ns = t // sb
    q_refs = (q1_ref, q2_ref)
    qn_refs = (q1n_ref, q2n_ref)
    cfar = cfar_ref[...]
    rows = [slice(a * sb, (a + 1) * sb) for a in range(ns)]
    hw = vt_ref.shape[1]

    def value_rows(blk, a):
        return jnp.concatenate([vt_ref[blk, :, rows[a]], jnp.ones((BF16_SUBLANES, sb), BF16)], axis=0)

    def fill_piece(blk, a, mp, s_ref, cmax):
        s = lax.dot_general(k_ref[blk, rows[a], :], q_refs[mp][...], _NT, preferred_element_type=F32)
        if a == ns - 1:
            is_prev = jnp.where(blk == i - 1, 1.0, 0.0).astype(F32)
            s = _patched(s, dt_ref, [(0, 0, _PREV, is_prev)])
        s_ref[mp, rows[a], :] = s
        pm = _fold8(s, jnp.max)
        return pm if cmax is None else jnp.maximum(cmax, pm)

    def past_producer(blk, s_ref, c_ref):
        cmax = [None, None]

        def piece(a, mp):
            cmax[mp] = fill_piece(blk, a, mp, s_ref, cmax[mp])

        def finish():
            for mp in range(2):
                c_ref[mp] = jnp.max(cmax[mp], axis=0, keepdims=True)
        return piece, finish

    def diag_strip(qr, blk, a, mp):
        s = lax.dot_general(k_ref[blk, rows[a], :], qr[mp][a * sb:, :], _NT,
                            preferred_element_type=F32)
        patches = [(0, 0, _DIAG, None)] + ([(0, sb, _PREV, None)] if a + 1 < ns else [])
        return _patched(s, dt_ref, patches)

    def strip_max(s, a, smax):
        pm = _lpad(jnp.max(s, axis=0, keepdims=True), a * sb, MASK_VALUE)
        return pm if smax is None else jnp.maximum(smax, pm)

    def next_diag_producer():
        blk = jnp.minimum(i + 1, nblk - 1)
        smax = [None, None]

        def piece(a, mp):
            s = diag_strip(qn_refs, blk, a, mp)
            sd_ref[mp, rows[a], a * sb:] = s
            smax[mp] = strip_max(s, a, smax[mp])

        def finish():
            for mp in range(2):
                cd_ref[mp] = smax[mp]
        return piece, finish

    def past_step(blk, cur_s, cur_c, producer):
        piece, finish = producer
        shift, alpha = [], []
        for mp in range(2):
            m_old = m_ref[mp]
            m_new = jnp.maximum(m_old, cur_c[mp] + cfar)
            m_ref[mp] = m_new
            shift.append(m_new - cfar)
            alpha.append(jnp.exp2(m_old - m_new))
        pv = [None, None]
        for a in range(ns):
            for mp in range(2):
                e = jnp.exp2(cur_s[mp, rows[a], :] - shift[mp])
                piece(a, mp)
                d = jnp.dot(value_rows(blk, a), e.astype(BF16), preferred_element_type=F32)
                pv[mp] = d if pv[mp] is None else pv[mp] + d
        finish()
        for mp in range(2):
            acc_ref[mp] = alpha[mp] * acc_ref[mp] + pv[mp][:hw]
            l_ref[mp] = alpha[mp] * l_ref[mp] + pv[mp][hw:hw + SUBLANES] * (1.0 / SUBLANES)

    def diag_step(strip, smax, producer):
        piece, finish = producer
        lsum = [None, None]
        for a in range(ns):
            q_lo = a * sb
            for mp in range(2):
                e = jnp.exp2(strip(mp, a) - smax[mp][:, q_lo:])
                piece(a, mp)
                d = jnp.dot(value_rows(i, a), e.astype(BF16), preferred_element_type=F32)
                ps = _lpad(d[hw:hw + SUBLANES], q_lo, 0.0)
                lsum[mp] = ps if lsum[mp] is None else lsum[mp] + ps
                if a == 0:
                    acc_ref[mp] = d[:hw]
                else:
                    acc_ref[mp, :, q_lo:] += d[:hw]
        finish()
        for mp in range(2):
            m_ref[mp] = smax[mp] + cfar
            l_ref[mp] = lsum[mp] * (1.0 / SUBLANES)

    @pl.when(i == 0)
    def _():
        strips, smax = [[None] * ns, [None] * ns], [None, None]
        for a in range(ns):
            for mp in range(2):
                strips[mp][a] = diag_strip(q_refs, i, a, mp)
                smax[mp] = strip_max(strips[mp][a], a, smax[mp])
        diag_step(lambda mp, a: strips[mp][a], smax, next_diag_producer())

    @pl.when(i > 0)
    def _():
        diag_step(lambda mp, a: sd_ref[mp, rows[a], a * sb:], [cd_ref[0], cd_ref[1]],
                  past_producer(0, sa_ref, ca_ref))

    def pair_body(p, carry):
        past_step(2 * p, sa_ref, ca_ref, past_producer(2 * p + 1, sb_ref, cb_ref))
        past_step(2 * p + 1, sb_ref, cb_ref, past_producer(2 * p + 2, sa_ref, ca_ref))
        return carry

    lax.fori_loop(0, jnp.maximum(i - 1, 0) // 2, pair_body, 0)

    @pl.when(jnp.logical_and(i >= 2, i % 2 == 0))
    def _():
        past_step(i - 2, sa_ref, ca_ref, past_producer(i - 1, sb_ref, cb_ref))
        past_step(i - 1, sb_ref, cb_ref, next_diag_producer())

    @pl.when(i % 2 == 1)
    def _():
        past_step(i - 1, sa_ref, ca_ref, next_diag_producer())

    lam = lam_ref[...]
    lam_full = (jnp.exp(jnp.sum(lam[0:1] * lam[1:2], axis=1, keepdims=True))
                - jnp.exp(jnp.sum(lam[2:3] * lam[3:4], axis=1, keepdims=True)) + lambda_init)
    inv_l = [1.0 / jnp.sum(l_ref[mp], axis=0, keepdims=True) for mp in range(2)]
    o = acc_ref[0] * inv_l[0] - lam_full * (acc_ref[1] * inv_l[1])
    y = o * lax.rsqrt(jnp.mean(o * o, axis=0, keepdims=True) + SUBLN_EPS)
    o_ref[...] = (y.T * (g_ref[...] * (1.0 - lambda_init))).astype(BF16)


def _diff_attention(lam, q1, q2, k4, vt4, dtiles, cfar, subln_g, lambda_init, layer, to_cast):
    bsz, nblk, t, _ = k4.shape
    seq = nblk * t
    hw = HEAD_WIDTH
    sb = dtiles.shape[-1]
    casts = _CastPlan(to_cast, bsz * DIFF_HEADS * nblk,
                      lambda b, h, i: (b * DIFF_HEADS + h) * nblk + i)
    kern = functools.partial(_attn_kernel, lambda_init=lambda_init, n_cast=len(casts))
    qspec = pl.BlockSpec((None, t, hw), lambda b, h, i: (b, i, h))
    qnext = pl.BlockSpec((None, t, hw), lambda b, h, i: (b, jnp.minimum(i + 1, nblk - 1), h))
    return pl.pallas_call(
        kern,
        grid=(bsz, DIFF_HEADS, nblk),
        in_specs=[
            _layer_spec(lam, layer // 2),
            qspec, qspec, qnext, qnext,
            pl.BlockSpec((None, nblk, t, hw), lambda b, h, i: (b, 0, 0, h)),
            pl.BlockSpec((None, nblk, hw, t), lambda b, h, i: (b, 0, h, 0)),
            pl.BlockSpec((None, 2, sb, sb), lambda b, h, i: (h, 0, 0, 0)),
            pl.BlockSpec((None, 1, t), lambda b, h, i: (h, 0, 0)),
            _layer_spec(subln_g, layer // 2),
        ] + casts.in_specs,
        out_specs=[qspec] + casts.out_specs,
        out_shape=[jax.ShapeDtypeStruct((bsz, seq, A_QK), BF16)] + casts.out_shapes,
        scratch_shapes=[pltpu.VMEM((2, t, t), F32)] * 3
        + [pltpu.VMEM((2, 1, t), F32)] * 3
        + [pltpu.VMEM((2, 1, t), F32),
           pltpu.VMEM((2, SUBLANES, t), F32),
           pltpu.VMEM((2, hw, t), F32)],
        compiler_params=pltpu.CompilerParams(
            dimension_semantics=("arbitrary", "arbitrary", "arbitrary"),
            vmem_limit_bytes=VMEM_LIMIT_BYTES),
        name="diff_attention",
    )(lam, q1, q2, q1, q2, k4, vt4, dtiles, cfar, subln_g, *casts.arrays)


def _row_parts(tm):
    step = tm // ROW_PARTS
    return [slice(p * step, (p + 1) * step) for p in range(ROW_PARTS)]


def _residual_ffn(x_ref, mixes, parts, ng_ref, wg_ref, wu_ref, wd_ref, o_ref):
    x1s, h2s, acts = [], [], []
    for p, mix in zip(parts, mixes):
        x1 = x_ref[p, :] + _rms(mix, ng_ref[1:2, :])
        x1s.append(x1)
        h2s.append(_rms(x1, ng_ref[2:3, :]).astype(BF16))
    for h2 in h2s:
        gate = jnp.dot(h2, wg_ref[...], preferred_element_type=F32)
        up = jnp.dot(h2, wu_ref[...], preferred_element_type=F32)
        acts.append((gate * jax.nn.sigmoid(gate) * up).astype(BF16))
    for p, x1, act in zip(parts, x1s, acts):
        f = jnp.dot(act, wd_ref[...], preferred_element_type=F32)
        o_ref[p, :] = x1 + _rms(f, ng_ref[3:4, :])


def _even_out_kernel(x_ref, a_ref, bo_ref, wo_ref, ng_ref, wg_ref, wu_ref, wd_ref, o_ref):
    aw = a_ref.shape[1]
    parts = _row_parts(x_ref.shape[0])
    mixes = [jnp.dot(a_ref[p, :], wo_ref[0:aw, :], preferred_element_type=F32)
             + jnp.dot(bo_ref[p, :], wo_ref[aw:, :], preferred_element_type=F32) for p in parts]
    _residual_ffn(x_ref, mixes, parts, ng_ref, wg_ref, wu_ref, wd_ref, o_ref)


def _even_out(x2d, a2d, bo2d, wo_bf, ng, wg_bf, wu_bf, wd_bf, layer):
    n, d = x2d.shape
    tm = TOKEN_TILE
    tok = lambda w: pl.BlockSpec((tm, w), lambda i: (i, 0))
    return pl.pallas_call(
        _even_out_kernel,
        grid=(n // tm,),
        in_specs=[tok(d), tok(a2d.shape[1]), tok(bo2d.shape[1]), _const_spec(wo_bf.shape),
                  _layer_spec(ng, layer), _const_spec(wg_bf.shape), _const_spec(wu_bf.shape),
                  _const_spec(wd_bf.shape)],
        out_specs=tok(d),
        out_shape=jax.ShapeDtypeStruct((n, d), F32),
        compiler_params=pltpu.CompilerParams(dimension_semantics=("arbitrary",),
                                             vmem_limit_bytes=VMEM_LIMIT_BYTES),
        name="even_out_ffn",
    )(x2d, a2d, bo2d, wo_bf, ng, wg_bf, wu_bf, wd_bf)


def _odd_kernel(x_ref, wi_ref, lng_ref, lnb_ref, ws_ref, sb_ref, wo_ref, ng_ref,
                wg_ref, wu_ref, wd_ref, o_ref):
    sw = wo_ref.shape[0]
    gd = sw // SGU_GROUPS
    parts = _row_parts(x_ref.shape[0])
    nch = (parts[0].stop - parts[0].start) // CHUNK

    def gelu(z):
        return 0.5 * z * (1.0 + lax.erf(z * math.sqrt(0.5)))

    def gate_inputs(h):
        u = gelu(jnp.dot(h, wi_ref[:, 0:sw], preferred_element_type=F32))
        v = gelu(jnp.dot(h, wi_ref[:, sw:], preferred_element_type=F32))
        mu = jnp.mean(v, axis=-1, keepdims=True)
        vc = v - mu
        v = (vc * lax.rsqrt(jnp.mean(vc * vc, axis=-1, keepdims=True) + LN_EPS) * lng_ref[...]
             + lnb_ref[...]).astype(BF16)
        return u, v

    def spatial_gate(u, v):
        tiles = [[None] * SGU_GROUPS for _ in range(nch)]
        for g in range(SGU_GROUPS):
            rhs = jnp.concatenate([v[n * CHUNK:(n + 1) * CHUNK, g * gd:(g + 1) * gd] for n in range(nch)],
                                  axis=1)
            mixed = jnp.dot(ws_ref[g], rhs, preferred_element_type=F32) + sb_ref[g]
            for n in range(nch):
                tiles[n][g] = (u[n * CHUNK:(n + 1) * CHUNK, g * gd:(g + 1) * gd]
                               * mixed[:, n * gd:(n + 1) * gd])
        gated = jnp.concatenate([jnp.concatenate(r, axis=1) for r in tiles], axis=0).astype(BF16)
        return jnp.dot(gated, wo_ref[...], preferred_element_type=F32)

    hs = [_rms(x_ref[p, :], ng_ref[0:1, :]).astype(BF16) for p in parts]
    uvs = [gate_inputs(h) for h in hs]
    mixes = [spatial_gate(u, v) for u, v in uvs]
    _residual_ffn(x_ref, mixes, parts, ng_ref, wg_ref, wu_ref, wd_ref, o_ref)


def _odd_layer(x2d, wi_bf, ln_g, ln_b, ws_bf, sb, wo_bf, ng, wg_bf, wu_bf, wd_bf, layer):
    n, d = x2d.shape
    tm = TOKEN_TILE
    tok = pl.BlockSpec((tm, d), lambda i: (i, 0))
    consts = (wi_bf, ln_g, ln_b, ws_bf, sb, wo_bf, ng, wg_bf, wu_bf, wd_bf)
    whole = lambda a: _const_spec(a.shape)
    odd = lambda a: _layer_spec(a, layer // 2)
    return pl.pallas_call(
        _odd_kernel,
        grid=(n // tm,),
        in_specs=[tok, whole(wi_bf), odd(ln_g), odd(ln_b), odd(ws_bf), odd(sb), whole(wo_bf),
                  _layer_spec(ng, layer), whole(wg_bf), whole(wu_bf), whole(wd_bf)],
        out_specs=tok,
        out_shape=jax.ShapeDtypeStruct((n, d), F32),
        compiler_params=pltpu.CompilerParams(dimension_semantics=("arbitrary",),
                                             vmem_limit_bytes=VMEM_LIMIT_BYTES),
        name="odd_sgu_ffn",
    )(x2d, *consts)


def _t5_bucket_of_distance(n):
    max_exact = REL_BUCKETS // 2
    nf = jnp.maximum(n, 1).astype(F32)
    large = max_exact + (jnp.log(nf / max_exact) / math.log(REL_MAX_DIST / max_exact)
                         * (REL_BUCKETS - max_exact)).astype(jnp.int32)
    large = jnp.minimum(large, REL_BUCKETS - 1)
    return jnp.where(n < max_exact, n, large)


def _bias_tiles(rel_bias, sb, t):
    assert sb >= REL_MAX_DIST
    heads = rel_bias.shape[1]
    table = rel_bias.astype(F32).T
    span = 2 * sb
    hit = _t5_bucket_of_distance(jnp.arange(span, dtype=jnp.int32))[:, None] == jnp.arange(REL_BUCKETS)
    far = table[:, REL_BUCKETS - 1:]
    by_dist = (jnp.sum(jnp.where(hit[None], table[:, None, :], 0.0), axis=-1) - far) * LOG2_E

    def toeplitz(v):
        rows = jnp.broadcast_to(v[:, None, :], (heads, sb, span))
        skew = jnp.pad(rows, ((0, 0), (0, 0), (0, 1))).reshape(heads, sb * (span + 1))
        return skew[:, :sb * span].reshape(heads, sb, span)[:, :, sb:]

    prev = toeplitz(by_dist)
    diag = toeplitz(jnp.concatenate(
        [jnp.full((heads, sb), MASK_VALUE, F32), by_dist[:, :sb]], axis=1))
    tiles = jnp.stack([diag, prev], axis=1)
    cfar = jnp.broadcast_to((far * LOG2_E)[:, :, None], (heads, 1, t))
    return tiles, cfar


def kernel(x, rel_bias, w_in_even, diff_lambda, diff_subln_g, conv_w, w_out_even, w_in_odd,
           sgu_ln_g, sgu_ln_b, sgu_w, sgu_b, w_out_odd, norm_g, w_gate, w_up, w_down):
    bsz, seq, d = x.shape
    depth = norm_g.shape[0]
    t = ATTN_BLOCK
    assert seq % t == 0 and t % TOKEN_TILE == 0 and t % BIAS_TILE == 0
    assert TOKEN_TILE % (ROW_PARTS * CHUNK) == 0
    x2d = x.reshape(bsz * seq, d)
    dtiles, cfar = _bias_tiles(rel_bias, BIAS_TILE, t)
    tril = jnp.tril(jnp.ones((CHUNK, CHUNK), dtype=bool))
    w_in_even_bf = w_in_even.astype(BF16)
    ws = jnp.where(tril, sgu_w, 0.0).astype(BF16)
    ffn = lambda i: [(w_gate, i), (w_up, i), (w_down, i)]
    odd_bf = None
    for i in range(depth):
        if i % 2 == 0:
            lambda_init = 0.8 - 0.6 * math.exp(-0.3 * i)
            q1, q2, k, vt, bo, wo_bf, wg, wu, wd = _even_in(
                x2d, norm_g, w_in_even_bf, conv_w, seq, i, [(w_out_even, i // 2)] + ffn(i))
            nxt = [(w_in_odd, i // 2), (w_out_odd, i // 2)] + ffn(i + 1) if i + 1 < depth else []
            a, *odd_bf = _diff_attention(
                diff_lambda, q1.reshape(bsz, seq, A_QK), q2.reshape(bsz, seq, A_QK),
                k.reshape(bsz, seq // t, t, A_QK), vt.reshape(bsz, seq // t, A_QK, t),
                dtiles, cfar, diff_subln_g[:, None, :], lambda_init, i, nxt)
            x2d = _even_out(x2d, a.reshape(bsz * seq, A_QK), bo, wo_bf, norm_g, wg, wu, wd, i)
        else:
            wi_bf, wo_bf, wg, wu, wd = odd_bf
            x2d = _odd_layer(x2d, wi_bf, sgu_ln_g[:, None, :], sgu_ln_b[:, None, :], ws,
                             sgu_b[..., None], wo_bf, norm_g, wg, wu, wd, i)
    return x2d.reshape(bsz, seq, d)
```

```python
import functools
import math

import jax
import jax.numpy as jnp
from jax import lax
from jax.experimental import pallas as pl
from jax.experimental.pallas import tpu as pltpu

F32 = jnp.float32
BF16 = jnp.bfloat16

DIFF_HEADS = 4
DIFF_QK_DIM = 64
HEAD_WIDTH = 2 * DIFF_QK_DIM
A_QK = DIFF_HEADS * HEAD_WIDTH
CONV_WIDTH = 3
SGU_GROUPS = 8
CHUNK = 128
REL_BUCKETS = 32
REL_MAX_DIST = 128
RMS_EPS = 1e-6
SUBLN_EPS = 1e-5
LN_EPS = 1e-5
MASK_VALUE = -1e30
LOG2_E = math.log2(math.e)

SUBLANES = 8
BF16_SUBLANES = 16
ROW_PART = 256
EVEN_TILE = 4 * ROW_PART
ODD_TILE = 2 * ROW_PART
ATTN_BLOCK = 1024
BIAS_TILE = 256
VMEM_LIMIT_BYTES = 56 * 1024 * 1024

_NT = (((1,), (1,)), ((), ()))


def _rms(x, g, eps=RMS_EPS):
    return x * lax.rsqrt(jnp.mean(x * x, axis=-1, keepdims=True) + eps) * g


def _const_spec(shape):
    return pl.BlockSpec(shape, lambda *_: (0,) * len(shape), pipeline_mode=pl.Buffered(1))


def _layer_spec(stacked, layer):
    rest = stacked.shape[1:]
    return pl.BlockSpec((None,) + rest, lambda *_: (layer,) + (0,) * len(rest),
                        pipeline_mode=pl.Buffered(1))


class _CastPlan:
    def __init__(self, weights, nsteps, step_of):
        self.arrays = [w for w, _ in weights]
        self.in_specs, self.out_specs, self.out_shapes = [], [], []
        for w, layer in weights:
            _, r, c = w.shape
            slab = next(s for s in range(BF16_SUBLANES, r + 1, BF16_SUBLANES)
                        if r % s == 0 and nsteps % (r // s) == 0)
            per_slab = nsteps // (r // slab)
            self.in_specs.append(pl.BlockSpec(
                (None, slab, c), lambda *g, layer=layer, per=per_slab: (layer, step_of(*g) // per, 0)))
            self.out_specs.append(pl.BlockSpec(
                (slab, c), lambda *g, per=per_slab: (step_of(*g) // per, 0)))
            self.out_shapes.append(jax.ShapeDtypeStruct((r, c), BF16))

    def __len__(self):
        return len(self.arrays)


def _cast_slabs(in_refs, out_refs):
    for src, dst in zip(in_refs, out_refs):
        dst[...] = src[...].astype(BF16)


def _even_in_kernel(x_ref, ng_ref, w_ref, cw_ref, *refs, tiles_per_seq, n_cast):
    cast_in, refs = refs[:n_cast], refs[n_cast:]
    q1_ref, q2_ref, k_ref, vt_ref, bo_ref = refs[:5]
    cast_out, (carry_ref,) = refs[5:5 + n_cast], refs[5 + n_cast:]
    _cast_slabs(cast_in, cast_out)
    tm = x_ref.shape[0]
    bw = bo_ref.shape[1]

    @pl.when(pl.program_id(0) % tiles_per_seq == 0)
    def _():
        carry_ref[...] = jnp.zeros_like(carry_ref)

    def proj(h, lo, width):
        return jnp.dot(h, w_ref[:, lo:lo + width], preferred_element_type=F32)

    parts = _row_parts(tm)
    hs = [_rms(x_ref[p, :], ng_ref[0:1, :]).astype(BF16) for p in parts]
    c0 = 3 * A_QK
    b_parts, z_parts = [], []
    for p, h in zip(parts, hs):
        q = proj(h, 0, A_QK) * (DIFF_QK_DIM ** -0.5 * LOG2_E)
        first_map = (lax.broadcasted_iota(jnp.int32, q.shape, 1) % HEAD_WIDTH) < DIFF_QK_DIM
        q1_ref[p, :] = jnp.where(first_map, q, 0.0).astype(BF16)
        q2_ref[p, :] = jnp.where(first_map, 0.0, q).astype(BF16)
        k_ref[p, :] = proj(h, A_QK, A_QK).astype(BF16)
        vt_ref[:, p] = proj(h, 2 * A_QK, A_QK).T.astype(BF16)
        b_parts.append(proj(h, c0, bw))
        z_parts.append(proj(h, c0 + bw, bw) * proj(h, c0 + 2 * bw, bw))
    b_gate = jnp.concatenate(b_parts, axis=0)
    z = jnp.concatenate(z_parts, axis=0)
    row = lax.broadcasted_iota(jnp.int32, z.shape, 0)
    prev = carry_ref[...]
    zm1 = jnp.where(row == 0, prev[SUBLANES - 1:SUBLANES], pltpu.roll(z, 1, 0))
    zm2 = jnp.where(row == 0, prev[SUBLANES - 2:SUBLANES - 1],
                    jnp.where(row == 1, prev[SUBLANES - 1:SUBLANES], pltpu.roll(z, 2, 0)))
    cw = cw_ref[...]
    y = cw[0:1] * zm2 + cw[1:2] * zm1 + cw[2:3] * z
    bo_ref[...] = (b_gate * y).astype(BF16)
    carry_ref[...] = z[tm - SUBLANES:, :]


def _even_in(x2d, norm_g, w_bf, conv_w, seq, layer, to_cast):
    n, d = x2d.shape
    tm = EVEN_TILE
    t = ATTN_BLOCK
    per_blk = t // tm
    bw = conv_w.shape[-1]
    casts = _CastPlan(to_cast, n // tm, lambda i: i)
    kern = functools.partial(_even_in_kernel, tiles_per_seq=seq // tm, n_cast=len(casts))
    tok = lambda w: pl.BlockSpec((tm, w), lambda i: (i, 0))
    return pl.pallas_call(
        kern,
        grid=(n // tm,),
        in_specs=[tok(d), _layer_spec(norm_g, layer), _layer_spec(w_bf, layer // 2),
                  _layer_spec(conv_w, layer // 2)] + casts.in_specs,
        out_specs=[tok(A_QK), tok(A_QK), tok(A_QK),
                   pl.BlockSpec((None, A_QK, tm), lambda i: (i // per_blk, 0, i % per_blk)),
                   tok(bw)] + casts.out_specs,
        out_shape=[jax.ShapeDtypeStruct((n, A_QK), BF16)] * 3
        + [jax.ShapeDtypeStruct((n // t, A_QK, t), BF16), jax.ShapeDtypeStruct((n, bw), BF16)]
        + casts.out_shapes,
        scratch_shapes=[pltpu.VMEM((SUBLANES, bw), F32)],
        compiler_params=pltpu.CompilerParams(dimension_semantics=("arbitrary",),
                                             vmem_limit_bytes=VMEM_LIMIT_BYTES),
        name="even_in_proj",
    )(x2d, norm_g, w_bf, conv_w, *casts.arrays)


_DIAG, _PREV = 0, 1


def _patched(s, tile_ref, patches):
    sb = tile_ref.shape[-1]
    for r, c, idx, scale in patches:
        band = s[r:r + sb, :]
        cols = [band[:, :c]] if c else []
        tile = tile_ref[idx] if scale is None else tile_ref[idx] * scale
        cols.append(band[:, c:c + sb] + tile)
        if c + sb < s.shape[1]:
            cols.append(band[:, c + sb:])
        rows = [s[:r]] if r else []
        rows.append(jnp.concatenate(cols, axis=1) if len(cols) > 1 else cols[0])
        if r + sb < s.shape[0]:
            rows.append(s[r + sb:])
        s = jnp.concatenate(rows, axis=0) if len(rows) > 1 else rows[0]
    return s


def _fold8(x, op):
    return op(x.reshape(x.shape[0] // SUBLANES, SUBLANES, x.shape[1]), axis=0)


def _lpad(x, width, value):
    if not width:
        return x
    return jnp.concatenate([jnp.full((x.shape[0], width), value, x.dtype), x], axis=1)


def _attn_kernel(lam_ref, q1_ref, q2_ref, q1n_ref, q2n_ref, k_ref, vt_ref, dt_ref, cfar_ref, g_ref,
                 *refs, lambda_init, n_cast):
    cast_in, o_ref, cast_out = refs[:n_cast], refs[n_cast], refs[n_cast + 1:2 * n_cast + 1]
    sa_ref, sb_ref, sd_ref, ca_ref, cb_ref, cd_ref, m_ref, l_ref, acc_ref = refs[2 * n_cast + 1:]
    _cast_slabs(cast_in, cast_out)
    i = pl.program_id(2)
    nblk = k_ref.shape[0]
    t = q1_ref.shape[0]
    sb = dt_ref.shape[-1]
    ns = t // sb
    q_refs = (q1_ref, q2_ref)
    qn_refs = (q1n_ref, q2n_ref)
    cfar = cfar_ref[...]
    rows = [slice(a * sb, (a + 1) * sb) for a in range(ns)]
    hw = vt_ref.shape[1]

    def value_rows(blk, a):
        return jnp.concatenate([vt_ref[blk, :, rows[a]], jnp.ones((BF16_SUBLANES, sb), BF16)], axis=0)

    def fill_piece(blk, a, mp, s_ref, cmax):
        s = lax.dot_general(k_ref[blk, rows[a], :], q_refs[mp][...], _NT, preferred_element_type=F32)
        if a == ns - 1:
            is_prev = jnp.where(blk == i - 1, 1.0, 0.0).astype(F32)
            s = _patched(s, dt_ref, [(0, 0, _PREV, is_prev)])
        s_ref[mp, rows[a], :] = s
        pm = _fold8(s, jnp.max)
        return pm if cmax is None else jnp.maximum(cmax, pm)

    def past_producer(blk, s_ref, c_ref):
        cmax = [None, None]

        def piece(a, mp):
            cmax[mp] = fill_piece(blk, a, mp, s_ref, cmax[mp])

        def finish():
            for mp in range(2):
                c_ref[mp] = jnp.max(cmax[mp], axis=0, keepdims=True)
        return piece, finish

    def diag_strip(qr, blk, a, mp):
        s = lax.dot_general(k_ref[blk, rows[a], :], qr[mp][a * sb:, :], _NT,
                            preferred_element_type=F32)
        patches = [(0, 0, _DIAG, None)] + ([(0, sb, _PREV, None)] if a + 1 < ns else [])
        return _patched(s, dt_ref, patches)

    def strip_max(s, a, smax):
        pm = _lpad(jnp.max(s, axis=0, keepdims=True), a * sb, MASK_VALUE)
        return pm if smax is None else jnp.maximum(smax, pm)

    def next_diag_producer():
        blk = jnp.minimum(i + 1, nblk - 1)
        smax = [None, None]

        def piece(a, mp):
            s = diag_strip(qn_refs, blk, a, mp)
            sd_ref[mp, rows[a], a * sb:] = s
            smax[mp] = strip_max(s, a, smax[mp])

        def finish():
            for mp in range(2):
                cd_ref[mp] = smax[mp]
        return piece, finish

    def finalize():
        lam = lam_ref[...]
        lam_full = (jnp.exp(jnp.sum(lam[0:1] * lam[1:2], axis=1, keepdims=True))
                    - jnp.exp(jnp.sum(lam[2:3] * lam[3:4], axis=1, keepdims=True)) + lambda_init)
        inv_l = [1.0 / jnp.sum(l_ref[mp], axis=0, keepdims=True) for mp in range(2)]
        o = acc_ref[0] * inv_l[0] - lam_full * (acc_ref[1] * inv_l[1])
        y = o * lax.rsqrt(jnp.mean(o * o, axis=0, keepdims=True) + SUBLN_EPS)
        o_ref[...] = (y.T * (g_ref[...] * (1.0 - lambda_init))).astype(BF16)

    def wrap_up(producer):
        piece, finish = producer
        for a in range(1, ns):
            for mp in range(2):
                piece(a, mp)
        finish()
        finalize()

    def past_step(blk, cur_s, cur_c, producer, last=False):
        piece, finish = producer
        shift, alpha = [], []
        for mp in range(2):
            m_old = m_ref[mp]
            m_new = jnp.maximum(m_old, cur_c[mp] + cfar)
            m_ref[mp] = m_new
            shift.append(m_new - cfar)
            alpha.append(jnp.exp2(m_old - m_new))
        pv = [None, None]
        for a in range(ns):
            for mp in range(2):
                e = jnp.exp2(cur_s[mp, rows[a], :] - shift[mp])
                if not last or a == 0:
                    piece(a, mp)
                d = jnp.dot(value_rows(blk, a), e.astype(BF16), preferred_element_type=F32)
                pv[mp] = d if pv[mp] is None else pv[mp] + d
        if not last:
            finish()
        for mp in range(2):
            acc_ref[mp] = alpha[mp] * acc_ref[mp] + pv[mp][:hw]
            l_ref[mp] = alpha[mp] * l_ref[mp] + pv[mp][hw:hw + SUBLANES] * (1.0 / SUBLANES)
        if last:
            wrap_up(producer)

    def diag_step(strip, smax, producer, last=False):
        piece, finish = producer
        lsum = [None, None]
        for a in range(ns):
            q_lo = a * sb
            for mp in range(2):
                e = jnp.exp2(strip(mp, a) - smax[mp][:, q_lo:])
                if not last or a == 0:
                    piece(a, mp)
                d = jnp.dot(value_rows(i, a), e.astype(BF16), preferred_element_type=F32)
                ps = _lpad(d[hw:hw + SUBLANES], q_lo, 0.0)
                lsum[mp] = ps if lsum[mp] is None else lsum[mp] + ps
                if a == 0:
                    acc_ref[mp] = d[:hw]
                else:
                    acc_ref[mp, :, q_lo:] += d[:hw]
        if not last:
            finish()
        for mp in range(2):
            m_ref[mp] = smax[mp] + cfar
            l_ref[mp] = lsum[mp] * (1.0 / SUBLANES)
        if last:
            wrap_up(producer)

    @pl.when(i == 0)
    def _():
        strips, smax = [[None] * ns, [None] * ns], [None, None]
        for a in range(ns):
            for mp in range(2):
                strips[mp][a] = diag_strip(q_refs, i, a, mp)
                smax[mp] = strip_max(strips[mp][a], a, smax[mp])
        diag_step(lambda mp, a: strips[mp][a], smax, next_diag_producer(), last=True)

    @pl.when(i > 0)
    def _():
        diag_step(lambda mp, a: sd_ref[mp, rows[a], a * sb:], [cd_ref[0], cd_ref[1]],
                  past_producer(0, sa_ref, ca_ref))

    def pair_body(p, carry):
        past_step(2 * p, sa_ref, ca_ref, past_producer(2 * p + 1, sb_ref, cb_ref))
        past_step(2 * p + 1, sb_ref, cb_ref, past_producer(2 * p + 2, sa_ref, ca_ref))
        return carry

    lax.fori_loop(0, jnp.maximum(i - 1, 0) // 2, pair_body, 0)

    @pl.when(jnp.logical_and(i >= 2, i % 2 == 0))
    def _():
        past_step(i - 2, sa_ref, ca_ref, past_producer(i - 1, sb_ref, cb_ref))
        past_step(i - 1, sb_ref, cb_ref, next_diag_producer(), last=True)

    @pl.when(i % 2 == 1)
    def _():
        past_step(i - 1, sa_ref, ca_ref, next_diag_producer(), last=True)


def _diff_attention(lam, q1, q2, k4, vt4, dtiles, cfar, subln_g, lambda_init, layer, to_cast):
    bsz, nblk, t, _ = k4.shape
    seq = nblk * t
    hw = HEAD_WIDTH
    sb = dtiles.shape[-1]
    casts = _CastPlan(to_cast, bsz * DIFF_HEADS * nblk,
                      lambda b, h, i: (b * DIFF_HEADS + h) * nblk + i)
    kern = functools.partial(_attn_kernel, lambda_init=lambda_init, n_cast=len(casts))
    qspec = pl.BlockSpec((None, t, hw), lambda b, h, i: (b, i, h))
    qnext = pl.BlockSpec((None, t, hw), lambda b, h, i: (b, jnp.minimum(i + 1, nblk - 1), h))
    return pl.pallas_call(
        kern,
        grid=(bsz, DIFF_HEADS, nblk),
        in_specs=[
            _layer_spec(lam, layer // 2),
            qspec, qspec, qnext, qnext,
            pl.BlockSpec((None, nblk, t, hw), lambda b, h, i: (b, 0, 0, h)),
            pl.BlockSpec((None, nblk, hw, t), lambda b, h, i: (b, 0, h, 0)),
            pl.BlockSpec((None, 2, sb, sb), lambda b, h, i: (h, 0, 0, 0)),
            pl.BlockSpec((None, 1, t), lambda b, h, i: (h, 0, 0)),
            _layer_spec(subln_g, layer // 2),
        ] + casts.in_specs,
        out_specs=[qspec] + casts.out_specs,
        out_shape=[jax.ShapeDtypeStruct((bsz, seq, A_QK), BF16)] + casts.out_shapes,
        scratch_shapes=[pltpu.VMEM((2, t, t), F32)] * 3
        + [pltpu.VMEM((2, 1, t), F32)] * 3
        + [pltpu.VMEM((2, 1, t), F32),
           pltpu.VMEM((2, SUBLANES, t), F32),
           pltpu.VMEM((2, hw, t), F32)],
        compiler_params=pltpu.CompilerParams(
            dimension_semantics=("arbitrary", "arbitrary", "arbitrary"),
            vmem_limit_bytes=VMEM_LIMIT_BYTES),
        name="diff_attention",
    )(lam, q1, q2, q1, q2, k4, vt4, dtiles, cfar, subln_g, *casts.arrays)


def _row_parts(tm):
    return [slice(lo, lo + ROW_PART) for lo in range(0, tm, ROW_PART)]


def _residual_ffn(x_ref, mixes, parts, ng_ref, wg_ref, wu_ref, wd_ref, o_ref):
    x1s, h2s, acts = [], [], []
    for p, mix in zip(parts, mixes):
        x1 = x_ref[p, :] + _rms(mix, ng_ref[1:2, :])
        x1s.append(x1)
        h2s.append(_rms(x1, ng_ref[2:3, :]).astype(BF16))
    for h2 in h2s:
        gate = jnp.dot(h2, wg_ref[...], preferred_element_type=F32)
        up = jnp.dot(h2, wu_ref[...], preferred_element_type=F32)
        acts.append((gate * jax.nn.sigmoid(gate) * up).astype(BF16))
    for p, x1, act in zip(parts, x1s, acts):
        f = jnp.dot(act, wd_ref[...], preferred_element_type=F32)
        o_ref[p, :] = x1 + _rms(f, ng_ref[3:4, :])


def _even_out_kernel(x_ref, a_ref, bo_ref, wo_ref, ng_ref, wg_ref, wu_ref, wd_ref, o_ref):
    aw = a_ref.shape[1]
    parts = _row_parts(x_ref.shape[0])
    mixes = [jnp.dot(a_ref[p, :], wo_ref[0:aw, :], preferred_element_type=F32)
             + jnp.dot(bo_ref[p, :], wo_ref[aw:, :], preferred_element_type=F32) for p in parts]
    _residual_ffn(x_ref, mixes, parts, ng_ref, wg_ref, wu_ref, wd_ref, o_ref)


def _even_out(x2d, a2d, bo2d, wo_bf, ng, wg_bf, wu_bf, wd_bf, layer):
    n, d = x2d.shape
    tm = EVEN_TILE
    tok = lambda w: pl.BlockSpec((tm, w), lambda i: (i, 0))
    return pl.pallas_call(
        _even_out_kernel,
        grid=(n // tm,),
        in_specs=[tok(d), tok(a2d.shape[1]), tok(bo2d.shape[1]), _const_spec(wo_bf.shape),
                  _layer_spec(ng, layer), _const_spec(wg_bf.shape), _const_spec(wu_bf.shape),
                  _const_spec(wd_bf.shape)],
        out_specs=tok(d),
        out_shape=jax.ShapeDtypeStruct((n, d), F32),
        compiler_params=pltpu.CompilerParams(dimension_semantics=("arbitrary",),
                                             vmem_limit_bytes=VMEM_LIMIT_BYTES),
        name="even_out_ffn",
    )(x2d, a2d, bo2d, wo_bf, ng, wg_bf, wu_bf, wd_bf)


def _odd_kernel(x_ref, wi_ref, lng_ref, lnb_ref, ws_ref, sb_ref, wo_ref, ng_ref,
                wg_ref, wu_ref, wd_ref, o_ref):
    sw = wo_ref.shape[0]
    gd = sw // SGU_GROUPS
    parts = _row_parts(x_ref.shape[0])
    nch = (parts[0].stop - parts[0].start) // CHUNK

    def gelu(z):
        return 0.5 * z * (1.0 + lax.erf(z * math.sqrt(0.5)))

    def gate_inputs(h):
        u = gelu(jnp.dot(h, wi_ref[:, 0:sw], preferred_element_type=F32))
        v = gelu(jnp.dot(h, wi_ref[:, sw:], preferred_element_type=F32))
        mu = jnp.mean(v, axis=-1, keepdims=True)
        vc = v - mu
        v = (vc * lax.rsqrt(jnp.mean(vc * vc, axis=-1, keepdims=True) + LN_EPS) * lng_ref[...]
             + lnb_ref[...]).astype(BF16)
        return u, v

    def spatial_gate(u, v):
        tiles = [[None] * SGU_GROUPS for _ in range(nch)]
        for g in range(SGU_GROUPS):
            rhs = jnp.concatenate([v[n * CHUNK:(n + 1) * CHUNK, g * gd:(g + 1) * gd] for n in range(nch)],
                                  axis=1)
            mixed = jnp.dot(ws_ref[g], rhs, preferred_element_type=F32) + sb_ref[g]
            for n in range(nch):
                tiles[n][g] = (u[n * CHUNK:(n + 1) * CHUNK, g * gd:(g + 1) * gd]
                               * mixed[:, n * gd:(n + 1) * gd])
        gated = jnp.concatenate([jnp.concatenate(r, axis=1) for r in tiles], axis=0).astype(BF16)
        return jnp.dot(gated, wo_ref[...], preferred_element_type=F32)

    hs = [_rms(x_ref[p, :], ng_ref[0:1, :]).astype(BF16) for p in parts]
    uvs = [gate_inputs(h) for h in hs]
    mixes = [spatial_gate(u, v) for u, v in uvs]
    _residual_ffn(x_ref, mixes, parts, ng_ref, wg_ref, wu_ref, wd_ref, o_ref)


def _odd_layer(x2d, wi_bf, ln_g, ln_b, ws_bf, sb, wo_bf, ng, wg_bf, wu_bf, wd_bf, layer):
    n, d = x2d.shape
    tm = ODD_TILE
    tok = pl.BlockSpec((tm, d), lambda i: (i, 0))
    consts = (wi_bf, ln_g, ln_b, ws_bf, sb, wo_bf, ng, wg_bf, wu_bf, wd_bf)
    whole = lambda a: _const_spec(a.shape)
    odd = lambda a: _layer_spec(a, layer // 2)
    return pl.pallas_call(
        _odd_kernel,
        grid=(n // tm,),
        in_specs=[tok, whole(wi_bf), odd(ln_g), odd(ln_b), odd(ws_bf), odd(sb), whole(wo_bf),
                  _layer_spec(ng, layer), whole(wg_bf), whole(wu_bf), whole(wd_bf)],
        out_specs=tok,
        out_shape=jax.ShapeDtypeStruct((n, d), F32),
        compiler_params=pltpu.CompilerParams(dimension_semantics=("arbitrary",),
                                             vmem_limit_bytes=VMEM_LIMIT_BYTES),
        name="odd_sgu_ffn",
    )(x2d, *consts)


def _t5_bucket_of_distance(n):
    max_exact = REL_BUCKETS // 2
    nf = jnp.maximum(n, 1).astype(F32)
    large = max_exact + (jnp.log(nf / max_exact) / math.log(REL_MAX_DIST / max_exact)
                         * (REL_BUCKETS - max_exact)).astype(jnp.int32)
    large = jnp.minimum(large, REL_BUCKETS - 1)
    return jnp.where(n < max_exact, n, large)


def _bias_tiles(rel_bias, sb, t):
    assert sb >= REL_MAX_DIST
    heads = rel_bias.shape[1]
    table = rel_bias.astype(F32).T
    span = 2 * sb
    hit = _t5_bucket_of_distance(jnp.arange(span, dtype=jnp.int32))[:, None] == jnp.arange(REL_BUCKETS)
    far = table[:, REL_BUCKETS - 1:]
    by_dist = (jnp.sum(jnp.where(hit[None], table[:, None, :], 0.0), axis=-1) - far) * LOG2_E

    def toeplitz(v):
        rows = jnp.broadcast_to(v[:, None, :], (heads, sb, span))
        skew = jnp.pad(rows, ((0, 0), (0, 0), (0, 1))).reshape(heads, sb * (span + 1))
        return skew[:, :sb * span].reshape(heads, sb, span)[:, :, sb:]

    prev = toeplitz(by_dist)
    diag = toeplitz(jnp.concatenate(
        [jnp.full((heads, sb), MASK_VALUE, F32), by_dist[:, :sb]], axis=1))
    tiles = jnp.stack([diag, prev], axis=1)
    cfar = jnp.broadcast_to((far * LOG2_E)[:, :, None], (heads, 1, t))
    return tiles, cfar


def kernel(x, rel_bias, w_in_even, diff_lambda, diff_subln_g, conv_w, w_out_even, w_in_odd,
           sgu_ln_g, sgu_ln_b, sgu_w, sgu_b, w_out_odd, norm_g, w_gate, w_up, w_down):
    bsz, seq, d = x.shape
    depth = norm_g.shape[0]
    t = ATTN_BLOCK
    assert seq % t == 0 and t % EVEN_TILE == 0 and t % BIAS_TILE == 0
    assert seq % ODD_TILE == 0 and ROW_PART % CHUNK == 0
    x2d = x.reshape(bsz * seq, d)
    dtiles, cfar = _bias_tiles(rel_bias, BIAS_TILE, t)
    tril = jnp.tril(jnp.ones((CHUNK, CHUNK), dtype=bool))
    w_in_even_bf = w_in_even.astype(BF16)
    ws = jnp.where(tril, sgu_w, 0.0).astype(BF16)
    ffn = lambda i: [(w_gate, i), (w_up, i), (w_down, i)]
    odd_bf = None
    for i in range(depth):
        if i % 2 == 0:
            lambda_init = 0.8 - 0.6 * math.exp(-0.3 * i)
            q1, q2, k, vt, bo, wo_bf, wg, wu, wd = _even_in(
                x2d, norm_g, w_in_even_bf, conv_w, seq, i, [(w_out_even, i // 2)] + ffn(i))
            nxt = [(w_in_odd, i // 2), (w_out_odd, i // 2)] + ffn(i + 1) if i + 1 < depth else []
            a, *odd_bf = _diff_attention(
                diff_lambda, q1.reshape(bsz, seq, A_QK), q2.reshape(bsz, seq, A_QK),
                k.reshape(bsz, seq // t, t, A_QK), vt.reshape(bsz, seq // t, A_QK, t),
                dtiles, cfar, diff_subln_g[:, None, :], lambda_init, i, nxt)
            x2d = _even_out(x2d, a.reshape(bsz * seq, A_QK), bo, wo_bf, norm_g, wg, wu, wd, i)
        else:
            wi_bf, wo_bf, wg, wu, wd = odd_bf
            x2d = _odd_layer(x2d, wi_bf, sgu_ln_g[:, None, :], sgu_ln_b[:, None, :], ws,
                             sgu_b[..., None], wo_bf, norm_g, wg, wu, wd, i)
    return x2d.reshape(bsz, seq, d)
```

```python
import functools
import math

import jax
import jax.numpy as jnp
from jax import lax
from jax.experimental import pallas as pl
from jax.experimental.pallas import tpu as pltpu

F32 = jnp.float32
BF16 = jnp.bfloat16

DIFF_HEADS = 4
DIFF_QK_DIM = 64
HEAD_WIDTH = 2 * DIFF_QK_DIM
A_QK = DIFF_HEADS * HEAD_WIDTH
CONV_WIDTH = 3
SGU_GROUPS = 8
CHUNK = 128
REL_BUCKETS = 32
REL_MAX_DIST = 128
RMS_EPS = 1e-6
SUBLN_EPS = 1e-5
LN_EPS = 1e-5
MASK_VALUE = -1e30
LOG2_E = math.log2(math.e)

SUBLANES = 8
BF16_SUBLANES = 16
ROW_PART = 256
EVEN_TILE = 4 * ROW_PART
ODD_TILE = 2 * ROW_PART
ATTN_BLOCK = 1024
BIAS_TILE = 256
VMEM_LIMIT_BYTES = 56 * 1024 * 1024

_NT = (((1,), (1,)), ((), ()))


def _rms(x, g, eps=RMS_EPS):
    return x * lax.rsqrt(jnp.mean(x * x, axis=-1, keepdims=True) + eps) * g


def _const_spec(shape):
    return pl.BlockSpec(shape, lambda *_: (0,) * len(shape), pipeline_mode=pl.Buffered(1))


def _layer_spec(stacked, layer):
    rest = stacked.shape[1:]
    return pl.BlockSpec((None,) + rest, lambda *_: (layer,) + (0,) * len(rest),
                        pipeline_mode=pl.Buffered(1))


class _CastPlan:
    def __init__(self, weights, nsteps, step_of):
        self.arrays = [w for w, _ in weights]
        self.in_specs, self.out_specs, self.out_shapes = [], [], []
        for w, layer in weights:
            _, r, c = w.shape
            slab = next(s for s in range(BF16_SUBLANES, r + 1, BF16_SUBLANES)
                        if r % s == 0 and nsteps % (r // s) == 0)
            per_slab = nsteps // (r // slab)
            self.in_specs.append(pl.BlockSpec(
                (None, slab, c), lambda *g, layer=layer, per=per_slab: (layer, step_of(*g) // per, 0)))
            self.out_specs.append(pl.BlockSpec(
                (slab, c), lambda *g, per=per_slab: (step_of(*g) // per, 0)))
            self.out_shapes.append(jax.ShapeDtypeStruct((r, c), BF16))

    def __len__(self):
        return len(self.arrays)


def _cast_slabs(in_refs, out_refs):
    for src, dst in zip(in_refs, out_refs):
        dst[...] = src[...].astype(BF16)


def _even_in_kernel(x_ref, ng_ref, w_ref, cw_ref, *refs, tiles_per_seq, n_cast):
    cast_in, refs = refs[:n_cast], refs[n_cast:]
    q1_ref, q2_ref, k_ref, vt_ref, bo_ref = refs[:5]
    cast_out, (carry_ref,) = refs[5:5 + n_cast], refs[5 + n_cast:]
    _cast_slabs(cast_in, cast_out)
    tm = x_ref.shape[0]
    bw = bo_ref.shape[1]

    @pl.when(pl.program_id(0) % tiles_per_seq == 0)
    def _():
        carry_ref[...] = jnp.zeros_like(carry_ref)

    parts = _row_parts(tm)
    hs = [_rms(x_ref[p, :], ng_ref[0:1, :]).astype(BF16) for p in parts]
    c0 = 3 * A_QK
    b_parts, z_parts = [], []
    for p, h in zip(parts, hs):
        proj = jnp.dot(h, w_ref[...], preferred_element_type=F32)
        q = proj[:, 0:A_QK] * (DIFF_QK_DIM ** -0.5 * LOG2_E)
        first_map = (lax.broadcasted_iota(jnp.int32, q.shape, 1) % HEAD_WIDTH) < DIFF_QK_DIM
        q1_ref[p, :] = jnp.where(first_map, q, 0.0).astype(BF16)
        q2_ref[p, :] = jnp.where(first_map, 0.0, q).astype(BF16)
        k_ref[p, :] = proj[:, A_QK:2 * A_QK].astype(BF16)
        vt_ref[:, p] = proj[:, 2 * A_QK:c0].T.astype(BF16)
        b_parts.append(proj[:, c0:c0 + bw])
        z_parts.append(proj[:, c0 + bw:c0 + 2 * bw] * proj[:, c0 + 2 * bw:c0 + 3 * bw])
    b_gate = jnp.concatenate(b_parts, axis=0)
    z = jnp.concatenate(z_parts, axis=0)
    row = lax.broadcasted_iota(jnp.int32, z.shape, 0)
    prev = carry_ref[...]
    zm1 = jnp.where(row == 0, prev[SUBLANES - 1:SUBLANES], pltpu.roll(z, 1, 0))
    zm2 = jnp.where(row == 0, prev[SUBLANES - 2:SUBLANES - 1],
                    jnp.where(row == 1, prev[SUBLANES - 1:SUBLANES], pltpu.roll(z, 2, 0)))
    cw = cw_ref[...]
    y = cw[0:1] * zm2 + cw[1:2] * zm1 + cw[2:3] * z
    bo_ref[...] = (b_gate * y).astype(BF16)
    carry_ref[...] = z[tm - SUBLANES:, :]


def _even_in(x2d, norm_g, w_bf, conv_w, seq, layer, to_cast):
    n, d = x2d.shape
    tm = EVEN_TILE
    t = ATTN_BLOCK
    per_blk = t // tm
    bw = conv_w.shape[-1]
    casts = _CastPlan(to_cast, n // tm, lambda i: i)
    kern = functools.partial(_even_in_kernel, tiles_per_seq=seq // tm, n_cast=len(casts))
    tok = lambda w: pl.BlockSpec((tm, w), lambda i: (i, 0))
    return pl.pallas_call(
        kern,
        grid=(n // tm,),
        in_specs=[tok(d), _layer_spec(norm_g, layer), _layer_spec(w_bf, layer // 2),
                  _layer_spec(conv_w, layer // 2)] + casts.in_specs,
        out_specs=[tok(A_QK), tok(A_QK), tok(A_QK),
                   pl.BlockSpec((None, A_QK, tm), lambda i: (i // per_blk, 0, i % per_blk)),
                   tok(bw)] + casts.out_specs,
        out_shape=[jax.ShapeDtypeStruct((n, A_QK), BF16)] * 3
        + [jax.ShapeDtypeStruct((n // t, A_QK, t), BF16), jax.ShapeDtypeStruct((n, bw), BF16)]
        + casts.out_shapes,
        scratch_shapes=[pltpu.VMEM((SUBLANES, bw), F32)],
        compiler_params=pltpu.CompilerParams(dimension_semantics=("arbitrary",),
                                             vmem_limit_bytes=VMEM_LIMIT_BYTES),
        name="even_in_proj",
    )(x2d, norm_g, w_bf, conv_w, *casts.arrays)


_DIAG, _PREV = 0, 1


def _patched(s, tile_ref, patches):
    sb = tile_ref.shape[-1]
    for r, c, idx, scale in patches:
        band = s[r:r + sb, :]
        cols = [band[:, :c]] if c else []
        tile = tile_ref[idx] if scale is None else tile_ref[idx] * scale
        cols.append(band[:, c:c + sb] + tile)
        if c + sb < s.shape[1]:
            cols.append(band[:, c + sb:])
        rows = [s[:r]] if r else []
        rows.append(jnp.concatenate(cols, axis=1) if len(cols) > 1 else cols[0])
        if r + sb < s.shape[0]:
            rows.append(s[r + sb:])
        s = jnp.concatenate(rows, axis=0) if len(rows) > 1 else rows[0]
    return s


def _fold8(x, op):
    return op(x.reshape(x.shape[0] // SUBLANES, SUBLANES, x.shape[1]), axis=0)


def _lpad(x, width, value):
    if not width:
        return x
    return jnp.concatenate([jnp.full((x.shape[0], width), value, x.dtype), x], axis=1)


def _attn_kernel(lam_ref, q1_ref, q2_ref, q1n_ref, q2n_ref, k_ref, vt_ref, dt_ref, cfar_ref, g_ref,
                 *refs, lambda_init, n_cast):
    cast_in, o_ref, cast_out = refs[:n_cast], refs[n_cast], refs[n_cast + 1:2 * n_cast + 1]
    sa_ref, sb_ref, sd_ref, ca_ref, cb_ref, cd_ref, m_ref, l_ref, acc_ref = refs[2 * n_cast + 1:]
    _cast_slabs(cast_in, cast_out)
    i = pl.program_id(2)
    nblk = k_ref.shape[0]
    t = q1_ref.shape[0]
    sb = dt_ref.shape[-1]
    ns = t // sb
    q_refs = (q1_ref, q2_ref)
    qn_refs = (q1n_ref, q2n_ref)
    cfar = cfar_ref[...]
    rows = [slice(a * sb, (a + 1) * sb) for a in range(ns)]
    hw = vt_ref.shape[1]

    def value_rows(blk, a):
        return jnp.concatenate([vt_ref[blk, :, rows[a]], jnp.ones((BF16_SUBLANES, sb), BF16)], axis=0)

    def fill_piece(blk, a, mp, s_ref, cmax):
        s = lax.dot_general(k_ref[blk, rows[a], :], q_refs[mp][...], _NT, preferred_element_type=F32)
        if a == ns - 1:
            is_prev = jnp.where(blk == i - 1, 1.0, 0.0).astype(F32)
            s = _patched(s, dt_ref, [(0, 0, _PREV, is_prev)])
        s_ref[mp, rows[a], :] = s
        pm = _fold8(s, jnp.max)
        return pm if cmax is None else jnp.maximum(cmax, pm)

    def past_producer(blk, s_ref, c_ref):
        cmax = [None, None]

        def piece(a, mp):
            cmax[mp] = fill_piece(blk, a, mp, s_ref, cmax[mp])

        def finish():
            for mp in range(2):
                c_ref[mp] = jnp.max(cmax[mp], axis=0, keepdims=True)
        return piece, finish

    def diag_strip(qr, blk, a, mp):
        s = lax.dot_general(k_ref[blk, rows[a], :], qr[mp][a * sb:, :], _NT,
                            preferred_element_type=F32)
        patches = [(0, 0, _DIAG, None)] + ([(0, sb, _PREV, None)] if a + 1 < ns else [])
        return _patched(s, dt_ref, patches)

    def strip_max(s, a, smax):
        pm = _lpad(jnp.max(s, axis=0, keepdims=True), a * sb, MASK_VALUE)
        return pm if smax is None else jnp.maximum(smax, pm)

    def next_diag_producer():
        blk = jnp.minimum(i + 1, nblk - 1)
        smax = [None, None]

        def piece(a, mp):
            s = diag_strip(qn_refs, blk, a, mp)
            sd_ref[mp, rows[a], a * sb:] = s
            smax[mp] = strip_max(s, a, smax[mp])

        def finish():
            for mp in range(2):
                cd_ref[mp] = smax[mp]
        return piece, finish

    def past_step(blk, cur_s, cur_c, producer):
        piece, finish = producer
        shift, alpha = [], []
        for mp in range(2):
            m_old = m_ref[mp]
            m_new = jnp.maximum(m_old, cur_c[mp] + cfar)
            m_ref[mp] = m_new
            shift.append(m_new - cfar)
            alpha.append(jnp.exp2(m_old - m_new))
        pv = [None, None]
        for a in range(ns):
            for mp in range(2):
                e = jnp.exp2(cur_s[mp, rows[a], :] - shift[mp])
                piece(a, mp)
                d = jnp.dot(value_rows(blk, a), e.astype(BF16), preferred_element_type=F32)
                pv[mp] = d if pv[mp] is None else pv[mp] + d
        finish()
        for mp in range(2):
            acc_ref[mp] = alpha[mp] * acc_ref[mp] + pv[mp][:hw]
            l_ref[mp] = alpha[mp] * l_ref[mp] + pv[mp][hw:hw + SUBLANES] * (1.0 / SUBLANES)

    def diag_step(strip, smax, producer):
        piece, finish = producer
        lsum = [None, None]
        for a in range(ns):
            q_lo = a * sb
            for mp in range(2):
                e = jnp.exp2(strip(mp, a) - smax[mp][:, q_lo:])
                piece(a, mp)
                d = jnp.dot(value_rows(i, a), e.astype(BF16), preferred_element_type=F32)
                ps = _lpad(d[hw:hw + SUBLANES], q_lo, 0.0)
                lsum[mp] = ps if lsum[mp] is None else lsum[mp] + ps
                if a == 0:
                    acc_ref[mp] = d[:hw]
                else:
                    acc_ref[mp, :, q_lo:] += d[:hw]
        finish()
        for mp in range(2):
            m_ref[mp] = smax[mp] + cfar
            l_ref[mp] = lsum[mp] * (1.0 / SUBLANES)

    @pl.when(i == 0)
    def _():
        strips, smax = [[None] * ns, [None] * ns], [None, None]
        for a in range(ns):
            for mp in range(2):
                strips[mp][a] = diag_strip(q_refs, i, a, mp)
                smax[mp] = strip_max(strips[mp][a], a, smax[mp])
        diag_step(lambda mp, a: strips[mp][a], smax, next_diag_producer())

    @pl.when(i > 0)
    def _():
        diag_step(lambda mp, a: sd_ref[mp, rows[a], a * sb:], [cd_ref[0], cd_ref[1]],
                  past_producer(0, sa_ref, ca_ref))

    def pair_body(p, carry):
        past_step(2 * p, sa_ref, ca_ref, past_producer(2 * p + 1, sb_ref, cb_ref))
        past_step(2 * p + 1, sb_ref, cb_ref, past_producer(2 * p + 2, sa_ref, ca_ref))
        return carry

    lax.fori_loop(0, jnp.maximum(i - 1, 0) // 2, pair_body, 0)

    @pl.when(jnp.logical_and(i >= 2, i % 2 == 0))
    def _():
        past_step(i - 2, sa_ref, ca_ref, past_producer(i - 1, sb_ref, cb_ref))
        past_step(i - 1, sb_ref, cb_ref, next_diag_producer())

    @pl.when(i % 2 == 1)
    def _():
        past_step(i - 1, sa_ref, ca_ref, next_diag_producer())

    lam = lam_ref[...]
    lam_full = (jnp.exp(jnp.sum(lam[0:1] * lam[1:2], axis=1, keepdims=True))
                - jnp.exp(jnp.sum(lam[2:3] * lam[3:4], axis=1, keepdims=True)) + lambda_init)
    inv_l = [1.0 / jnp.sum(l_ref[mp], axis=0, keepdims=True) for mp in range(2)]
    o = acc_ref[0] * inv_l[0] - lam_full * (acc_ref[1] * inv_l[1])
    y = o * lax.rsqrt(jnp.mean(o * o, axis=0, keepdims=True) + SUBLN_EPS)
    o_ref[...] = (y.T * (g_ref[...] * (1.0 - lambda_init))).astype(BF16)


def _diff_attention(lam, q1, q2, k4, vt4, dtiles, cfar, subln_g, lambda_init, layer, to_cast):
    bsz, nblk, t, _ = k4.shape
    seq = nblk * t
    hw = HEAD_WIDTH
    sb = dtiles.shape[-1]
    casts = _CastPlan(to_cast, bsz * DIFF_HEADS * nblk,
                      lambda b, h, i: (b * DIFF_HEADS + h) * nblk + i)
    kern = functools.partial(_attn_kernel, lambda_init=lambda_init, n_cast=len(casts))
    qspec = pl.BlockSpec((None, t, hw), lambda b, h, i: (b, i, h))
    qnext = pl.BlockSpec((None, t, hw), lambda b, h, i: (b, jnp.minimum(i + 1, nblk - 1), h))
    return pl.pallas_call(
        kern,
        grid=(bsz, DIFF_HEADS, nblk),
        in_specs=[
            _layer_spec(lam, layer // 2),
            qspec, qspec, qnext, qnext,
            pl.BlockSpec((None, nblk, t, hw), lambda b, h, i: (b, 0, 0, h)),
            pl.BlockSpec((None, nblk, hw, t), lambda b, h, i: (b, 0, h, 0)),
            pl.BlockSpec((None, 2, sb, sb), lambda b, h, i: (h, 0, 0, 0)),
            pl.BlockSpec((None, 1, t), lambda b, h, i: (h, 0, 0)),
            _layer_spec(subln_g, layer // 2),
        ] + casts.in_specs,
        out_specs=[qspec] + casts.out_specs,
        out_shape=[jax.ShapeDtypeStruct((bsz, seq, A_QK), BF16)] + casts.out_shapes,
        scratch_shapes=[pltpu.VMEM((2, t, t), F32)] * 3
        + [pltpu.VMEM((2, 1, t), F32)] * 3
        + [pltpu.VMEM((2, 1, t), F32),
           pltpu.VMEM((2, SUBLANES, t), F32),
           pltpu.VMEM((2, hw, t), F32)],
        compiler_params=pltpu.CompilerParams(
            dimension_semantics=("arbitrary", "arbitrary", "arbitrary"),
            vmem_limit_bytes=VMEM_LIMIT_BYTES),
        name="diff_attention",
    )(lam, q1, q2, q1, q2, k4, vt4, dtiles, cfar, subln_g, *casts.arrays)


def _row_parts(tm):
    return [slice(lo, lo + ROW_PART) for lo in range(0, tm, ROW_PART)]


def _residual_ffn(x_ref, mixes, parts, ng_ref, wg_ref, wu_ref, wd_ref, o_ref):
    x1s, h2s, acts = [], [], []
    for p, mix in zip(parts, mixes):
        x1 = x_ref[p, :] + _rms(mix, ng_ref[1:2, :])
        x1s.append(x1)
        h2s.append(_rms(x1, ng_ref[2:3, :]).astype(BF16))
    for h2 in h2s:
        gate = jnp.dot(h2, wg_ref[...], preferred_element_type=F32)
        up = jnp.dot(h2, wu_ref[...], preferred_element_type=F32)
        acts.append((gate * jax.nn.sigmoid(gate) * up).astype(BF16))
    for p, x1, act in zip(parts, x1s, acts):
        f = jnp.dot(act, wd_ref[...], preferred_element_type=F32)
        o_ref[p, :] = x1 + _rms(f, ng_ref[3:4, :])


def _even_out_kernel(x_ref, a_ref, bo_ref, wo_ref, ng_ref, wg_ref, wu_ref, wd_ref, o_ref):
    aw = a_ref.shape[1]
    parts = _row_parts(x_ref.shape[0])
    mixes = [jnp.dot(a_ref[p, :], wo_ref[0:aw, :], preferred_element_type=F32)
             + jnp.dot(bo_ref[p, :], wo_ref[aw:, :], preferred_element_type=F32) for p in parts]
    _residual_ffn(x_ref, mixes, parts, ng_ref, wg_ref, wu_ref, wd_ref, o_ref)


def _even_out(x2d, a2d, bo2d, wo_bf, ng, wg_bf, wu_bf, wd_bf, layer):
    n, d = x2d.shape
    tm = EVEN_TILE
    tok = lambda w: pl.BlockSpec((tm, w), lambda i: (i, 0))
    return pl.pallas_call(
        _even_out_kernel,
        grid=(n // tm,),
        in_specs=[tok(d), tok(a2d.shape[1]), tok(bo2d.shape[1]), _const_spec(wo_bf.shape),
                  _layer_spec(ng, layer), _const_spec(wg_bf.shape), _const_spec(wu_bf.shape),
                  _const_spec(wd_bf.shape)],
        out_specs=tok(d),
        out_shape=jax.ShapeDtypeStruct((n, d), F32),
        compiler_params=pltpu.CompilerParams(dimension_semantics=("arbitrary",),
                                             vmem_limit_bytes=VMEM_LIMIT_BYTES),
        name="even_out_ffn",
    )(x2d, a2d, bo2d, wo_bf, ng, wg_bf, wu_bf, wd_bf)


def _odd_kernel(x_ref, wi_ref, lng_ref, lnb_ref, ws_ref, sb_ref, wo_ref, ng_ref,
                wg_ref, wu_ref, wd_ref, o_ref):
    sw = wo_ref.shape[0]
    gd = sw // SGU_GROUPS
    parts = _row_parts(x_ref.shape[0])
    nch = (parts[0].stop - parts[0].start) // CHUNK

    def gelu(z):
        return 0.5 * z * (1.0 + lax.erf(z * math.sqrt(0.5)))

    def gate_inputs(h):
        u = gelu(jnp.dot(h, wi_ref[:, 0:sw], preferred_element_type=F32))
        v = gelu(jnp.dot(h, wi_ref[:, sw:], preferred_element_type=F32))
        mu = jnp.mean(v, axis=-1, keepdims=True)
        vc = v - mu
        v = (vc * lax.rsqrt(jnp.mean(vc * vc, axis=-1, keepdims=True) + LN_EPS) * lng_ref[...]
             + lnb_ref[...]).astype(BF16)
        return u, v

    def spatial_gate(u, v):
        tiles = [[None] * SGU_GROUPS for _ in range(nch)]
        for g in range(SGU_GROUPS):
            rhs = jnp.concatenate([v[n * CHUNK:(n + 1) * CHUNK, g * gd:(g + 1) * gd] for n in range(nch)],
                                  axis=1)
            mixed = jnp.dot(ws_ref[g], rhs, preferred_element_type=F32) + sb_ref[g]
            for n in range(nch):
                tiles[n][g] = (u[n * CHUNK:(n + 1) * CHUNK, g * gd:(g + 1) * gd]
                               * mixed[:, n * gd:(n + 1) * gd])
        gated = jnp.concatenate([jnp.concatenate(r, axis=1) for r in tiles], axis=0).astype(BF16)
        return jnp.dot(gated, wo_ref[...], preferred_element_type=F32)

    hs = [_rms(x_ref[p, :], ng_ref[0:1, :]).astype(BF16) for p in parts]
    uvs = [gate_inputs(h) for h in hs]
    mixes = [spatial_gate(u, v) for u, v in uvs]
    _residual_ffn(x_ref, mixes, parts, ng_ref, wg_ref, wu_ref, wd_ref, o_ref)


def _odd_layer(x2d, wi_bf, ln_g, ln_b, ws_bf, sb, wo_bf, ng, wg_bf, wu_bf, wd_bf, layer):
    n, d = x2d.shape
    tm = ODD_TILE
    tok = pl.BlockSpec((tm, d), lambda i: (i, 0))
    consts = (wi_bf, ln_g, ln_b, ws_bf, sb, wo_bf, ng, wg_bf, wu_bf, wd_bf)
    whole = lambda a: _const_spec(a.shape)
    odd = lambda a: _layer_spec(a, layer // 2)
    return pl.pallas_call(
        _odd_kernel,
        grid=(n // tm,),
        in_specs=[tok, whole(wi_bf), odd(ln_g), odd(ln_b), odd(ws_bf), odd(sb), whole(wo_bf),
                  _layer_spec(ng, layer), whole(wg_bf), whole(wu_bf), whole(wd_bf)],
        out_specs=tok,
        out_shape=jax.ShapeDtypeStruct((n, d), F32),
        compiler_params=pltpu.CompilerParams(dimension_semantics=("arbitrary",),
                                             vmem_limit_bytes=VMEM_LIMIT_BYTES),
        name="odd_sgu_ffn",
    )(x2d, *consts)


def _t5_bucket_of_distance(n):
    max_exact = REL_BUCKETS // 2
    nf = jnp.maximum(n, 1).astype(F32)
    large = max_exact + (jnp.log(nf / max_exact) / math.log(REL_MAX_DIST / max_exact)
                         * (REL_BUCKETS - max_exact)).astype(jnp.int32)
    large = jnp.minimum(large, REL_BUCKETS - 1)
    return jnp.where(n < max_exact, n, large)


def _bias_tiles(rel_bias, sb, t):
    assert sb >= REL_MAX_DIST
    heads = rel_bias.shape[1]
    table = rel_bias.astype(F32).T
    span = 2 * sb
    hit = _t5_bucket_of_distance(jnp.arange(span, dtype=jnp.int32))[:, None] == jnp.arange(REL_BUCKETS)
    far = table[:, REL_BUCKETS - 1:]
    by_dist = (jnp.sum(jnp.where(hit[None], table[:, None, :], 0.0), axis=-1) - far) * LOG2_E

    def toeplitz(v):
        rows = jnp.broadcast_to(v[:, None, :], (heads, sb, span))
        skew = jnp.pad(rows, ((0, 0), (0, 0), (0, 1))).reshape(heads, sb * (span + 1))
        return skew[:, :sb * span].reshape(heads, sb, span)[:, :, sb:]

    prev = toeplitz(by_dist)
    diag = toeplitz(jnp.concatenate(
        [jnp.full((heads, sb), MASK_VALUE, F32), by_dist[:, :sb]], axis=1))
    tiles = jnp.stack([diag, prev], axis=1)
    cfar = jnp.broadcast_to((far * LOG2_E)[:, :, None], (heads, 1, t))
    return tiles, cfar


def kernel(x, rel_bias, w_in_even, diff_lambda, diff_subln_g, conv_w, w_out_even, w_in_odd,
           sgu_ln_g, sgu_ln_b, sgu_w, sgu_b, w_out_odd, norm_g, w_gate, w_up, w_down):
    bsz, seq, d = x.shape
    depth = norm_g.shape[0]
    t = ATTN_BLOCK
    assert seq % t == 0 and t % EVEN_TILE == 0 and t % BIAS_TILE == 0
    assert seq % ODD_TILE == 0 and ROW_PART % CHUNK == 0
    x2d = x.reshape(bsz * seq, d)
    dtiles, cfar = _bias_tiles(rel_bias, BIAS_TILE, t)
    tril = jnp.tril(jnp.ones((CHUNK, CHUNK), dtype=bool))
    w_in_even_bf = w_in_even.astype(BF16)
    ws = jnp.where(tril, sgu_w, 0.0).astype(BF16)
    ffn = lambda i: [(w_gate, i), (w_up, i), (w_down, i)]
    odd_bf = None
    for i in range(depth):
        if i % 2 == 0:
            lambda_init = 0.8 - 0.6 * math.exp(-0.3 * i)
            q1, q2, k, vt, bo, wo_bf, wg, wu, wd = _even_in(
                x2d, norm_g, w_in_even_bf, conv_w, seq, i, [(w_out_even, i // 2)] + ffn(i))
            nxt = [(w_in_odd, i // 2), (w_out_odd, i // 2)] + ffn(i + 1) if i + 1 < depth else []
            a, *odd_bf = _diff_attention(
                diff_lambda, q1.reshape(bsz, seq, A_QK), q2.reshape(bsz, seq, A_QK),
                k.reshape(bsz, seq // t, t, A_QK), vt.reshape(bsz, seq // t, A_QK, t),
                dtiles, cfar, diff_subln_g[:, None, :], lambda_init, i, nxt)
            x2d = _even_out(x2d, a.reshape(bsz * seq, A_QK), bo, wo_bf, norm_g, wg, wu, wd, i)
        else:
            wi_bf, wo_bf, wg, wu, wd = odd_bf
            x2d = _odd_layer(x2d, wi_bf, sgu_ln_g[:, None, :], sgu_ln_b[:, None, :], ws,
                             sgu_b[..., None], wo_bf, norm_g, wg, wu, wd, i)
    return x2d.reshape(bsz, seq, d)
```

```python
import functools
import math

import jax
import jax.numpy as jnp
from jax import lax
from jax.experimental import pallas as pl
from jax.experimental.pallas import tpu as pltpu

F32 = jnp.float32
BF16 = jnp.bfloat16

DIFF_HEADS = 4
DIFF_QK_DIM = 64
HEAD_WIDTH = 2 * DIFF_QK_DIM
A_QK = DIFF_HEADS * HEAD_WIDTH
CONV_WIDTH = 3
SGU_GROUPS = 8
CHUNK = 128
REL_BUCKETS = 32
REL_MAX_DIST = 128
RMS_EPS = 1e-6
SUBLN_EPS = 1e-5
LN_EPS = 1e-5
MASK_VALUE = -1e30
LOG2_E = math.log2(math.e)

SUBLANES = 8
BF16_SUBLANES = 16
ROW_PART = 256
EVEN_TILE = 4 * ROW_PART
ODD_TILE = 2 * ROW_PART
ATTN_BLOCK = 1024
BIAS_TILE = 256
VMEM_LIMIT_BYTES = 60 * 1024 * 1024

_NT = (((1,), (1,)), ((), ()))


def _rms(x, g, eps=RMS_EPS):
    return x * lax.rsqrt(jnp.mean(x * x, axis=-1, keepdims=True) + eps) * g


def _const_spec(shape):
    return pl.BlockSpec(shape, lambda *_: (0,) * len(shape), pipeline_mode=pl.Buffered(1))


def _layer_spec(stacked, layer):
    rest = stacked.shape[1:]
    return pl.BlockSpec((None,) + rest, lambda *_: (layer,) + (0,) * len(rest),
                        pipeline_mode=pl.Buffered(1))


class _CastPlan:
    def __init__(self, weights, nsteps, step_of):
        self.arrays = [w for w, _ in weights]
        self.in_specs, self.out_specs, self.out_shapes = [], [], []
        for w, layer in weights:
            _, r, c = w.shape
            slab = next(s for s in range(BF16_SUBLANES, r + 1, BF16_SUBLANES)
                        if r % s == 0 and nsteps % (r // s) == 0)
            per_slab = nsteps // (r // slab)
            self.in_specs.append(pl.BlockSpec(
                (None, slab, c), lambda *g, layer=layer, per=per_slab: (layer, step_of(*g) // per, 0)))
            self.out_specs.append(pl.BlockSpec(
                (slab, c), lambda *g, per=per_slab: (step_of(*g) // per, 0)))
            self.out_shapes.append(jax.ShapeDtypeStruct((r, c), BF16))

    def __len__(self):
        return len(self.arrays)


def _cast_slabs(in_refs, out_refs):
    for src, dst in zip(in_refs, out_refs):
        dst[...] = src[...].astype(BF16)


def _even_in_kernel(x_ref, ng_ref, w_ref, cw_ref, *refs, tiles_per_seq, n_cast):
    cast_in, refs = refs[:n_cast], refs[n_cast:]
    q1_ref, q2_ref, k_ref, vt_ref, bo_ref = refs[:5]
    cast_out, (carry_ref,) = refs[5:5 + n_cast], refs[5 + n_cast:]
    _cast_slabs(cast_in, cast_out)
    tm = x_ref.shape[0]
    bw = bo_ref.shape[1]

    @pl.when(pl.program_id(0) % tiles_per_seq == 0)
    def _():
        carry_ref[...] = jnp.zeros_like(carry_ref)

    parts = _row_parts(tm)
    hs = [_rms(x_ref[p, :], ng_ref[0:1, :]).astype(BF16) for p in parts]
    c0 = 3 * A_QK
    b_parts, z_parts = [], []
    for p, h in zip(parts, hs):
        proj = jnp.dot(h, w_ref[...], preferred_element_type=F32)
        q = proj[:, 0:A_QK] * (DIFF_QK_DIM ** -0.5 * LOG2_E)
        first_map = (lax.broadcasted_iota(jnp.int32, q.shape, 1) % HEAD_WIDTH) < DIFF_QK_DIM
        q1_ref[p, :] = jnp.where(first_map, q, 0.0).astype(BF16)
        q2_ref[p, :] = jnp.where(first_map, 0.0, q).astype(BF16)
        k_ref[p, :] = proj[:, A_QK:2 * A_QK].astype(BF16)
        vt_ref[:, p] = proj[:, 2 * A_QK:c0].T.astype(BF16)
        b_parts.append(proj[:, c0:c0 + bw])
        z_parts.append(proj[:, c0 + bw:c0 + 2 * bw] * proj[:, c0 + 2 * bw:c0 + 3 * bw])
    b_gate = jnp.concatenate(b_parts, axis=0)
    z = jnp.concatenate(z_parts, axis=0)
    row = lax.broadcasted_iota(jnp.int32, z.shape, 0)
    prev = carry_ref[...]
    zm1 = jnp.where(row == 0, prev[SUBLANES - 1:SUBLANES], pltpu.roll(z, 1, 0))
    zm2 = jnp.where(row == 0, prev[SUBLANES - 2:SUBLANES - 1],
                    jnp.where(row == 1, prev[SUBLANES - 1:SUBLANES], pltpu.roll(z, 2, 0)))
    cw = cw_ref[...]
    y = cw[0:1] * zm2 + cw[1:2] * zm1 + cw[2:3] * z
    bo_ref[...] = (b_gate * y).astype(BF16)
    carry_ref[...] = z[tm - SUBLANES:, :]


def _even_in(x2d, norm_g, w_bf, conv_w, seq, layer, to_cast):
    n, d = x2d.shape
    tm = EVEN_TILE
    t = ATTN_BLOCK
    per_blk = t // tm
    bw = conv_w.shape[-1]
    casts = _CastPlan(to_cast, n // tm, lambda i: i)
    kern = functools.partial(_even_in_kernel, tiles_per_seq=seq // tm, n_cast=len(casts))
    tok = lambda w: pl.BlockSpec((tm, w), lambda i: (i, 0))
    return pl.pallas_call(
        kern,
        grid=(n // tm,),
        in_specs=[tok(d), _layer_spec(norm_g, layer), _layer_spec(w_bf, layer // 2),
                  _layer_spec(conv_w, layer // 2)] + casts.in_specs,
        out_specs=[tok(A_QK), tok(A_QK), tok(A_QK),
                   pl.BlockSpec((None, A_QK, tm), lambda i: (i // per_blk, 0, i % per_blk)),
                   tok(bw)] + casts.out_specs,
        out_shape=[jax.ShapeDtypeStruct((n, A_QK), BF16)] * 3
        + [jax.ShapeDtypeStruct((n // t, A_QK, t), BF16), jax.ShapeDtypeStruct((n, bw), BF16)]
        + casts.out_shapes,
        scratch_shapes=[pltpu.VMEM((SUBLANES, bw), F32)],
        compiler_params=pltpu.CompilerParams(dimension_semantics=("arbitrary",),
                                             vmem_limit_bytes=VMEM_LIMIT_BYTES),
        name="even_in_proj",
    )(x2d, norm_g, w_bf, conv_w, *casts.arrays)


_DIAG, _PREV = 0, 1


def _patched(s, tile_ref, patches):
    sb = tile_ref.shape[-1]
    for r, c, idx, scale in patches:
        band = s[r:r + sb, :]
        cols = [band[:, :c]] if c else []
        tile = tile_ref[idx] if scale is None else tile_ref[idx] * scale
        cols.append(band[:, c:c + sb] + tile)
        if c + sb < s.shape[1]:
            cols.append(band[:, c + sb:])
        rows = [s[:r]] if r else []
        rows.append(jnp.concatenate(cols, axis=1) if len(cols) > 1 else cols[0])
        if r + sb < s.shape[0]:
            rows.append(s[r + sb:])
        s = jnp.concatenate(rows, axis=0) if len(rows) > 1 else rows[0]
    return s


def _fold8(x, op):
    return op(x.reshape(x.shape[0] // SUBLANES, SUBLANES, x.shape[1]), axis=0)


def _lpad(x, width, value):
    if not width:
        return x
    return jnp.concatenate([jnp.full((x.shape[0], width), value, x.dtype), x], axis=1)


def _attn_kernel(lam_ref, q1_ref, q2_ref, q1n_ref, q2n_ref, k_ref, vt_ref, dt_ref, cfar_ref, g_ref,
                 *refs, lambda_init, n_cast):
    cast_in, o_ref, cast_out = refs[:n_cast], refs[n_cast], refs[n_cast + 1:2 * n_cast + 1]
    sa_ref, sb_ref, sd_ref, ca_ref, cb_ref, cd_ref, m_ref, l_ref, acc_ref = refs[2 * n_cast + 1:]
    _cast_slabs(cast_in, cast_out)
    i = pl.program_id(2)
    nblk = k_ref.shape[0]
    t = q1_ref.shape[0]
    sb = dt_ref.shape[-1]
    ns = t // sb
    q_refs = (q1_ref, q2_ref)
    qn_refs = (q1n_ref, q2n_ref)
    cfar = cfar_ref[...]
    rows = [slice(a * sb, (a + 1) * sb) for a in range(ns)]
    hw = vt_ref.shape[1]

    def value_rows(blk, a):
        return jnp.concatenate([vt_ref[blk, :, rows[a]], jnp.ones((BF16_SUBLANES, sb), BF16)], axis=0)

    def fill_piece(blk, a, mp, s_ref, cmax):
        s = lax.dot_general(k_ref[blk, rows[a], :], q_refs[mp][...], _NT, preferred_element_type=F32)
        if a == ns - 1:
            is_prev = jnp.where(blk == i - 1, 1.0, 0.0).astype(F32)
            s = _patched(s, dt_ref, [(0, 0, _PREV, is_prev)])
        s_ref[mp, rows[a], :] = s
        pm = _fold8(s, jnp.max)
        return pm if cmax is None else jnp.maximum(cmax, pm)

    def past_producer(blk, s_ref, c_ref):
        cmax = [None, None]

        def piece(a, mp):
            cmax[mp] = fill_piece(blk, a, mp, s_ref, cmax[mp])

        def finish():
            for mp in range(2):
                c_ref[mp] = jnp.max(cmax[mp], axis=0, keepdims=True)
        return piece, finish

    def diag_strip(qr, blk, a, mp):
        s = lax.dot_general(k_ref[blk, rows[a], :], qr[mp][a * sb:, :], _NT,
                            preferred_element_type=F32)
        patches = [(0, 0, _DIAG, None)] + ([(0, sb, _PREV, None)] if a + 1 < ns else [])
        return _patched(s, dt_ref, patches)

    def strip_max(s, a, smax):
        pm = _lpad(jnp.max(s, axis=0, keepdims=True), a * sb, MASK_VALUE)
        return pm if smax is None else jnp.maximum(smax, pm)

    def next_diag_producer():
        blk = jnp.minimum(i + 1, nblk - 1)
        smax = [None, None]

        def piece(a, mp):
            s = diag_strip(qn_refs, blk, a, mp)
            sd_ref[mp, rows[a], a * sb:] = s
            smax[mp] = strip_max(s, a, smax[mp])

        def finish():
            for mp in range(2):
                cd_ref[mp] = smax[mp]
        return piece, finish

    def past_step(blk, cur_s, cur_c, producer):
        piece, finish = producer
        shift, alpha = [], []
        for mp in range(2):
            m_old = m_ref[mp]
            m_new = jnp.maximum(m_old, cur_c[mp] + cfar)
            m_ref[mp] = m_new
            shift.append(m_new - cfar)
            alpha.append(jnp.exp2(m_old - m_new))
        pv = [None, None]
        for a in range(ns):
            for mp in range(2):
                e = jnp.exp2(cur_s[mp, rows[a], :] - shift[mp])
                piece(a, mp)
                d = jnp.dot(value_rows(blk, a), e.astype(BF16), preferred_element_type=F32)
                pv[mp] = d if pv[mp] is None else pv[mp] + d
        finish()
        for mp in range(2):
            acc_ref[mp] = alpha[mp] * acc_ref[mp] + pv[mp][:hw]
            l_ref[mp] = alpha[mp] * l_ref[mp] + pv[mp][hw:hw + SUBLANES] * (1.0 / SUBLANES)

    def diag_step(strip, smax, producer):
        piece, finish = producer
        lsum = [None, None]
        for a in range(ns):
            q_lo = a * sb
            for mp in range(2):
                e = jnp.exp2(strip(mp, a) - smax[mp][:, q_lo:])
                piece(a, mp)
                d = jnp.dot(value_rows(i, a), e.astype(BF16), preferred_element_type=F32)
                ps = _lpad(d[hw:hw + SUBLANES], q_lo, 0.0)
                lsum[mp] = ps if lsum[mp] is None else lsum[mp] + ps
                if a == 0:
                    acc_ref[mp] = d[:hw]
                else:
                    acc_ref[mp, :, q_lo:] += d[:hw]
        finish()
        for mp in range(2):
            m_ref[mp] = smax[mp] + cfar
            l_ref[mp] = lsum[mp] * (1.0 / SUBLANES)

    @pl.when(i == 0)
    def _():
        strips, smax = [[None] * ns, [None] * ns], [None, None]
        for a in range(ns):
            for mp in range(2):
                strips[mp][a] = diag_strip(q_refs, i, a, mp)
                smax[mp] = strip_max(strips[mp][a], a, smax[mp])
        diag_step(lambda mp, a: strips[mp][a], smax, next_diag_producer())

    @pl.when(i > 0)
    def _():
        diag_step(lambda mp, a: sd_ref[mp, rows[a], a * sb:], [cd_ref[0], cd_ref[1]],
                  past_producer(0, sa_ref, ca_ref))

    def pair_body(p, carry):
        past_step(2 * p, sa_ref, ca_ref, past_producer(2 * p + 1, sb_ref, cb_ref))
        past_step(2 * p + 1, sb_ref, cb_ref, past_producer(2 * p + 2, sa_ref, ca_ref))
        return carry

    lax.fori_loop(0, jnp.maximum(i - 1, 0) // 2, pair_body, 0)

    @pl.when(jnp.logical_and(i >= 2, i % 2 == 0))
    def _():
        past_step(i - 2, sa_ref, ca_ref, past_producer(i - 1, sb_ref, cb_ref))
        past_step(i - 1, sb_ref, cb_ref, next_diag_producer())

    @pl.when(i % 2 == 1)
    def _():
        past_step(i - 1, sa_ref, ca_ref, next_diag_producer())

    lam = lam_ref[...]
    lam_full = (jnp.exp(jnp.sum(lam[0:1] * lam[1:2], axis=1, keepdims=True))
                - jnp.exp(jnp.sum(lam[2:3] * lam[3:4], axis=1, keepdims=True)) + lambda_init)
    inv_l = [1.0 / jnp.sum(l_ref[mp], axis=0, keepdims=True) for mp in range(2)]
    o = acc_ref[0] * inv_l[0] - lam_full * (acc_ref[1] * inv_l[1])
    y = o * lax.rsqrt(jnp.mean(o * o, axis=0, keepdims=True) + SUBLN_EPS)
    o_ref[...] = (y.T * (g_ref[...] * (1.0 - lambda_init))).astype(BF16)


def _diff_attention(lam, q1, q2, k4, vt4, dtiles, cfar, subln_g, lambda_init, layer, to_cast):
    bsz, nblk, t, _ = k4.shape
    seq = nblk * t
    hw = HEAD_WIDTH
    sb = dtiles.shape[-1]
    casts = _CastPlan(to_cast, bsz * DIFF_HEADS * nblk,
                      lambda b, h, i: (b * DIFF_HEADS + h) * nblk + i)
    kern = functools.partial(_attn_kernel, lambda_init=lambda_init, n_cast=len(casts))
    qspec = pl.BlockSpec((None, t, hw), lambda b, h, i: (b, i, h))
    qnext = pl.BlockSpec((None, t, hw), lambda b, h, i: (b, jnp.minimum(i + 1, nblk - 1), h))
    return pl.pallas_call(
        kern,
        grid=(bsz, DIFF_HEADS, nblk),
        in_specs=[
            _layer_spec(lam, layer // 2),
            qspec, qspec, qnext, qnext,
            pl.BlockSpec((None, nblk, t, hw), lambda b, h, i: (b, 0, 0, h)),
            pl.BlockSpec((None, nblk, hw, t), lambda b, h, i: (b, 0, h, 0)),
            pl.BlockSpec((None, 2, sb, sb), lambda b, h, i: (h, 0, 0, 0)),
            pl.BlockSpec((None, 1, t), lambda b, h, i: (h, 0, 0)),
            _layer_spec(subln_g, layer // 2),
        ] + casts.in_specs,
        out_specs=[qspec] + casts.out_specs,
        out_shape=[jax.ShapeDtypeStruct((bsz, seq, A_QK), BF16)] + casts.out_shapes,
        scratch_shapes=[pltpu.VMEM((2, t, t), F32)] * 3
        + [pltpu.VMEM((2, 1, t), F32)] * 3
        + [pltpu.VMEM((2, 1, t), F32),
           pltpu.VMEM((2, SUBLANES, t), F32),
           pltpu.VMEM((2, hw, t), F32)],
        compiler_params=pltpu.CompilerParams(
            dimension_semantics=("arbitrary", "arbitrary", "arbitrary"),
            vmem_limit_bytes=VMEM_LIMIT_BYTES),
        name="diff_attention",
    )(lam, q1, q2, q1, q2, k4, vt4, dtiles, cfar, subln_g, *casts.arrays)


def _row_parts(tm):
    return [slice(lo, lo + ROW_PART) for lo in range(0, tm, ROW_PART)]


def _residual_ffn(x_ref, mixes, parts, ng_ref, wg_ref, wu_ref, wd_ref, o_ref):
    x1s, h2s, acts = [], [], []
    for p, mix in zip(parts, mixes):
        x1 = x_ref[p, :] + _rms(mix, ng_ref[1:2, :])
        x1s.append(x1)
        h2s.append(_rms(x1, ng_ref[2:3, :]).astype(BF16))
    for h2 in h2s:
        gate = jnp.dot(h2, wg_ref[...], preferred_element_type=F32)
        up = jnp.dot(h2, wu_ref[...], preferred_element_type=F32)
        acts.append((gate * jax.nn.sigmoid(gate) * up).astype(BF16))
    for p, x1, act in zip(parts, x1s, acts):
        f = jnp.dot(act, wd_ref[...], preferred_element_type=F32)
        o_ref[p, :] = x1 + _rms(f, ng_ref[3:4, :])


def _even_out_kernel(x_ref, a_ref, bo_ref, wo_ref, ng_ref, wg_ref, wu_ref, wd_ref, *refs, n_cast):
    cast_in, o_ref, cast_out = refs[:n_cast], refs[n_cast], refs[n_cast + 1:]
    _cast_slabs(cast_in, cast_out)
    aw = a_ref.shape[1]
    parts = _row_parts(x_ref.shape[0])
    mixes = [jnp.dot(a_ref[p, :], wo_ref[0:aw, :], preferred_element_type=F32)
             + jnp.dot(bo_ref[p, :], wo_ref[aw:, :], preferred_element_type=F32) for p in parts]
    _residual_ffn(x_ref, mixes, parts, ng_ref, wg_ref, wu_ref, wd_ref, o_ref)


def _even_out(x2d, a2d, bo2d, wo_bf, ng, wg_bf, wu_bf, wd_bf, layer, to_cast):
    n, d = x2d.shape
    tm = EVEN_TILE
    tok = lambda w: pl.BlockSpec((tm, w), lambda i: (i, 0))
    casts = _CastPlan(to_cast, n // tm, lambda i: i)
    return pl.pallas_call(
        functools.partial(_even_out_kernel, n_cast=len(casts)),
        grid=(n // tm,),
        in_specs=[tok(d), tok(a2d.shape[1]), tok(bo2d.shape[1]), _const_spec(wo_bf.shape),
                  _layer_spec(ng, layer), _const_spec(wg_bf.shape), _const_spec(wu_bf.shape),
                  _const_spec(wd_bf.shape)] + casts.in_specs,
        out_specs=[tok(d)] + casts.out_specs,
        out_shape=[jax.ShapeDtypeStruct((n, d), F32)] + casts.out_shapes,
        compiler_params=pltpu.CompilerParams(dimension_semantics=("arbitrary",),
                                             vmem_limit_bytes=VMEM_LIMIT_BYTES),
        name="even_out_ffn",
    )(x2d, a2d, bo2d, wo_bf, ng, wg_bf, wu_bf, wd_bf, *casts.arrays)


def _odd_kernel(x_ref, wi_ref, lng_ref, lnb_ref, ws_ref, sb_ref, wo_ref, ng_ref,
                wg_ref, wu_ref, wd_ref, o_ref):
    sw = wo_ref.shape[0]
    gd = sw // SGU_GROUPS
    parts = _row_parts(x_ref.shape[0])
    nch = (parts[0].stop - parts[0].start) // CHUNK

    def gelu(z):
        return 0.5 * z * (1.0 + lax.erf(z * math.sqrt(0.5)))

    def gate_inputs(h):
        u = gelu(jnp.dot(h, wi_ref[:, 0:sw], preferred_element_type=F32))
        v = gelu(jnp.dot(h, wi_ref[:, sw:], preferred_element_type=F32))
        mu = jnp.mean(v, axis=-1, keepdims=True)
        vc = v - mu
        v = (vc * lax.rsqrt(jnp.mean(vc * vc, axis=-1, keepdims=True) + LN_EPS) * lng_ref[...]
             + lnb_ref[...]).astype(BF16)
        return u, v

    def spatial_gate(u, v):
        tiles = [[None] * SGU_GROUPS for _ in range(nch)]
        for g in range(SGU_GROUPS):
            rhs = jnp.concatenate([v[n * CHUNK:(n + 1) * CHUNK, g * gd:(g + 1) * gd] for n in range(nch)],
                                  axis=1)
            mixed = jnp.dot(ws_ref[g], rhs, preferred_element_type=F32) + sb_ref[g]
            for n in range(nch):
                tiles[n][g] = (u[n * CHUNK:(n + 1) * CHUNK, g * gd:(g + 1) * gd]
                               * mixed[:, n * gd:(n + 1) * gd])
        gated = jnp.concatenate([jnp.concatenate(r, axis=1) for r in tiles], axis=0).astype(BF16)
        return jnp.dot(gated, wo_ref[...], preferred_element_type=F32)

    hs = [_rms(x_ref[p, :], ng_ref[0:1, :]).astype(BF16) for p in parts]
    uvs = [gate_inputs(h) for h in hs]
    mixes = [spatial_gate(u, v) for u, v in uvs]
    _residual_ffn(x_ref, mixes, parts, ng_ref, wg_ref, wu_ref, wd_ref, o_ref)


def _odd_layer(x2d, wi_bf, ln_g, ln_b, ws_bf, sb, wo_bf, ng, wg_bf, wu_bf, wd_bf, layer):
    n, d = x2d.shape
    tm = ODD_TILE
    tok = pl.BlockSpec((tm, d), lambda i: (i, 0))
    consts = (wi_bf, ln_g, ln_b, ws_bf, sb, wo_bf, ng, wg_bf, wu_bf, wd_bf)
    whole = lambda a: _const_spec(a.shape)
    odd = lambda a: _layer_spec(a, layer // 2)
    return pl.pallas_call(
        _odd_kernel,
        grid=(n // tm,),
        in_specs=[tok, whole(wi_bf), odd(ln_g), odd(ln_b), odd(ws_bf), odd(sb), whole(wo_bf),
                  _layer_spec(ng, layer), whole(wg_bf), whole(wu_bf), whole(wd_bf)],
        out_specs=tok,
        out_shape=jax.ShapeDtypeStruct((n, d), F32),
        compiler_params=pltpu.CompilerParams(dimension_semantics=("arbitrary",),
                                             vmem_limit_bytes=VMEM_LIMIT_BYTES),
        name="odd_sgu_ffn",
    )(x2d, *consts)


def _t5_bucket_of_distance(n):
    max_exact = REL_BUCKETS // 2
    nf = jnp.maximum(n, 1).astype(F32)
    large = max_exact + (jnp.log(nf / max_exact) / math.log(REL_MAX_DIST / max_exact)
                         * (REL_BUCKETS - max_exact)).astype(jnp.int32)
    large = jnp.minimum(large, REL_BUCKETS - 1)
    return jnp.where(n < max_exact, n, large)


def _bias_tiles(rel_bias, sb, t):
    assert sb >= REL_MAX_DIST
    heads = rel_bias.shape[1]
    table = rel_bias.astype(F32).T
    span = 2 * sb
    hit = _t5_bucket_of_distance(jnp.arange(span, dtype=jnp.int32))[:, None] == jnp.arange(REL_BUCKETS)
    far = table[:, REL_BUCKETS - 1:]
    by_dist = (jnp.sum(jnp.where(hit[None], table[:, None, :], 0.0), axis=-1) - far) * LOG2_E

    def toeplitz(v):
        rows = jnp.broadcast_to(v[:, None, :], (heads, sb, span))
        skew = jnp.pad(rows, ((0, 0), (0, 0), (0, 1))).reshape(heads, sb * (span + 1))
        return skew[:, :sb * span].reshape(heads, sb, span)[:, :, sb:]

    prev = toeplitz(by_dist)
    diag = toeplitz(jnp.concatenate(
        [jnp.full((heads, sb), MASK_VALUE, F32), by_dist[:, :sb]], axis=1))
    tiles = jnp.stack([diag, prev], axis=1)
    cfar = jnp.broadcast_to((far * LOG2_E)[:, :, None], (heads, 1, t))
    return tiles, cfar


def kernel(x, rel_bias, w_in_even, diff_lambda, diff_subln_g, conv_w, w_out_even, w_in_odd,
           sgu_ln_g, sgu_ln_b, sgu_w, sgu_b, w_out_odd, norm_g, w_gate, w_up, w_down):
    bsz, seq, d = x.shape
    depth = norm_g.shape[0]
    t = ATTN_BLOCK
    assert seq % t == 0 and t % EVEN_TILE == 0 and t % BIAS_TILE == 0
    assert seq % ODD_TILE == 0 and ROW_PART % CHUNK == 0
    x2d = x.reshape(bsz * seq, d)
    dtiles, cfar = _bias_tiles(rel_bias, BIAS_TILE, t)
    tril = jnp.tril(jnp.ones((CHUNK, CHUNK), dtype=bool))
    w_in_even_bf = w_in_even.astype(BF16)
    ws = jnp.where(tril, sgu_w, 0.0).astype(BF16)
    ffn = lambda i: [(w_gate, i), (w_up, i), (w_down, i)]
    odd_bf = None
    for i in range(depth):
        if i % 2 == 0:
            lambda_init = 0.8 - 0.6 * math.exp(-0.3 * i)
            q1, q2, k, vt, bo, wo_bf, wg, wu, wd = _even_in(
                x2d, norm_g, w_in_even_bf, conv_w, seq, i, [(w_out_even, i // 2)] + ffn(i))
            nxt = [(w_in_odd, i // 2), (w_out_odd, i // 2)] + ffn(i + 1) if i + 1 < depth else []
            a, = _diff_attention(
                diff_lambda, q1.reshape(bsz, seq, A_QK), q2.reshape(bsz, seq, A_QK),
                k.reshape(bsz, seq // t, t, A_QK), vt.reshape(bsz, seq // t, A_QK, t),
                dtiles, cfar, diff_subln_g[:, None, :], lambda_init, i, [])
            x2d, *odd_bf = _even_out(x2d, a.reshape(bsz * seq, A_QK), bo, wo_bf, norm_g, wg, wu, wd, i, nxt)
        else:
            wi_bf, wo_bf, wg, wu, wd = odd_bf
            x2d = _odd_layer(x2d, wi_bf, sgu_ln_g[:, None, :], sgu_ln_b[:, None, :], ws,
                             sgu_b[..., None], wo_bf, norm_g, wg, wu, wd, i)
    return x2d.reshape(bsz, seq, d)
```

```python
import functools
import math

import jax
import jax.numpy as jnp
from jax import lax
from jax.experimental import pallas as pl
from jax.experimental.pallas import tpu as pltpu

F32 = jnp.float32
BF16 = jnp.bfloat16

DIFF_HEADS = 4
DIFF_QK_DIM = 64
HEAD_WIDTH = 2 * DIFF_QK_DIM
A_QK = DIFF_HEADS * HEAD_WIDTH
CONV_WIDTH = 3
SGU_GROUPS = 8
CHUNK = 128
REL_BUCKETS = 32
REL_MAX_DIST = 128
RMS_EPS = 1e-6
SUBLN_EPS = 1e-5
LN_EPS = 1e-5
MASK_VALUE = -1e30
LOG2_E = math.log2(math.e)

SUBLANES = 8
BF16_SUBLANES = 16
ROW_PART = 256
EVEN_TILE = 4 * ROW_PART
ODD_TILE = 2 * ROW_PART
ATTN_BLOCK = 1024
BIAS_TILE = 256
VMEM_LIMIT_BYTES = 56 * 1024 * 1024

_NT = (((1,), (1,)), ((), ()))


def _rms(x, g, eps=RMS_EPS):
    return x * lax.rsqrt(jnp.mean(x * x, axis=-1, keepdims=True) + eps) * g


def _const_spec(shape):
    return pl.BlockSpec(shape, lambda *_: (0,) * len(shape), pipeline_mode=pl.Buffered(1))


def _layer_spec(stacked, layer):
    rest = stacked.shape[1:]
    return pl.BlockSpec((None,) + rest, lambda *_: (layer,) + (0,) * len(rest),
                        pipeline_mode=pl.Buffered(1))


class _CastPlan:
    def __init__(self, weights, nsteps, step_of):
        self.arrays = [w for w, _ in weights]
        self.in_specs, self.out_specs, self.out_shapes = [], [], []
        for w, layer in weights:
            _, r, c = w.shape
            slab = next(s for s in range(BF16_SUBLANES, r + 1, BF16_SUBLANES)
                        if r % s == 0 and nsteps % (r // s) == 0)
            per_slab = nsteps // (r // slab)
            self.in_specs.append(pl.BlockSpec(
                (None, slab, c), lambda *g, layer=layer, per=per_slab: (layer, step_of(*g) // per, 0)))
            self.out_specs.append(pl.BlockSpec(
                (slab, c), lambda *g, per=per_slab: (step_of(*g) // per, 0)))
            self.out_shapes.append(jax.ShapeDtypeStruct((r, c), BF16))

    def __len__(self):
        return len(self.arrays)


def _cast_slabs(in_refs, out_refs):
    for src, dst in zip(in_refs, out_refs):
        dst[...] = src[...].astype(BF16)


def _even_in_kernel(x_ref, ng_ref, w_ref, cw_ref, *refs, tiles_per_seq, n_cast):
    cast_in, refs = refs[:n_cast], refs[n_cast:]
    q1_ref, q2_ref, k_ref, vt_ref, bo_ref = refs[:5]
    cast_out, (carry_ref,) = refs[5:5 + n_cast], refs[5 + n_cast:]
    _cast_slabs(cast_in, cast_out)
    tm = x_ref.shape[0]
    bw = bo_ref.shape[1]

    @pl.when(pl.program_id(0) % tiles_per_seq == 0)
    def _():
        carry_ref[...] = jnp.zeros_like(carry_ref)

    parts = _row_parts(tm)
    hs = [_rms(x_ref[p, :], ng_ref[0:1, :]).astype(BF16) for p in parts]
    c0 = 3 * A_QK
    b_parts, z_parts = [], []
    for p, h in zip(parts, hs):
        proj = jnp.dot(h, w_ref[...], preferred_element_type=F32)
        q = proj[:, 0:A_QK] * (DIFF_QK_DIM ** -0.5 * LOG2_E)
        first_map = (lax.broadcasted_iota(jnp.int32, q.shape, 1) % HEAD_WIDTH) < DIFF_QK_DIM
        q1_ref[p, :] = jnp.where(first_map, q, 0.0).astype(BF16)
        q2_ref[p, :] = jnp.where(first_map, 0.0, q).astype(BF16)
        k_ref[p, :] = proj[:, A_QK:2 * A_QK].astype(BF16)
        vt_ref[:, p] = proj[:, 2 * A_QK:c0].T.astype(BF16)
        b_parts.append(proj[:, c0:c0 + bw])
        z_parts.append(proj[:, c0 + bw:c0 + 2 * bw] * proj[:, c0 + 2 * bw:c0 + 3 * bw])
    b_gate = jnp.concatenate(b_parts, axis=0)
    z = jnp.concatenate(z_parts, axis=0)
    row = lax.broadcasted_iota(jnp.int32, z.shape, 0)
    prev = carry_ref[...]
    zm1 = jnp.where(row == 0, prev[SUBLANES - 1:SUBLANES], pltpu.roll(z, 1, 0))
    zm2 = jnp.where(row == 0, prev[SUBLANES - 2:SUBLANES - 1],
                    jnp.where(row == 1, prev[SUBLANES - 1:SUBLANES], pltpu.roll(z, 2, 0)))
    cw = cw_ref[...]
    y = cw[0:1] * zm2 + cw[1:2] * zm1 + cw[2:3] * z
    bo_ref[...] = (b_gate * y).astype(BF16)
    carry_ref[...] = z[tm - SUBLANES:, :]


def _even_in(x2d, norm_g, w_bf, conv_w, seq, layer, to_cast):
    n, d = x2d.shape
    tm = EVEN_TILE
    t = ATTN_BLOCK
    per_blk = t // tm
    bw = conv_w.shape[-1]
    casts = _CastPlan(to_cast, n // tm, lambda i: i)
    kern = functools.partial(_even_in_kernel, tiles_per_seq=seq // tm, n_cast=len(casts))
    tok = lambda w: pl.BlockSpec((tm, w), lambda i: (i, 0))
    return pl.pallas_call(
        kern,
        grid=(n // tm,),
        in_specs=[tok(d), _layer_spec(norm_g, layer), _layer_spec(w_bf, layer // 2),
                  _layer_spec(conv_w, layer // 2)] + casts.in_specs,
        out_specs=[tok(A_QK), tok(A_QK), tok(A_QK),
                   pl.BlockSpec((None, A_QK, tm), lambda i: (i // per_blk, 0, i % per_blk)),
                   tok(bw)] + casts.out_specs,
        out_shape=[jax.ShapeDtypeStruct((n, A_QK), BF16)] * 3
        + [jax.ShapeDtypeStruct((n // t, A_QK, t), BF16), jax.ShapeDtypeStruct((n, bw), BF16)]
        + casts.out_shapes,
        scratch_shapes=[pltpu.VMEM((SUBLANES, bw), F32)],
        compiler_params=pltpu.CompilerParams(dimension_semantics=("arbitrary",),
                                             vmem_limit_bytes=VMEM_LIMIT_BYTES),
        name="even_in_proj",
    )(x2d, norm_g, w_bf, conv_w, *casts.arrays)


_DIAG, _PREV = 0, 1


def _patched(s, tile_ref, patches):
    sb = tile_ref.shape[-1]
    for r, c, idx, scale in patches:
        band = s[r:r + sb, :]
        cols = [band[:, :c]] if c else []
        tile = tile_ref[idx] if scale is None else tile_ref[idx] * scale
        cols.append(band[:, c:c + sb] + tile)
        if c + sb < s.shape[1]:
            cols.append(band[:, c + sb:])
        rows = [s[:r]] if r else []
        rows.append(jnp.concatenate(cols, axis=1) if len(cols) > 1 else cols[0])
        if r + sb < s.shape[0]:
            rows.append(s[r + sb:])
        s = jnp.concatenate(rows, axis=0) if len(rows) > 1 else rows[0]
    return s


def _fold8(x, op):
    return op(x.reshape(x.shape[0] // SUBLANES, SUBLANES, x.shape[1]), axis=0)


def _lpad(x, width, value):
    if not width:
        return x
    return jnp.concatenate([jnp.full((x.shape[0], width), value, x.dtype), x], axis=1)


def _attn_kernel(lam_ref, q1_ref, q2_ref, q1n_ref, q2n_ref, k_ref, vt_ref, dt_ref, cfar_ref, g_ref,
                 *refs, lambda_init, n_cast):
    cast_in, o_ref, cast_out = refs[:n_cast], refs[n_cast], refs[n_cast + 1:2 * n_cast + 1]
    sa_ref, sb_ref, sd_ref, ca_ref, cb_ref, cd_ref, m_ref, l_ref, acc_ref = refs[2 * n_cast + 1:]
    _cast_slabs(cast_in, cast_out)
    i = pl.program_id(2)
    nblk = k_ref.shape[0]
    t = q1_ref.shape[0]
    sb = dt_ref.shape[-1]
    ns = t // sb
    q_refs = (q1_ref, q2_ref)
    qn_refs = (q1n_ref, q2n_ref)
    cfar = cfar_ref[...]
    rows = [slice(a * sb, (a + 1) * sb) for a in range(ns)]
    hw = vt_ref.shape[1]

    def value_rows(blk, a):
        return jnp.concatenate([vt_ref[blk, :, rows[a]], jnp.ones((BF16_SUBLANES, sb), BF16)], axis=0)

    def fill_piece(blk, a, mp, s_ref, cmax):
        s = lax.dot_general(k_ref[blk, rows[a], :], q_refs[mp][...], _NT, preferred_element_type=F32)
        if a == ns - 1:
            is_prev = jnp.where(blk == i - 1, 1.0, 0.0).astype(F32)
            s = _patched(s, dt_ref, [(0, 0, _PREV, is_prev)])
        s_ref[mp, rows[a], :] = s
        pm = _fold8(s, jnp.max)
        return pm if cmax is None else jnp.maximum(cmax, pm)

    def past_producer(blk, s_ref, c_ref):
        cmax = [None, None]

        def piece(a, mp):
            cmax[mp] = fill_piece(blk, a, mp, s_ref, cmax[mp])

        def finish():
            for mp in range(2):
                c_ref[mp] = jnp.max(cmax[mp], axis=0, keepdims=True)
        return piece, finish

    def diag_strip(qr, blk, a, mp):
        s = lax.dot_general(k_ref[blk, rows[a], :], qr[mp][a * sb:, :], _NT,
                            preferred_element_type=F32)
        patches = [(0, 0, _DIAG, None)] + ([(0, sb, _PREV, None)] if a + 1 < ns else [])
        return _patched(s, dt_ref, patches)

    def strip_max(s, a, smax):
        pm = _lpad(jnp.max(s, axis=0, keepdims=True), a * sb, MASK_VALUE)
        return pm if smax is None else jnp.maximum(smax, pm)

    def next_diag_producer():
        blk = jnp.minimum(i + 1, nblk - 1)
        smax = [None, None]

        def piece(a, mp):
            s = diag_strip(qn_refs, blk, a, mp)
            sd_ref[mp, rows[a], a * sb:] = s
            smax[mp] = strip_max(s, a, smax[mp])

        def finish():
            for mp in range(2):
                cd_ref[mp] = smax[mp]
        return piece, finish

    def past_step(blk, cur_s, cur_c, producer):
        piece, finish = producer
        shift, alpha = [], []
        for mp in range(2):
            m_old = m_ref[mp]
            m_new = jnp.maximum(m_old, cur_c[mp] + cfar)
            m_ref[mp] = m_new
            shift.append(m_new - cfar)
            alpha.append(jnp.exp2(m_old - m_new))
        pv = [None, None]
        for a in range(ns):
            for mp in range(2):
                e = jnp.exp2(cur_s[mp, rows[a], :] - shift[mp])
                piece(a, mp)
                d = jnp.dot(value_rows(blk, a), e.astype(BF16), preferred_element_type=F32)
                pv[mp] = d if pv[mp] is None else pv[mp] + d
        finish()
        for mp in range(2):
            acc_ref[mp] = alpha[mp] * acc_ref[mp] + pv[mp][:hw]
            l_ref[mp] = alpha[mp] * l_ref[mp] + pv[mp][hw:hw + SUBLANES] * (1.0 / SUBLANES)

    def diag_step(strip, smax, producer):
        piece, finish = producer
        lsum = [None, None]
        for a in range(ns):
            q_lo = a * sb
            for mp in range(2):
                e = jnp.exp2(strip(mp, a) - smax[mp][:, q_lo:])
                piece(a, mp)
                d = jnp.dot(value_rows(i, a), e.astype(BF16), preferred_element_type=F32)
                ps = _lpad(d[hw:hw + SUBLANES], q_lo, 0.0)
                lsum[mp] = ps if lsum[mp] is None else lsum[mp] + ps
                if a == 0:
                    acc_ref[mp] = d[:hw]
                else:
                    acc_ref[mp, :, q_lo:] += d[:hw]
        finish()
        for mp in range(2):
            m_ref[mp] = smax[mp] + cfar
            l_ref[mp] = lsum[mp] * (1.0 / SUBLANES)

    @pl.when(i == 0)
    def _():
        strips, smax = [[None] * ns, [None] * ns], [None, None]
        for a in range(ns):
            for mp in range(2):
                strips[mp][a] = diag_strip(q_refs, i, a, mp)
                smax[mp] = strip_max(strips[mp][a], a, smax[mp])
        diag_step(lambda mp, a: strips[mp][a], smax, next_diag_producer())

    @pl.when(i > 0)
    def _():
        diag_step(lambda mp, a: sd_ref[mp, rows[a], a * sb:], [cd_ref[0], cd_ref[1]],
                  past_producer(0, sa_ref, ca_ref))

    def pair_body(p, carry):
        past_step(2 * p, sa_ref, ca_ref, past_producer(2 * p + 1, sb_ref, cb_ref))
        past_step(2 * p + 1, sb_ref, cb_ref, past_producer(2 * p + 2, sa_ref, ca_ref))
        return carry

    lax.fori_loop(0, jnp.maximum(i - 1, 0) // 2, pair_body, 0)

    @pl.when(jnp.logical_and(i >= 2, i % 2 == 0))
    def _():
        past_step(i - 2, sa_ref, ca_ref, past_producer(i - 1, sb_ref, cb_ref))
        past_step(i - 1, sb_ref, cb_ref, next_diag_producer())

    @pl.when(i % 2 == 1)
    def _():
        past_step(i - 1, sa_ref, ca_ref, next_diag_producer())

    lam = lam_ref[...]
    lam_full = (jnp.exp(jnp.sum(lam[0:1] * lam[1:2], axis=1, keepdims=True))
                - jnp.exp(jnp.sum(lam[2:3] * lam[3:4], axis=1, keepdims=True)) + lambda_init)
    inv_l = [1.0 / jnp.sum(l_ref[mp], axis=0, keepdims=True) for mp in range(2)]
    o = acc_ref[0] * inv_l[0] - lam_full * (acc_ref[1] * inv_l[1])
    y = o * lax.rsqrt(jnp.mean(o * o, axis=0, keepdims=True) + SUBLN_EPS)
    o_ref[...] = (y * (g_ref[...] * (1.0 - lambda_init))).astype(BF16)


def _diff_attention(lam, q1, q2, k4, vt4, dtiles, cfar, subln_g, lambda_init, layer, to_cast):
    bsz, nblk, t, _ = k4.shape
    seq = nblk * t
    hw = HEAD_WIDTH
    sb = dtiles.shape[-1]
    casts = _CastPlan(to_cast, bsz * DIFF_HEADS * nblk,
                      lambda b, h, i: (b * DIFF_HEADS + h) * nblk + i)
    kern = functools.partial(_attn_kernel, lambda_init=lambda_init, n_cast=len(casts))
    qspec = pl.BlockSpec((None, t, hw), lambda b, h, i: (b, i, h))
    qnext = pl.BlockSpec((None, t, hw), lambda b, h, i: (b, jnp.minimum(i + 1, nblk - 1), h))
    return pl.pallas_call(
        kern,
        grid=(bsz, DIFF_HEADS, nblk),
        in_specs=[
            _layer_spec(lam, layer // 2),
            qspec, qspec, qnext, qnext,
            pl.BlockSpec((None, nblk, t, hw), lambda b, h, i: (b, 0, 0, h)),
            pl.BlockSpec((None, nblk, hw, t), lambda b, h, i: (b, 0, h, 0)),
            pl.BlockSpec((None, 2, sb, sb), lambda b, h, i: (h, 0, 0, 0)),
            pl.BlockSpec((None, 1, t), lambda b, h, i: (h, 0, 0)),
            _layer_spec(subln_g, layer // 2),
        ] + casts.in_specs,
        out_specs=[pl.BlockSpec((None, hw, t), lambda b, h, i: (b, h, i))] + casts.out_specs,
        out_shape=[jax.ShapeDtypeStruct((bsz, A_QK, seq), BF16)] + casts.out_shapes,
        scratch_shapes=[pltpu.VMEM((2, t, t), F32)] * 3
        + [pltpu.VMEM((2, 1, t), F32)] * 3
        + [pltpu.VMEM((2, 1, t), F32),
           pltpu.VMEM((2, SUBLANES, t), F32),
           pltpu.VMEM((2, hw, t), F32)],
        compiler_params=pltpu.CompilerParams(
            dimension_semantics=("arbitrary", "arbitrary", "arbitrary"),
            vmem_limit_bytes=VMEM_LIMIT_BYTES),
        name="diff_attention",
    )(lam, q1, q2, q1, q2, k4, vt4, dtiles, cfar, subln_g, *casts.arrays)


def _row_parts(tm):
    return [slice(lo, lo + ROW_PART) for lo in range(0, tm, ROW_PART)]


def _residual_ffn(x_ref, mixes, parts, ng_ref, wg_ref, wu_ref, wd_ref, o_ref):
    x1s, h2s, acts = [], [], []
    for p, mix in zip(parts, mixes):
        x1 = x_ref[p, :] + _rms(mix, ng_ref[1:2, :])
        x1s.append(x1)
        h2s.append(_rms(x1, ng_ref[2:3, :]).astype(BF16))
    for h2 in h2s:
        gate = jnp.dot(h2, wg_ref[...], preferred_element_type=F32)
        up = jnp.dot(h2, wu_ref[...], preferred_element_type=F32)
        acts.append((gate * jax.nn.sigmoid(gate) * up).astype(BF16))
    for p, x1, act in zip(parts, x1s, acts):
        f = jnp.dot(act, wd_ref[...], preferred_element_type=F32)
        o_ref[p, :] = x1 + _rms(f, ng_ref[3:4, :])


def _even_out_kernel(x_ref, a_ref, bo_ref, wo_ref, ng_ref, wg_ref, wu_ref, wd_ref, o_ref):
    aw = a_ref.shape[0]
    parts = _row_parts(x_ref.shape[0])
    mixes = [lax.dot_general(a_ref[:, p], wo_ref[0:aw, :], (((0,), (0,)), ((), ())),
                             preferred_element_type=F32)
             + jnp.dot(bo_ref[p, :], wo_ref[aw:, :], preferred_element_type=F32) for p in parts]
    _residual_ffn(x_ref, mixes, parts, ng_ref, wg_ref, wu_ref, wd_ref, o_ref)


def _even_out(x2d, a_t, bo2d, wo_bf, ng, wg_bf, wu_bf, wd_bf, layer):
    n, d = x2d.shape
    tm = EVEN_TILE
    per_seq = a_t.shape[2] // tm
    tok = lambda w: pl.BlockSpec((tm, w), lambda i: (i, 0))
    return pl.pallas_call(
        _even_out_kernel,
        grid=(n // tm,),
        in_specs=[tok(d), pl.BlockSpec((None, a_t.shape[1], tm), lambda i: (i // per_seq, 0, i % per_seq)),
                  tok(bo2d.shape[1]), _const_spec(wo_bf.shape),
                  _layer_spec(ng, layer), _const_spec(wg_bf.shape), _const_spec(wu_bf.shape),
                  _const_spec(wd_bf.shape)],
        out_specs=tok(d),
        out_shape=jax.ShapeDtypeStruct((n, d), F32),
        compiler_params=pltpu.CompilerParams(dimension_semantics=("arbitrary",),
                                             vmem_limit_bytes=VMEM_LIMIT_BYTES),
        name="even_out_ffn",
    )(x2d, a_t, bo2d, wo_bf, ng, wg_bf, wu_bf, wd_bf)


def _odd_kernel(x_ref, wi_ref, lng_ref, lnb_ref, ws_ref, sb_ref, wo_ref, ng_ref,
                wg_ref, wu_ref, wd_ref, o_ref):
    sw = wo_ref.shape[0]
    gd = sw // SGU_GROUPS
    parts = _row_parts(x_ref.shape[0])
    nch = (parts[0].stop - parts[0].start) // CHUNK

    def gelu(z):
        return 0.5 * z * (1.0 + lax.erf(z * math.sqrt(0.5)))

    def gate_inputs(h):
        u = gelu(jnp.dot(h, wi_ref[:, 0:sw], preferred_element_type=F32))
        v = gelu(jnp.dot(h, wi_ref[:, sw:], preferred_element_type=F32))
        mu = jnp.mean(v, axis=-1, keepdims=True)
        vc = v - mu
        v = (vc * lax.rsqrt(jnp.mean(vc * vc, axis=-1, keepdims=True) + LN_EPS) * lng_ref[...]
             + lnb_ref[...]).astype(BF16)
        return u, v

    def spatial_gate(u, v):
        tiles = [[None] * SGU_GROUPS for _ in range(nch)]
        for g in range(SGU_GROUPS):
            rhs = jnp.concatenate([v[n * CHUNK:(n + 1) * CHUNK, g * gd:(g + 1) * gd] for n in range(nch)],
                                  axis=1)
            mixed = jnp.dot(ws_ref[g], rhs, preferred_element_type=F32) + sb_ref[g]
            for n in range(nch):
                tiles[n][g] = (u[n * CHUNK:(n + 1) * CHUNK, g * gd:(g + 1) * gd]
                               * mixed[:, n * gd:(n + 1) * gd])
        gated = jnp.concatenate([jnp.concatenate(r, axis=1) for r in tiles], axis=0).astype(BF16)
        return jnp.dot(gated, wo_ref[...], preferred_element_type=F32)

    hs = [_rms(x_ref[p, :], ng_ref[0:1, :]).astype(BF16) for p in parts]
    uvs = [gate_inputs(h) for h in hs]
    mixes = [spatial_gate(u, v) for u, v in uvs]
    _residual_ffn(x_ref, mixes, parts, ng_ref, wg_ref, wu_ref, wd_ref, o_ref)


def _odd_layer(x2d, wi_bf, ln_g, ln_b, ws_bf, sb, wo_bf, ng, wg_bf, wu_bf, wd_bf, layer):
    n, d = x2d.shape
    tm = ODD_TILE
    tok = pl.BlockSpec((tm, d), lambda i: (i, 0))
    consts = (wi_bf, ln_g, ln_b, ws_bf, sb, wo_bf, ng, wg_bf, wu_bf, wd_bf)
    whole = lambda a: _const_spec(a.shape)
    odd = lambda a: _layer_spec(a, layer // 2)
    return pl.pallas_call(
        _odd_kernel,
        grid=(n // tm,),
        in_specs=[tok, whole(wi_bf), odd(ln_g), odd(ln_b), odd(ws_bf), odd(sb), whole(wo_bf),
                  _layer_spec(ng, layer), whole(wg_bf), whole(wu_bf), whole(wd_bf)],
        out_specs=tok,
        out_shape=jax.ShapeDtypeStruct((n, d), F32),
        compiler_params=pltpu.CompilerParams(dimension_semantics=("arbitrary",),
                                             vmem_limit_bytes=VMEM_LIMIT_BYTES),
        name="odd_sgu_ffn",
    )(x2d, *consts)


def _t5_bucket_of_distance(n):
    max_exact = REL_BUCKETS // 2
    nf = jnp.maximum(n, 1).astype(F32)
    large = max_exact + (jnp.log(nf / max_exact) / math.log(REL_MAX_DIST / max_exact)
                         * (REL_BUCKETS - max_exact)).astype(jnp.int32)
    large = jnp.minimum(large, REL_BUCKETS - 1)
    return jnp.where(n < max_exact, n, large)


def _bias_tiles(rel_bias, sb, t):
    assert sb >= REL_MAX_DIST
    heads = rel_bias.shape[1]
    table = rel_bias.astype(F32).T
    span = 2 * sb
    hit = _t5_bucket_of_distance(jnp.arange(span, dtype=jnp.int32))[:, None] == jnp.arange(REL_BUCKETS)
    far = table[:, REL_BUCKETS - 1:]
    by_dist = (jnp.sum(jnp.where(hit[None], table[:, None, :], 0.0), axis=-1) - far) * LOG2_E

    def toeplitz(v):
        rows = jnp.broadcast_to(v[:, None, :], (heads, sb, span))
        skew = jnp.pad(rows, ((0, 0), (0, 0), (0, 1))).reshape(heads, sb * (span + 1))
        return skew[:, :sb * span].reshape(heads, sb, span)[:, :, sb:]

    prev = toeplitz(by_dist)
    diag = toeplitz(jnp.concatenate(
        [jnp.full((heads, sb), MASK_VALUE, F32), by_dist[:, :sb]], axis=1))
    tiles = jnp.stack([diag, prev], axis=1)
    cfar = jnp.broadcast_to((far * LOG2_E)[:, :, None], (heads, 1, t))
    return tiles, cfar


def kernel(x, rel_bias, w_in_even, diff_lambda, diff_subln_g, conv_w, w_out_even, w_in_odd,
           sgu_ln_g, sgu_ln_b, sgu_w, sgu_b, w_out_odd, norm_g, w_gate, w_up, w_down):
    bsz, seq, d = x.shape
    depth = norm_g.shape[0]
    t = ATTN_BLOCK
    assert seq % t == 0 and t % EVEN_TILE == 0 and t % BIAS_TILE == 0
    assert seq % ODD_TILE == 0 and ROW_PART % CHUNK == 0
    x2d = x.reshape(bsz * seq, d)
    dtiles, cfar = _bias_tiles(rel_bias, BIAS_TILE, t)
    tril = jnp.tril(jnp.ones((CHUNK, CHUNK), dtype=bool))
    w_in_even_bf = w_in_even.astype(BF16)
    ws = jnp.where(tril, sgu_w, 0.0).astype(BF16)
    ffn = lambda i: [(w_gate, i), (w_up, i), (w_down, i)]
    odd_bf = None
    for i in range(depth):
        if i % 2 == 0:
            lambda_init = 0.8 - 0.6 * math.exp(-0.3 * i)
            q1, q2, k, vt, bo, wo_bf, wg, wu, wd = _even_in(
                x2d, norm_g, w_in_even_bf, conv_w, seq, i, [(w_out_even, i // 2)] + ffn(i))
            nxt = [(w_in_odd, i // 2), (w_out_odd, i // 2)] + ffn(i + 1) if i + 1 < depth else []
            a, *odd_bf = _diff_attention(
                diff_lambda, q1.reshape(bsz, seq, A_QK), q2.reshape(bsz, seq, A_QK),
                k.reshape(bsz, seq // t, t, A_QK), vt.reshape(bsz, seq // t, A_QK, t),
                dtiles, cfar, diff_subln_g[:, :, None], lambda_init, i, nxt)
            x2d = _even_out(x2d, a, bo, wo_bf, norm_g, wg, wu, wd, i)
        else:
            wi_bf, wo_bf, wg, wu, wd = odd_bf
            x2d = _odd_layer(x2d, wi_bf, sgu_ln_g[:, None, :], sgu_ln_b[:, None, :], ws,
                             sgu_b[..., None], wo_bf, norm_g, wg, wu, wd, i)
    return x2d.reshape(bsz, seq, d)
```

```python
import functools
import math

import jax
import jax.numpy as jnp
from jax import lax
from jax.experimental import pallas as pl
from jax.experimental.pallas import tpu as pltpu

F32 = jnp.float32
BF16 = jnp.bfloat16

DIFF_HEADS = 4
DIFF_QK_DIM = 64
HEAD_WIDTH = 2 * DIFF_QK_DIM
A_QK = DIFF_HEADS * HEAD_WIDTH
CONV_WIDTH = 3
SGU_GROUPS = 8
CHUNK = 128
REL_BUCKETS = 32
REL_MAX_DIST = 128
RMS_EPS = 1e-6
SUBLN_EPS = 1e-5
LN_EPS = 1e-5
MASK_VALUE = -1e30
LOG2_E = math.log2(math.e)

SUBLANES = 8
BF16_SUBLANES = 16
ROW_PART = 256
EVEN_TILE = 4 * ROW_PART
ODD_TILE = 2 * ROW_PART
ATTN_BLOCK = 1024
BIAS_TILE = 256
VMEM_LIMIT_BYTES = 56 * 1024 * 1024

_NT = (((1,), (1,)), ((), ()))


def _rms(x, g, eps=RMS_EPS):
    return x * lax.rsqrt(jnp.mean(x * x, axis=-1, keepdims=True) + eps) * g


def _const_spec(shape):
    return pl.BlockSpec(shape, lambda *_: (0,) * len(shape), pipeline_mode=pl.Buffered(1))


def _layer_spec(stacked, layer):
    rest = stacked.shape[1:]
    return pl.BlockSpec((None,) + rest, lambda *_: (layer,) + (0,) * len(rest),
                        pipeline_mode=pl.Buffered(1))


class _CastPlan:
    def __init__(self, weights, nsteps, step_of):
        self.arrays = [w for w, _ in weights]
        self.in_specs, self.out_specs, self.out_shapes = [], [], []
        for w, layer in weights:
            _, r, c = w.shape
            slab = next(s for s in range(BF16_SUBLANES, r + 1, BF16_SUBLANES)
                        if r % s == 0 and nsteps % (r // s) == 0)
            per_slab = nsteps // (r // slab)
            self.in_specs.append(pl.BlockSpec(
                (None, slab, c), lambda *g, layer=layer, per=per_slab: (layer, step_of(*g) // per, 0)))
            self.out_specs.append(pl.BlockSpec(
                (slab, c), lambda *g, per=per_slab: (step_of(*g) // per, 0)))
            self.out_shapes.append(jax.ShapeDtypeStruct((r, c), BF16))

    def __len__(self):
        return len(self.arrays)


def _cast_slabs(in_refs, out_refs):
    for src, dst in zip(in_refs, out_refs):
        dst[...] = src[...].astype(BF16)


def _even_in_kernel(x_ref, ng_ref, w_ref, cw_ref, *refs, tiles_per_seq, n_cast):
    cast_in, refs = refs[:n_cast], refs[n_cast:]
    q1_ref, q2_ref, k_ref, vt_ref, bo_ref = refs[:5]
    cast_out, (carry_ref, wbf_ref) = refs[5:5 + n_cast], refs[5 + n_cast:]
    _cast_slabs(cast_in, cast_out)
    tm = x_ref.shape[0]
    bw = bo_ref.shape[1]

    @pl.when(pl.program_id(0) == 0)
    def _():
        wbf_ref[...] = w_ref[...].astype(BF16)

    @pl.when(pl.program_id(0) % tiles_per_seq == 0)
    def _():
        carry_ref[...] = jnp.zeros_like(carry_ref)

    parts = _row_parts(tm)
    hs = [_rms(x_ref[p, :], ng_ref[0:1, :]).astype(BF16) for p in parts]
    c0 = 3 * A_QK
    b_parts, z_parts = [], []
    for p, h in zip(parts, hs):
        proj = jnp.dot(h, wbf_ref[...], preferred_element_type=F32)
        q = proj[:, 0:A_QK] * (DIFF_QK_DIM ** -0.5 * LOG2_E)
        first_map = (lax.broadcasted_iota(jnp.int32, q.shape, 1) % HEAD_WIDTH) < DIFF_QK_DIM
        q1_ref[p, :] = jnp.where(first_map, q, 0.0).astype(BF16)
        q2_ref[p, :] = jnp.where(first_map, 0.0, q).astype(BF16)
        k_ref[p, :] = proj[:, A_QK:2 * A_QK].astype(BF16)
        vt_ref[:, p] = proj[:, 2 * A_QK:c0].T.astype(BF16)
        b_parts.append(proj[:, c0:c0 + bw])
        z_parts.append(proj[:, c0 + bw:c0 + 2 * bw] * proj[:, c0 + 2 * bw:c0 + 3 * bw])
    b_gate = jnp.concatenate(b_parts, axis=0)
    z = jnp.concatenate(z_parts, axis=0)
    row = lax.broadcasted_iota(jnp.int32, z.shape, 0)
    prev = carry_ref[...]
    zm1 = jnp.where(row == 0, prev[SUBLANES - 1:SUBLANES], pltpu.roll(z, 1, 0))
    zm2 = jnp.where(row == 0, prev[SUBLANES - 2:SUBLANES - 1],
                    jnp.where(row == 1, prev[SUBLANES - 1:SUBLANES], pltpu.roll(z, 2, 0)))
    cw = cw_ref[...]
    y = cw[0:1] * zm2 + cw[1:2] * zm1 + cw[2:3] * z
    bo_ref[...] = (b_gate * y).astype(BF16)
    carry_ref[...] = z[tm - SUBLANES:, :]


def _even_in(x2d, norm_g, w_in, conv_w, seq, layer, to_cast):
    n, d = x2d.shape
    tm = EVEN_TILE
    t = ATTN_BLOCK
    per_blk = t // tm
    bw = conv_w.shape[-1]
    casts = _CastPlan(to_cast, n // tm, lambda i: i)
    kern = functools.partial(_even_in_kernel, tiles_per_seq=seq // tm, n_cast=len(casts))
    tok = lambda w: pl.BlockSpec((tm, w), lambda i: (i, 0))
    return pl.pallas_call(
        kern,
        grid=(n // tm,),
        in_specs=[tok(d), _layer_spec(norm_g, layer), _layer_spec(w_in, layer // 2),
                  _layer_spec(conv_w, layer // 2)] + casts.in_specs,
        out_specs=[tok(A_QK), tok(A_QK), tok(A_QK),
                   pl.BlockSpec((None, A_QK, tm), lambda i: (i // per_blk, 0, i % per_blk)),
                   tok(bw)] + casts.out_specs,
        out_shape=[jax.ShapeDtypeStruct((n, A_QK), BF16)] * 3
        + [jax.ShapeDtypeStruct((n // t, A_QK, t), BF16), jax.ShapeDtypeStruct((n, bw), BF16)]
        + casts.out_shapes,
        scratch_shapes=[pltpu.VMEM((SUBLANES, bw), F32), pltpu.VMEM(w_in.shape[1:], BF16)],
        compiler_params=pltpu.CompilerParams(dimension_semantics=("arbitrary",),
                                             vmem_limit_bytes=VMEM_LIMIT_BYTES),
        name="even_in_proj",
    )(x2d, norm_g, w_in, conv_w, *casts.arrays)


_DIAG, _PREV = 0, 1


def _patched(s, tile_ref, patches):
    sb = tile_ref.shape[-1]
    for r, c, idx, scale in patches:
        band = s[r:r + sb, :]
        cols = [band[:, :c]] if c else []
        tile = tile_ref[idx] if scale is None else tile_ref[idx] * scale
        cols.append(band[:, c:c + sb] + tile)
        if c + sb < s.shape[1]:
            cols.append(band[:, c + sb:])
        rows = [s[:r]] if r else []
        rows.append(jnp.concatenate(cols, axis=1) if len(cols) > 1 else cols[0])
        if r + sb < s.shape[0]:
            rows.append(s[r + sb:])
        s = jnp.concatenate(rows, axis=0) if len(rows) > 1 else rows[0]
    return s


def _fold8(x, op):
    return op(x.reshape(x.shape[0] // SUBLANES, SUBLANES, x.shape[1]), axis=0)


def _lpad(x, width, value):
    if not width:
        return x
    return jnp.concatenate([jnp.full((x.shape[0], width), value, x.dtype), x], axis=1)


def _attn_kernel(lam_ref, q1_ref, q2_ref, q1n_ref, q2n_ref, k_ref, vt_ref, dt_ref, cfar_ref, g_ref,
                 *refs, lambda_init, n_cast):
    cast_in, o_ref, cast_out = refs[:n_cast], refs[n_cast], refs[n_cast + 1:2 * n_cast + 1]
    sa_ref, sb_ref, sd_ref, ca_ref, cb_ref, cd_ref, m_ref, l_ref, acc_ref = refs[2 * n_cast + 1:]
    _cast_slabs(cast_in, cast_out)
    i = pl.program_id(2)
    nblk = k_ref.shape[0]
    t = q1_ref.shape[0]
    sb = dt_ref.shape[-1]
    ns = t // sb
    q_refs = (q1_ref, q2_ref)
    qn_refs = (q1n_ref, q2n_ref)
    cfar = cfar_ref[...]
    rows = [slice(a * sb, (a + 1) * sb) for a in range(ns)]
    hw = vt_ref.shape[1]

    def value_rows(blk, a):
        return jnp.concatenate([vt_ref[blk, :, rows[a]], jnp.ones((BF16_SUBLANES, sb), BF16)], axis=0)

    def fill_piece(blk, a, mp, s_ref, cmax):
        s = lax.dot_general(k_ref[blk, rows[a], :], q_refs[mp][...], _NT, preferred_element_type=F32)
        if a == ns - 1:
            is_prev = jnp.where(blk == i - 1, 1.0, 0.0).astype(F32)
            s = _patched(s, dt_ref, [(0, 0, _PREV, is_prev)])
        s_ref[mp, rows[a], :] = s
        pm = _fold8(s, jnp.max)
        return pm if cmax is None else jnp.maximum(cmax, pm)

    def past_producer(blk, s_ref, c_ref):
        cmax = [None, None]

        def piece(a, mp):
            cmax[mp] = fill_piece(blk, a, mp, s_ref, cmax[mp])

        def finish():
            for mp in range(2):
                c_ref[mp] = jnp.max(cmax[mp], axis=0, keepdims=True)
        return piece, finish

    def diag_strip(qr, blk, a, mp):
        s = lax.dot_general(k_ref[blk, rows[a], :], qr[mp][a * sb:, :], _NT,
                            preferred_element_type=F32)
        patches = [(0, 0, _DIAG, None)] + ([(0, sb, _PREV, None)] if a + 1 < ns else [])
        return _patched(s, dt_ref, patches)

    def strip_max(s, a, smax):
        pm = _lpad(jnp.max(s, axis=0, keepdims=True), a * sb, MASK_VALUE)
        return pm if smax is None else jnp.maximum(smax, pm)

    def next_diag_producer():
        blk = jnp.minimum(i + 1, nblk - 1)
        smax = [None, None]

        def piece(a, mp):
            s = diag_strip(qn_refs, blk, a, mp)
            sd_ref[mp, rows[a], a * sb:] = s
            smax[mp] = strip_max(s, a, smax[mp])

        def finish():
            for mp in range(2):
                cd_ref[mp] = smax[mp]
        return piece, finish

    def past_step(blk, cur_s, cur_c, producer):
        piece, finish = producer
        shift, alpha = [], []
        for mp in range(2):
            m_old = m_ref[mp]
            m_new = jnp.maximum(m_old, cur_c[mp] + cfar)
            m_ref[mp] = m_new
            shift.append(m_new - cfar)
            alpha.append(jnp.exp2(m_old - m_new))
        pv = [None, None]
        for a in range(ns):
            for mp in range(2):
                e = jnp.exp2(cur_s[mp, rows[a], :] - shift[mp])
                piece(a, mp)
                d = jnp.dot(value_rows(blk, a), e.astype(BF16), preferred_element_type=F32)
                pv[mp] = d if pv[mp] is None else pv[mp] + d
        finish()
        for mp in range(2):
            acc_ref[mp] = alpha[mp] * acc_ref[mp] + pv[mp][:hw]
            l_ref[mp] = alpha[mp] * l_ref[mp] + pv[mp][hw:hw + SUBLANES] * (1.0 / SUBLANES)

    def diag_step(strip, smax, producer):
        piece, finish = producer
        lsum = [None, None]
        for a in range(ns):
            q_lo = a * sb
            for mp in range(2):
                e = jnp.exp2(strip(mp, a) - smax[mp][:, q_lo:])
                piece(a, mp)
                d = jnp.dot(value_rows(i, a), e.astype(BF16), preferred_element_type=F32)
                ps = _lpad(d[hw:hw + SUBLANES], q_lo, 0.0)
                lsum[mp] = ps if lsum[mp] is None else lsum[mp] + ps
                if a == 0:
                    acc_ref[mp] = d[:hw]
                else:
                    acc_ref[mp, :, q_lo:] += d[:hw]
        finish()
        for mp in range(2):
            m_ref[mp] = smax[mp] + cfar
            l_ref[mp] = lsum[mp] * (1.0 / SUBLANES)

    @pl.when(i == 0)
    def _():
        strips, smax = [[None] * ns, [None] * ns], [None, None]
        for a in range(ns):
            for mp in range(2):
                strips[mp][a] = diag_strip(q_refs, i, a, mp)
                smax[mp] = strip_max(strips[mp][a], a, smax[mp])
        diag_step(lambda mp, a: strips[mp][a], smax, next_diag_producer())

    @pl.when(i > 0)
    def _():
        diag_step(lambda mp, a: sd_ref[mp, rows[a], a * sb:], [cd_ref[0], cd_ref[1]],
                  past_producer(0, sa_ref, ca_ref))

    def pair_body(p, carry):
        past_step(2 * p, sa_ref, ca_ref, past_producer(2 * p + 1, sb_ref, cb_ref))
        past_step(2 * p + 1, sb_ref, cb_ref, past_producer(2 * p + 2, sa_ref, ca_ref))
        return carry

    lax.fori_loop(0, jnp.maximum(i - 1, 0) // 2, pair_body, 0)

    @pl.when(jnp.logical_and(i >= 2, i % 2 == 0))
    def _():
        past_step(i - 2, sa_ref, ca_ref, past_producer(i - 1, sb_ref, cb_ref))
        past_step(i - 1, sb_ref, cb_ref, next_diag_producer())

    @pl.when(i % 2 == 1)
    def _():
        past_step(i - 1, sa_ref, ca_ref, next_diag_producer())

    lam = lam_ref[...]
    lam_full = (jnp.exp(jnp.sum(lam[0:1] * lam[1:2], axis=1, keepdims=True))
                - jnp.exp(jnp.sum(lam[2:3] * lam[3:4], axis=1, keepdims=True)) + lambda_init)
    inv_l = [1.0 / jnp.sum(l_ref[mp], axis=0, keepdims=True) for mp in range(2)]
    o = acc_ref[0] * inv_l[0] - lam_full * (acc_ref[1] * inv_l[1])
    y = o * lax.rsqrt(jnp.mean(o * o, axis=0, keepdims=True) + SUBLN_EPS)
    o_ref[...] = (y * (g_ref[...] * (1.0 - lambda_init))).astype(BF16)


def _diff_attention(lam, q1, q2, k4, vt4, dtiles, cfar, subln_g, lambda_init, layer, to_cast):
    bsz, nblk, t, _ = k4.shape
    seq = nblk * t
    hw = HEAD_WIDTH
    sb = dtiles.shape[-1]
    casts = _CastPlan(to_cast, bsz * DIFF_HEADS * nblk,
                      lambda b, h, i: (b * DIFF_HEADS + h) * nblk + i)
    kern = functools.partial(_attn_kernel, lambda_init=lambda_init, n_cast=len(casts))
    qspec = pl.BlockSpec((None, t, hw), lambda b, h, i: (b, i, h))
    qnext = pl.BlockSpec((None, t, hw), lambda b, h, i: (b, jnp.minimum(i + 1, nblk - 1), h))
    return pl.pallas_call(
        kern,
        grid=(bsz, DIFF_HEADS, nblk),
        in_specs=[
            _layer_spec(lam, layer // 2),
            qspec, qspec, qnext, qnext,
            pl.BlockSpec((None, nblk, t, hw), lambda b, h, i: (b, 0, 0, h)),
            pl.BlockSpec((None, nblk, hw, t), lambda b, h, i: (b, 0, h, 0)),
            pl.BlockSpec((None, 2, sb, sb), lambda b, h, i: (h, 0, 0, 0)),
            pl.BlockSpec((None, 1, t), lambda b, h, i: (h, 0, 0)),
            _layer_spec(subln_g, layer // 2),
        ] + casts.in_specs,
        out_specs=[pl.BlockSpec((None, hw, t), lambda b, h, i: (b, h, i))] + casts.out_specs,
        out_shape=[jax.ShapeDtypeStruct((bsz, A_QK, seq), BF16)] + casts.out_shapes,
        scratch_shapes=[pltpu.VMEM((2, t, t), F32)] * 3
        + [pltpu.VMEM((2, 1, t), F32)] * 3
        + [pltpu.VMEM((2, 1, t), F32),
           pltpu.VMEM((2, SUBLANES, t), F32),
           pltpu.VMEM((2, hw, t), F32)],
        compiler_params=pltpu.CompilerParams(
            dimension_semantics=("arbitrary", "arbitrary", "arbitrary"),
            vmem_limit_bytes=VMEM_LIMIT_BYTES),
        name="diff_attention",
    )(lam, q1, q2, q1, q2, k4, vt4, dtiles, cfar, subln_g, *casts.arrays)


def _row_parts(tm):
    return [slice(lo, lo + ROW_PART) for lo in range(0, tm, ROW_PART)]


def _residual_ffn(x_ref, mixes, parts, ng_ref, wg_ref, wu_ref, wd_ref, o_ref):
    x1s, h2s, acts = [], [], []
    for p, mix in zip(parts, mixes):
        x1 = x_ref[p, :] + _rms(mix, ng_ref[1:2, :])
        x1s.append(x1)
        h2s.append(_rms(x1, ng_ref[2:3, :]).astype(BF16))
    for h2 in h2s:
        gate = jnp.dot(h2, wg_ref[...], preferred_element_type=F32)
        up = jnp.dot(h2, wu_ref[...], preferred_element_type=F32)
        acts.append((gate * jax.nn.sigmoid(gate) * up).astype(BF16))
    for p, x1, act in zip(parts, x1s, acts):
        f = jnp.dot(act, wd_ref[...], preferred_element_type=F32)
        o_ref[p, :] = x1 + _rms(f, ng_ref[3:4, :])


def _even_out_kernel(x_ref, a_ref, bo_ref, wo_ref, ng_ref, wg_ref, wu_ref, wd_ref, o_ref):
    aw = a_ref.shape[0]
    parts = _row_parts(x_ref.shape[0])
    mixes = [lax.dot_general(a_ref[:, p], wo_ref[0:aw, :], (((0,), (0,)), ((), ())),
                             preferred_element_type=F32)
             + jnp.dot(bo_ref[p, :], wo_ref[aw:, :], preferred_element_type=F32) for p in parts]
    _residual_ffn(x_ref, mixes, parts, ng_ref, wg_ref, wu_ref, wd_ref, o_ref)


def _even_out(x2d, a_t, bo2d, wo_bf, ng, wg_bf, wu_bf, wd_bf, layer):
    n, d = x2d.shape
    tm = EVEN_TILE
    per_seq = a_t.shape[2] // tm
    tok = lambda w: pl.BlockSpec((tm, w), lambda i: (i, 0))
    return pl.pallas_call(
        _even_out_kernel,
        grid=(n // tm,),
        in_specs=[tok(d), pl.BlockSpec((None, a_t.shape[1], tm), lambda i: (i // per_seq, 0, i % per_seq)),
                  tok(bo2d.shape[1]), _const_spec(wo_bf.shape),
                  _layer_spec(ng, layer), _const_spec(wg_bf.shape), _const_spec(wu_bf.shape),
                  _const_spec(wd_bf.shape)],
        out_specs=tok(d),
        out_shape=jax.ShapeDtypeStruct((n, d), F32),
        compiler_params=pltpu.CompilerParams(dimension_semantics=("arbitrary",),
                                             vmem_limit_bytes=VMEM_LIMIT_BYTES),
        name="even_out_ffn",
    )(x2d, a_t, bo2d, wo_bf, ng, wg_bf, wu_bf, wd_bf)


def _odd_kernel(x_ref, wi_ref, lng_ref, lnb_ref, ws_ref, sb_ref, wo_ref, ng_ref,
                wg_ref, wu_ref, wd_ref, o_ref):
    sw = wo_ref.shape[0]
    gd = sw // SGU_GROUPS
    parts = _row_parts(x_ref.shape[0])
    nch = (parts[0].stop - parts[0].start) // CHUNK

    def gelu(z):
        return 0.5 * z * (1.0 + lax.erf(z * math.sqrt(0.5)))

    def gate_inputs(h):
        u = gelu(jnp.dot(h, wi_ref[:, 0:sw], preferred_element_type=F32))
        v = gelu(jnp.dot(h, wi_ref[:, sw:], preferred_element_type=F32))
        mu = jnp.mean(v, axis=-1, keepdims=True)
        vc = v - mu
        v = (vc * lax.rsqrt(jnp.mean(vc * vc, axis=-1, keepdims=True) + LN_EPS) * lng_ref[...]
             + lnb_ref[...]).astype(BF16)
        return u, v

    def spatial_gate(u, v):
        tiles = [[None] * SGU_GROUPS for _ in range(nch)]
        for g in range(SGU_GROUPS):
            rhs = jnp.concatenate([v[n * CHUNK:(n + 1) * CHUNK, g * gd:(g + 1) * gd] for n in range(nch)],
                                  axis=1)
            mixed = jnp.dot(ws_ref[g], rhs, preferred_element_type=F32) + sb_ref[g]
            for n in range(nch):
                tiles[n][g] = (u[n * CHUNK:(n + 1) * CHUNK, g * gd:(g + 1) * gd]
                               * mixed[:, n * gd:(n + 1) * gd])
        gated = jnp.concatenate([jnp.concatenate(r, axis=1) for r in tiles], axis=0).astype(BF16)
        return jnp.dot(gated, wo_ref[...], preferred_element_type=F32)

    hs = [_rms(x_ref[p, :], ng_ref[0:1, :]).astype(BF16) for p in parts]
    uvs = [gate_inputs(h) for h in hs]
    mixes = [spatial_gate(u, v) for u, v in uvs]
    _residual_ffn(x_ref, mixes, parts, ng_ref, wg_ref, wu_ref, wd_ref, o_ref)


def _odd_layer(x2d, wi_bf, ln_g, ln_b, ws_bf, sb, wo_bf, ng, wg_bf, wu_bf, wd_bf, layer):
    n, d = x2d.shape
    tm = ODD_TILE
    tok = pl.BlockSpec((tm, d), lambda i: (i, 0))
    consts = (wi_bf, ln_g, ln_b, ws_bf, sb, wo_bf, ng, wg_bf, wu_bf, wd_bf)
    whole = lambda a: _const_spec(a.shape)
    odd = lambda a: _layer_spec(a, layer // 2)
    return pl.pallas_call(
        _odd_kernel,
        grid=(n // tm,),
        in_specs=[tok, whole(wi_bf), odd(ln_g), odd(ln_b), odd(ws_bf), odd(sb), whole(wo_bf),
                  _layer_spec(ng, layer), whole(wg_bf), whole(wu_bf), whole(wd_bf)],
        out_specs=tok,
        out_shape=jax.ShapeDtypeStruct((n, d), F32),
        compiler_params=pltpu.CompilerParams(dimension_semantics=("arbitrary",),
                                             vmem_limit_bytes=VMEM_LIMIT_BYTES),
        name="odd_sgu_ffn",
    )(x2d, *consts)


def _t5_bucket_of_distance(n):
    max_exact = REL_BUCKETS // 2
    nf = jnp.maximum(n, 1).astype(F32)
    large = max_exact + (jnp.log(nf / max_exact) / math.log(REL_MAX_DIST / max_exact)
                         * (REL_BUCKETS - max_exact)).astype(jnp.int32)
    large = jnp.minimum(large, REL_BUCKETS - 1)
    return jnp.where(n < max_exact, n, large)


def _bias_tiles(rel_bias, sb, t):
    assert sb >= REL_MAX_DIST
    heads = rel_bias.shape[1]
    table = rel_bias.astype(F32).T
    span = 2 * sb
    hit = _t5_bucket_of_distance(jnp.arange(span, dtype=jnp.int32))[:, None] == jnp.arange(REL_BUCKETS)
    far = table[:, REL_BUCKETS - 1:]
    by_dist = (jnp.sum(jnp.where(hit[None], table[:, None, :], 0.0), axis=-1) - far) * LOG2_E

    def toeplitz(v):
        rows = jnp.broadcast_to(v[:, None, :], (heads, sb, span))
        skew = jnp.pad(rows, ((0, 0), (0, 0), (0, 1))).reshape(heads, sb * (span + 1))
        return skew[:, :sb * span].reshape(heads, sb, span)[:, :, sb:]

    prev = toeplitz(by_dist)
    diag = toeplitz(jnp.concatenate(
        [jnp.full((heads, sb), MASK_VALUE, F32), by_dist[:, :sb]], axis=1))
    tiles = jnp.stack([diag, prev], axis=1)
    cfar = jnp.broadcast_to((far * LOG2_E)[:, :, None], (heads, 1, t))
    return tiles, cfar


def kernel(x, rel_bias, w_in_even, diff_lambda, diff_subln_g, conv_w, w_out_even, w_in_odd,
           sgu_ln_g, sgu_ln_b, sgu_w, sgu_b, w_out_odd, norm_g, w_gate, w_up, w_down):
    bsz, seq, d = x.shape
    depth = norm_g.shape[0]
    t = ATTN_BLOCK
    assert seq % t == 0 and t % EVEN_TILE == 0 and t % BIAS_TILE == 0
    assert seq % ODD_TILE == 0 and ROW_PART % CHUNK == 0
    x2d = x.reshape(bsz * seq, d)
    dtiles, cfar = _bias_tiles(rel_bias, BIAS_TILE, t)
    tril = jnp.tril(jnp.ones((CHUNK, CHUNK), dtype=bool))
    ws = jnp.where(tril, sgu_w, 0.0).astype(BF16)
    ffn = lambda i: [(w_gate, i), (w_up, i), (w_down, i)]
    odd_bf = None
    for i in range(depth):
        if i % 2 == 0:
            lambda_init = 0.8 - 0.6 * math.exp(-0.3 * i)
            q1, q2, k, vt, bo, wo_bf, wg, wu, wd = _even_in(
                x2d, norm_g, w_in_even, conv_w, seq, i, [(w_out_even, i // 2)] + ffn(i))
            nxt = [(w_in_odd, i // 2), (w_out_odd, i // 2)] + ffn(i + 1) if i + 1 < depth else []
            a, *odd_bf = _diff_attention(
                diff_lambda, q1.reshape(bsz, seq, A_QK), q2.reshape(bsz, seq, A_QK),
                k.reshape(bsz, seq // t, t, A_QK), vt.reshape(bsz, seq // t, A_QK, t),
                dtiles, cfar, diff_subln_g[:, :, None], lambda_init, i, nxt)
            x2d = _even_out(x2d, a, bo, wo_bf, norm_g, wg, wu, wd, i)
        else:
            wi_bf, wo_bf, wg, wu, wd = odd_bf
            x2d = _odd_layer(x2d, wi_bf, sgu_ln_g[:, None, :], sgu_ln_b[:, None, :], ws,
                             sgu_b[..., None], wo_bf, norm_g, wg, wu, wd, i)
    return x2d.reshape(bsz, seq, d)
```

```python
import functools
import math

import jax
import jax.numpy as jnp
from jax import lax
from jax.experimental import pallas as pl
from jax.experimental.pallas import tpu as pltpu

F32 = jnp.float32
BF16 = jnp.bfloat16

DIFF_HEADS = 4
DIFF_QK_DIM = 64
HEAD_WIDTH = 2 * DIFF_QK_DIM
A_QK = DIFF_HEADS * HEAD_WIDTH
CONV_WIDTH = 3
SGU_GROUPS = 8
CHUNK = 128
REL_BUCKETS = 32
REL_MAX_DIST = 128
RMS_EPS = 1e-6
SUBLN_EPS = 1e-5
LN_EPS = 1e-5
MASK_VALUE = -1e30
LOG2_E = math.log2(math.e)

SUBLANES = 8
BF16_SUBLANES = 16
ROW_PART = 256
EVEN_TILE = 4 * ROW_PART
ODD_TILE = 2 * ROW_PART
ATTN_BLOCK = 1024
BIAS_TILE = 256
VMEM_LIMIT_BYTES = 56 * 1024 * 1024

_NT = (((1,), (1,)), ((), ()))


def _rms(x, g, eps=RMS_EPS):
    return x * lax.rsqrt(jnp.mean(x * x, axis=-1, keepdims=True) + eps) * g


def _const_spec(shape):
    return pl.BlockSpec(shape, lambda *_: (0,) * len(shape), pipeline_mode=pl.Buffered(1))


def _layer_spec(stacked, layer):
    rest = stacked.shape[1:]
    return pl.BlockSpec((None,) + rest, lambda *_: (layer,) + (0,) * len(rest),
                        pipeline_mode=pl.Buffered(1))


class _CastPlan:
    def __init__(self, weights, nsteps, step_of):
        self.arrays = [w for w, _ in weights]
        self.in_specs, self.out_specs, self.out_shapes = [], [], []
        for w, layer in weights:
            _, r, c = w.shape
            slab = next(s for s in range(BF16_SUBLANES, r + 1, BF16_SUBLANES)
                        if r % s == 0 and nsteps % (r // s) == 0)
            per_slab = nsteps // (r // slab)
            self.in_specs.append(pl.BlockSpec(
                (None, slab, c), lambda *g, layer=layer, per=per_slab: (layer, step_of(*g) // per, 0)))
            self.out_specs.append(pl.BlockSpec(
                (slab, c), lambda *g, per=per_slab: (step_of(*g) // per, 0)))
            self.out_shapes.append(jax.ShapeDtypeStruct((r, c), BF16))

    def __len__(self):
        return len(self.arrays)


def _cast_slabs(in_refs, out_refs):
    for src, dst in zip(in_refs, out_refs):
        dst[...] = src[...].astype(BF16)


def _even_in_kernel(x_ref, ng_ref, w_ref, cw_ref, *refs, tiles_per_seq, n_cast):
    cast_in, refs = refs[:n_cast], refs[n_cast:]
    q1_ref, q2_ref, k_ref, vt_ref, bo_ref = refs[:5]
    cast_out, (carry_ref, wbf_ref) = refs[5:5 + n_cast], refs[5 + n_cast:]
    _cast_slabs(cast_in, cast_out)
    tm = x_ref.shape[0]
    bw = bo_ref.shape[1]

    @pl.when(pl.program_id(0) == 0)
    def _():
        wbf_ref[...] = w_ref[...].astype(BF16)

    @pl.when(pl.program_id(0) % tiles_per_seq == 0)
    def _():
        carry_ref[...] = jnp.zeros_like(carry_ref)

    parts = _row_parts(tm)
    hs = [_rms(x_ref[p, :], ng_ref[0:1, :]).astype(BF16) for p in parts]
    c0 = 3 * A_QK
    b_parts, z_parts = [], []
    for p, h in zip(parts, hs):
        proj = jnp.dot(h, wbf_ref[...], preferred_element_type=F32)
        q = proj[:, 0:A_QK] * (DIFF_QK_DIM ** -0.5 * LOG2_E)
        first_map = (lax.broadcasted_iota(jnp.int32, q.shape, 1) % HEAD_WIDTH) < DIFF_QK_DIM
        q1_ref[p, :] = jnp.where(first_map, q, 0.0).astype(BF16)
        q2_ref[p, :] = jnp.where(first_map, 0.0, q).astype(BF16)
        k_ref[p, :] = proj[:, A_QK:2 * A_QK].astype(BF16)
        vt_ref[:, p] = proj[:, 2 * A_QK:c0].T.astype(BF16)
        b_parts.append(proj[:, c0:c0 + bw])
        z_parts.append(proj[:, c0 + bw:c0 + 2 * bw] * proj[:, c0 + 2 * bw:c0 + 3 * bw])
    b_gate = jnp.concatenate(b_parts, axis=0)
    z = jnp.concatenate(z_parts, axis=0)
    row = lax.broadcasted_iota(jnp.int32, z.shape, 0)
    prev = carry_ref[...]
    zm1 = jnp.where(row == 0, prev[SUBLANES - 1:SUBLANES], pltpu.roll(z, 1, 0))
    zm2 = jnp.where(row == 0, prev[SUBLANES - 2:SUBLANES - 1],
                    jnp.where(row == 1, prev[SUBLANES - 1:SUBLANES], pltpu.roll(z, 2, 0)))
    cw = cw_ref[...]
    y = cw[0:1] * zm2 + cw[1:2] * zm1 + cw[2:3] * z
    bo_ref[...] = (b_gate * y).astype(BF16)
    carry_ref[...] = z[tm - SUBLANES:, :]


def _even_in(x2d, norm_g, w_in, conv_w, seq, layer, to_cast):
    n, d = x2d.shape
    tm = EVEN_TILE
    t = ATTN_BLOCK
    per_blk = t // tm
    bw = conv_w.shape[-1]
    casts = _CastPlan(to_cast, n // tm, lambda i: i)
    kern = functools.partial(_even_in_kernel, tiles_per_seq=seq // tm, n_cast=len(casts))
    tok = lambda w: pl.BlockSpec((tm, w), lambda i: (i, 0))
    return pl.pallas_call(
        kern,
        grid=(n // tm,),
        in_specs=[tok(d), _layer_spec(norm_g, layer), _layer_spec(w_in, layer // 2),
                  _layer_spec(conv_w, layer // 2)] + casts.in_specs,
        out_specs=[tok(A_QK), tok(A_QK), tok(A_QK),
                   pl.BlockSpec((None, A_QK, tm), lambda i: (i // per_blk, 0, i % per_blk)),
                   tok(bw)] + casts.out_specs,
        out_shape=[jax.ShapeDtypeStruct((n, A_QK), BF16)] * 3
        + [jax.ShapeDtypeStruct((n // t, A_QK, t), BF16), jax.ShapeDtypeStruct((n, bw), BF16)]
        + casts.out_shapes,
        scratch_shapes=[pltpu.VMEM((SUBLANES, bw), F32), pltpu.VMEM(w_in.shape[1:], BF16)],
        compiler_params=pltpu.CompilerParams(dimension_semantics=("arbitrary",),
                                             vmem_limit_bytes=VMEM_LIMIT_BYTES),
        name="even_in_proj",
    )(x2d, norm_g, w_in, conv_w, *casts.arrays)


_DIAG, _PREV = 0, 1


def _patched(s, tile_ref, patches):
    sb = tile_ref.shape[0]
    for r, c, idx, scale in patches:
        band = s[r:r + sb, :]
        cols = [band[:, :c]] if c else []
        tile = tile_ref[:, idx * sb:(idx + 1) * sb]
        if scale is not None:
            tile = tile * scale
        cols.append(band[:, c:c + sb] + tile)
        if c + sb < s.shape[1]:
            cols.append(band[:, c + sb:])
        rows = [s[:r]] if r else []
        rows.append(jnp.concatenate(cols, axis=1) if len(cols) > 1 else cols[0])
        if r + sb < s.shape[0]:
            rows.append(s[r + sb:])
        s = jnp.concatenate(rows, axis=0) if len(rows) > 1 else rows[0]
    return s


def _fold8(x, op):
    return op(x.reshape(x.shape[0] // SUBLANES, SUBLANES, x.shape[1]), axis=0)


def _lpad(x, width, value):
    if not width:
        return x
    return jnp.concatenate([jnp.full((x.shape[0], width), value, x.dtype), x], axis=1)


def _attn_kernel(lam_ref, q1_ref, q2_ref, q1n_ref, q2n_ref, k_ref, vt_ref, dt_ref, cfar_ref, g_ref,
                 *refs, lambda_init, n_cast):
    cast_in, o_ref, cast_out = refs[:n_cast], refs[n_cast], refs[n_cast + 1:2 * n_cast + 1]
    sa_ref, sb_ref, sd_ref, ca_ref, cb_ref, cd_ref, m_ref, l_ref, acc_ref = refs[2 * n_cast + 1:]
    _cast_slabs(cast_in, cast_out)
    i = pl.program_id(2)
    nblk = k_ref.shape[0]
    t = q1_ref.shape[0]
    sb = dt_ref.shape[0]
    ns = t // sb
    q_refs = (q1_ref, q2_ref)
    qn_refs = (q1n_ref, q2n_ref)
    cfar = cfar_ref[...]
    rows = [slice(a * sb, (a + 1) * sb) for a in range(ns)]
    hw = vt_ref.shape[1]

    def value_rows(blk, a):
        return jnp.concatenate([vt_ref[blk, :, rows[a]], jnp.ones((BF16_SUBLANES, sb), BF16)], axis=0)

    def fill_piece(blk, a, mp, s_ref, cmax):
        s = lax.dot_general(k_ref[blk, rows[a], :], q_refs[mp][...], _NT, preferred_element_type=F32)
        if a == ns - 1:
            is_prev = jnp.where(blk == i - 1, 1.0, 0.0).astype(F32)
            s = _patched(s, dt_ref, [(0, 0, _PREV, is_prev)])
        s_ref[mp, rows[a], :] = s
        pm = _fold8(s, jnp.max)
        return pm if cmax is None else jnp.maximum(cmax, pm)

    def past_producer(blk, s_ref, c_ref):
        cmax = [None, None]

        def piece(a, mp):
            cmax[mp] = fill_piece(blk, a, mp, s_ref, cmax[mp])

        def finish():
            for mp in range(2):
                c_ref[mp] = jnp.max(cmax[mp], axis=0, keepdims=True)
        return piece, finish

    def diag_strip(qr, blk, a, mp):
        s = lax.dot_general(k_ref[blk, rows[a], :], qr[mp][a * sb:, :], _NT,
                            preferred_element_type=F32)
        patches = [(0, 0, _DIAG, None)] + ([(0, sb, _PREV, None)] if a + 1 < ns else [])
        return _patched(s, dt_ref, patches)

    def strip_max(s, a, smax):
        pm = _lpad(jnp.max(s, axis=0, keepdims=True), a * sb, MASK_VALUE)
        return pm if smax is None else jnp.maximum(smax, pm)

    def next_diag_producer():
        blk = jnp.minimum(i + 1, nblk - 1)
        smax = [None, None]

        def piece(a, mp):
            s = diag_strip(qn_refs, blk, a, mp)
            sd_ref[mp, rows[a], a * sb:] = s
            smax[mp] = strip_max(s, a, smax[mp])

        def finish():
            for mp in range(2):
                cd_ref[mp] = smax[mp]
        return piece, finish

    def past_step(blk, cur_s, cur_c, producer):
        piece, finish = producer
        shift, alpha = [], []
        for mp in range(2):
            m_old = m_ref[mp]
            m_new = jnp.maximum(m_old, cur_c[mp] + cfar)
            m_ref[mp] = m_new
            shift.append(m_new - cfar)
            alpha.append(jnp.exp2(m_old - m_new))
        pv = [None, None]
        for a in range(ns):
            for mp in range(2):
                e = jnp.exp2(cur_s[mp, rows[a], :] - shift[mp])
                piece(a, mp)
                d = jnp.dot(value_rows(blk, a), e.astype(BF16), preferred_element_type=F32)
                pv[mp] = d if pv[mp] is None else pv[mp] + d
        finish()
        for mp in range(2):
            acc_ref[mp] = alpha[mp] * acc_ref[mp] + pv[mp][:hw]
            l_ref[mp] = alpha[mp] * l_ref[mp] + pv[mp][hw:hw + SUBLANES] * (1.0 / SUBLANES)

    def diag_step(strip, smax, producer):
        piece, finish = producer
        lsum = [None, None]
        for a in range(ns):
            q_lo = a * sb
            for mp in range(2):
                e = jnp.exp2(strip(mp, a) - smax[mp][:, q_lo:])
                piece(a, mp)
                d = jnp.dot(value_rows(i, a), e.astype(BF16), preferred_element_type=F32)
                ps = _lpad(d[hw:hw + SUBLANES], q_lo, 0.0)
                lsum[mp] = ps if lsum[mp] is None else lsum[mp] + ps
                if a == 0:
                    acc_ref[mp] = d[:hw]
                else:
                    acc_ref[mp, :, q_lo:] += d[:hw]
        finish()
        for mp in range(2):
            m_ref[mp] = smax[mp] + cfar
            l_ref[mp] = lsum[mp] * (1.0 / SUBLANES)

    @pl.when(i == 0)
    def _():
        strips, smax = [[None] * ns, [None] * ns], [None, None]
        for a in range(ns):
            for mp in range(2):
                strips[mp][a] = diag_strip(q_refs, i, a, mp)
                smax[mp] = strip_max(strips[mp][a], a, smax[mp])
        diag_step(lambda mp, a: strips[mp][a], smax, next_diag_producer())

    @pl.when(i > 0)
    def _():
        diag_step(lambda mp, a: sd_ref[mp, rows[a], a * sb:], [cd_ref[0], cd_ref[1]],
                  past_producer(0, sa_ref, ca_ref))

    def pair_body(p, carry):
        past_step(2 * p, sa_ref, ca_ref, past_producer(2 * p + 1, sb_ref, cb_ref))
        past_step(2 * p + 1, sb_ref, cb_ref, past_producer(2 * p + 2, sa_ref, ca_ref))
        return carry

    lax.fori_loop(0, jnp.maximum(i - 1, 0) // 2, pair_body, 0)

    @pl.when(jnp.logical_and(i >= 2, i % 2 == 0))
    def _():
        past_step(i - 2, sa_ref, ca_ref, past_producer(i - 1, sb_ref, cb_ref))
        past_step(i - 1, sb_ref, cb_ref, next_diag_producer())

    @pl.when(i % 2 == 1)
    def _():
        past_step(i - 1, sa_ref, ca_ref, next_diag_producer())

    lam = lam_ref[...]
    lam_full = (jnp.exp(jnp.sum(lam[0:1] * lam[1:2], axis=1, keepdims=True))
                - jnp.exp(jnp.sum(lam[2:3] * lam[3:4], axis=1, keepdims=True)) + lambda_init)
    inv_l = [1.0 / jnp.sum(l_ref[mp], axis=0, keepdims=True) for mp in range(2)]
    o = acc_ref[0] * inv_l[0] - lam_full * (acc_ref[1] * inv_l[1])
    y = o * lax.rsqrt(jnp.mean(o * o, axis=0, keepdims=True) + SUBLN_EPS)
    o_ref[...] = (y * (g_ref[...] * (1.0 - lambda_init))).astype(BF16)


def _diff_attention(lam, q1, q2, k4, vt4, dtiles, cfar, subln_g, lambda_init, layer, to_cast):
    bsz, nblk, t, _ = k4.shape
    seq = nblk * t
    hw = HEAD_WIDTH
    sb = dtiles.shape[1]
    casts = _CastPlan(to_cast, bsz * DIFF_HEADS * nblk,
                      lambda b, h, i: (b * DIFF_HEADS + h) * nblk + i)
    kern = functools.partial(_attn_kernel, lambda_init=lambda_init, n_cast=len(casts))
    qspec = pl.BlockSpec((None, t, hw), lambda b, h, i: (b, i, h))
    qnext = pl.BlockSpec((None, t, hw), lambda b, h, i: (b, jnp.minimum(i + 1, nblk - 1), h))
    return pl.pallas_call(
        kern,
        grid=(bsz, DIFF_HEADS, nblk),
        in_specs=[
            _layer_spec(lam, layer // 2),
            qspec, qspec, qnext, qnext,
            pl.BlockSpec((None, nblk, t, hw), lambda b, h, i: (b, 0, 0, h)),
            pl.BlockSpec((None, nblk, hw, t), lambda b, h, i: (b, 0, h, 0)),
            pl.BlockSpec((None, sb, 2 * sb), lambda b, h, i: (h, 0, 0)),
            pl.BlockSpec((None, 1, t), lambda b, h, i: (h, 0, 0)),
            _layer_spec(subln_g, layer // 2),
        ] + casts.in_specs,
        out_specs=[pl.BlockSpec((None, hw, t), lambda b, h, i: (b, h, i))] + casts.out_specs,
        out_shape=[jax.ShapeDtypeStruct((bsz, A_QK, seq), BF16)] + casts.out_shapes,
        scratch_shapes=[pltpu.VMEM((2, t, t), F32)] * 3
        + [pltpu.VMEM((2, 1, t), F32)] * 3
        + [pltpu.VMEM((2, 1, t), F32),
           pltpu.VMEM((2, SUBLANES, t), F32),
           pltpu.VMEM((2, hw, t), F32)],
        compiler_params=pltpu.CompilerParams(
            dimension_semantics=("arbitrary", "arbitrary", "arbitrary"),
            vmem_limit_bytes=VMEM_LIMIT_BYTES),
        name="diff_attention",
    )(lam, q1, q2, q1, q2, k4, vt4, dtiles, cfar, subln_g, *casts.arrays)


def _row_parts(tm):
    return [slice(lo, lo + ROW_PART) for lo in range(0, tm, ROW_PART)]


def _residual_ffn(x_ref, mixes, parts, ng_ref, wg_ref, wu_ref, wd_ref, o_ref):
    x1s, h2s, acts = [], [], []
    for p, mix in zip(parts, mixes):
        x1 = x_ref[p, :] + _rms(mix, ng_ref[1:2, :])
        x1s.append(x1)
        h2s.append(_rms(x1, ng_ref[2:3, :]).astype(BF16))
    for h2 in h2s:
        gate = jnp.dot(h2, wg_ref[...], preferred_element_type=F32)
        up = jnp.dot(h2, wu_ref[...], preferred_element_type=F32)
        acts.append((gate * jax.nn.sigmoid(gate) * up).astype(BF16))
    for p, x1, act in zip(parts, x1s, acts):
        f = jnp.dot(act, wd_ref[...], preferred_element_type=F32)
        o_ref[p, :] = x1 + _rms(f, ng_ref[3:4, :])


def _even_out_kernel(x_ref, a_ref, bo_ref, wo_ref, ng_ref, wg_ref, wu_ref, wd_ref, o_ref):
    aw = a_ref.shape[0]
    parts = _row_parts(x_ref.shape[0])
    mixes = [lax.dot_general(a_ref[:, p], wo_ref[0:aw, :], (((0,), (0,)), ((), ())),
                             preferred_element_type=F32)
             + jnp.dot(bo_ref[p, :], wo_ref[aw:, :], preferred_element_type=F32) for p in parts]
    _residual_ffn(x_ref, mixes, parts, ng_ref, wg_ref, wu_ref, wd_ref, o_ref)


def _even_out(x2d, a_t, bo2d, wo_bf, ng, wg_bf, wu_bf, wd_bf, layer):
    n, d = x2d.shape
    tm = EVEN_TILE
    per_seq = a_t.shape[2] // tm
    tok = lambda w: pl.BlockSpec((tm, w), lambda i: (i, 0))
    return pl.pallas_call(
        _even_out_kernel,
        grid=(n // tm,),
        in_specs=[tok(d), pl.BlockSpec((None, a_t.shape[1], tm), lambda i: (i // per_seq, 0, i % per_seq)),
                  tok(bo2d.shape[1]), _const_spec(wo_bf.shape),
                  _layer_spec(ng, layer), _const_spec(wg_bf.shape), _const_spec(wu_bf.shape),
                  _const_spec(wd_bf.shape)],
        out_specs=tok(d),
        out_shape=jax.ShapeDtypeStruct((n, d), F32),
        compiler_params=pltpu.CompilerParams(dimension_semantics=("arbitrary",),
                                             vmem_limit_bytes=VMEM_LIMIT_BYTES),
        name="even_out_ffn",
    )(x2d, a_t, bo2d, wo_bf, ng, wg_bf, wu_bf, wd_bf)


def _odd_kernel(x_ref, wi_ref, lng_ref, lnb_ref, ws_ref, sb_ref, wo_ref, ng_ref,
                wg_ref, wu_ref, wd_ref, o_ref):
    sw = wo_ref.shape[0]
    gd = sw // SGU_GROUPS
    parts = _row_parts(x_ref.shape[0])
    nch = (parts[0].stop - parts[0].start) // CHUNK

    def gelu(z):
        return 0.5 * z * (1.0 + lax.erf(z * math.sqrt(0.5)))

    def gate_inputs(h):
        u = gelu(jnp.dot(h, wi_ref[:, 0:sw], preferred_element_type=F32))
        v = gelu(jnp.dot(h, wi_ref[:, sw:], preferred_element_type=F32))
        mu = jnp.mean(v, axis=-1, keepdims=True)
        vc = v - mu
        v = (vc * lax.rsqrt(jnp.mean(vc * vc, axis=-1, keepdims=True) + LN_EPS) * lng_ref[...]
             + lnb_ref[...]).astype(BF16)
        return u, v

    def spatial_gate(u, v):
        tiles = [[None] * SGU_GROUPS for _ in range(nch)]
        for g in range(SGU_GROUPS):
            rhs = jnp.concatenate([v[n * CHUNK:(n + 1) * CHUNK, g * gd:(g + 1) * gd] for n in range(nch)],
                                  axis=1)
            mixed = jnp.dot(ws_ref[g], rhs, preferred_element_type=F32) + sb_ref[g]
            for n in range(nch):
                tiles[n][g] = (u[n * CHUNK:(n + 1) * CHUNK, g * gd:(g + 1) * gd]
                               * mixed[:, n * gd:(n + 1) * gd])
        gated = jnp.concatenate([jnp.concatenate(r, axis=1) for r in tiles], axis=0).astype(BF16)
        return jnp.dot(gated, wo_ref[...], preferred_element_type=F32)

    hs = [_rms(x_ref[p, :], ng_ref[0:1, :]).astype(BF16) for p in parts]
    uvs = [gate_inputs(h) for h in hs]
    mixes = [spatial_gate(u, v) for u, v in uvs]
    _residual_ffn(x_ref, mixes, parts, ng_ref, wg_ref, wu_ref, wd_ref, o_ref)


def _odd_layer(x2d, wi_bf, ln_g, ln_b, ws_bf, sb, wo_bf, ng, wg_bf, wu_bf, wd_bf, layer):
    n, d = x2d.shape
    tm = ODD_TILE
    tok = pl.BlockSpec((tm, d), lambda i: (i, 0))
    consts = (wi_bf, ln_g, ln_b, ws_bf, sb, wo_bf, ng, wg_bf, wu_bf, wd_bf)
    whole = lambda a: _const_spec(a.shape)
    odd = lambda a: _layer_spec(a, layer // 2)
    return pl.pallas_call(
        _odd_kernel,
        grid=(n // tm,),
        in_specs=[tok, whole(wi_bf), odd(ln_g), odd(ln_b), odd(ws_bf), odd(sb), whole(wo_bf),
                  _layer_spec(ng, layer), whole(wg_bf), whole(wu_bf), whole(wd_bf)],
        out_specs=tok,
        out_shape=jax.ShapeDtypeStruct((n, d), F32),
        compiler_params=pltpu.CompilerParams(dimension_semantics=("arbitrary",),
                                             vmem_limit_bytes=VMEM_LIMIT_BYTES),
        name="odd_sgu_ffn",
    )(x2d, *consts)


def _t5_bucket_of_distance(n):
    max_exact = REL_BUCKETS // 2
    nf = jnp.maximum(n, 1).astype(F32)
    large = max_exact + (jnp.log(nf / max_exact) / math.log(REL_MAX_DIST / max_exact)
                         * (REL_BUCKETS - max_exact)).astype(jnp.int32)
    large = jnp.minimum(large, REL_BUCKETS - 1)
    return jnp.where(n < max_exact, n, large)


def _bias_tiles(rel_bias, sb, t):
    assert sb >= REL_MAX_DIST
    heads = rel_bias.shape[1]
    table = rel_bias.astype(F32).T
    span = 2 * sb
    hit = _t5_bucket_of_distance(jnp.arange(span, dtype=jnp.int32))[:, None] == jnp.arange(REL_BUCKETS)
    far = table[:, REL_BUCKETS - 1:]
    by_dist = (jnp.sum(jnp.where(hit[None], table[:, None, :], 0.0), axis=-1) - far) * LOG2_E

    u = jnp.concatenate([jnp.full((heads, sb), MASK_VALUE, F32), by_dist], axis=1)
    width = 3 * sb
    rows = jnp.broadcast_to(u[:, None, :], (heads, sb, width))
    skew = jnp.pad(rows, ((0, 0), (0, 0), (0, 1))).reshape(heads, sb * (width + 1))
    tiles = skew[:, :sb * width].reshape(heads, sb, width)[:, :, sb:]
    cfar = jnp.broadcast_to((far * LOG2_E)[:, :, None], (heads, 1, t))
    return tiles, cfar


def kernel(x, rel_bias, w_in_even, diff_lambda, diff_subln_g, conv_w, w_out_even, w_in_odd,
           sgu_ln_g, sgu_ln_b, sgu_w, sgu_b, w_out_odd, norm_g, w_gate, w_up, w_down):
    bsz, seq, d = x.shape
    depth = norm_g.shape[0]
    t = ATTN_BLOCK
    assert seq % t == 0 and t % EVEN_TILE == 0 and t % BIAS_TILE == 0
    assert seq % ODD_TILE == 0 and ROW_PART % CHUNK == 0
    x2d = x.reshape(bsz * seq, d)
    dtiles, cfar = _bias_tiles(rel_bias, BIAS_TILE, t)
    tril = jnp.tril(jnp.ones((CHUNK, CHUNK), dtype=bool))
    ws = jnp.where(tril, sgu_w, 0.0).astype(BF16)
    ffn = lambda i: [(w_gate, i), (w_up, i), (w_down, i)]
    odd_bf = None
    for i in range(depth):
        if i % 2 == 0:
            lambda_init = 0.8 - 0.6 * math.exp(-0.3 * i)
            q1, q2, k, vt, bo, wo_bf, wg, wu, wd = _even_in(
                x2d, norm_g, w_in_even, conv_w, seq, i, [(w_out_even, i // 2)] + ffn(i))
            nxt = [(w_in_odd, i // 2), (w_out_odd, i // 2)] + ffn(i + 1) if i + 1 < depth else []
            a, *odd_bf = _diff_attention(
                diff_lambda, q1.reshape(bsz, seq, A_QK), q2.reshape(bsz, seq, A_QK),
                k.reshape(bsz, seq // t, t, A_QK), vt.reshape(bsz, seq // t, A_QK, t),
                dtiles, cfar, diff_subln_g[:, :, None], lambda_init, i, nxt)
            x2d = _even_out(x2d, a, bo, wo_bf, norm_g, wg, wu, wd, i)
        else:
            wi_bf, wo_bf, wg, wu, wd = odd_bf
            x2d = _odd_layer(x2d, wi_bf, sgu_ln_g[:, None, :], sgu_ln_b[:, None, :], ws,
                             sgu_b[..., None], wo_bf, norm_g, wg, wu, wd, i)
    return x2d.reshape(bsz, seq, d)
```

```python
import functools
import math

import jax
import jax.numpy as jnp
from jax import lax
from jax.experimental import pallas as pl
from jax.experimental.pallas import tpu as pltpu

F32 = jnp.float32
BF16 = jnp.bfloat16

DIFF_HEADS = 4
DIFF_QK_DIM = 64
HEAD_WIDTH = 2 * DIFF_QK_DIM
A_QK = DIFF_HEADS * HEAD_WIDTH
CONV_WIDTH = 3
SGU_GROUPS = 8
CHUNK = 128
REL_BUCKETS = 32
REL_MAX_DIST = 128
RMS_EPS = 1e-6
SUBLN_EPS = 1e-5
LN_EPS = 1e-5
MASK_VALUE = -1e30
LOG2_E = math.log2(math.e)

SUBLANES = 8
BF16_SUBLANES = 16
ROW_PART = 256
EVEN_TILE = 4 * ROW_PART
ODD_TILE = 2 * ROW_PART
ATTN_BLOCK = 1024
BIAS_TILE = 256
VMEM_LIMIT_BYTES = 56 * 1024 * 1024

_NT = (((1,), (1,)), ((), ()))


def _rms(x, g, eps=RMS_EPS):
    return x * lax.rsqrt(jnp.mean(x * x, axis=-1, keepdims=True) + eps) * g


def _const_spec(shape):
    return pl.BlockSpec(shape, lambda *_: (0,) * len(shape), pipeline_mode=pl.Buffered(1))


def _layer_spec(stacked, layer):
    rest = stacked.shape[1:]
    return pl.BlockSpec((None,) + rest, lambda *_: (layer,) + (0,) * len(rest),
                        pipeline_mode=pl.Buffered(1))


class _CastPlan:
    def __init__(self, weights, nsteps, step_of):
        self.arrays = [w for w, _ in weights]
        self.in_specs, self.out_specs, self.out_shapes = [], [], []
        for w, layer in weights:
            _, r, c = w.shape
            slab = next(s for s in range(BF16_SUBLANES, r + 1, BF16_SUBLANES)
                        if r % s == 0 and nsteps % (r // s) == 0)
            per_slab = nsteps // (r // slab)
            self.in_specs.append(pl.BlockSpec(
                (None, slab, c), lambda *g, layer=layer, per=per_slab: (layer, step_of(*g) // per, 0)))
            self.out_specs.append(pl.BlockSpec(
                (slab, c), lambda *g, per=per_slab: (step_of(*g) // per, 0)))
            self.out_shapes.append(jax.ShapeDtypeStruct((r, c), BF16))

    def __len__(self):
        return len(self.arrays)


def _cast_slabs(in_refs, out_refs):
    for src, dst in zip(in_refs, out_refs):
        dst[...] = src[...].astype(BF16)


def _even_in_kernel(x_ref, ng_ref, w_ref, cw_ref, *refs, tiles_per_seq, n_cast):
    cast_in, refs = refs[:n_cast], refs[n_cast:]
    q1_ref, q2_ref, k_ref, vt_ref, bo_ref = refs[:5]
    cast_out, (carry_ref, wbf_ref) = refs[5:5 + n_cast], refs[5 + n_cast:]
    _cast_slabs(cast_in, cast_out)
    tm = x_ref.shape[0]
    bw = bo_ref.shape[1]

    @pl.when(pl.program_id(0) == 0)
    def _():
        wbf_ref[...] = w_ref[...].astype(BF16)

    @pl.when(pl.program_id(0) % tiles_per_seq == 0)
    def _():
        carry_ref[...] = jnp.zeros_like(carry_ref)

    parts = _row_parts(tm)
    hs = [_rms(x_ref[p, :], ng_ref[0:1, :]).astype(BF16) for p in parts]
    c0 = 3 * A_QK
    b_parts, z_parts = [], []
    for p, h in zip(parts, hs):
        proj = jnp.dot(h, wbf_ref[...], preferred_element_type=F32)
        q = proj[:, 0:A_QK] * (DIFF_QK_DIM ** -0.5 * LOG2_E)
        first_map = (lax.broadcasted_iota(jnp.int32, q.shape, 1) % HEAD_WIDTH) < DIFF_QK_DIM
        q1_ref[p, :] = jnp.where(first_map, q, 0.0).astype(BF16)
        q2_ref[p, :] = jnp.where(first_map, 0.0, q).astype(BF16)
        k_ref[p, :] = proj[:, A_QK:2 * A_QK].astype(BF16)
        vt_ref[:, p] = proj[:, 2 * A_QK:c0].T.astype(BF16)
        b_parts.append(proj[:, c0:c0 + bw])
        z_parts.append(proj[:, c0 + bw:c0 + 2 * bw] * proj[:, c0 + 2 * bw:c0 + 3 * bw])
    b_gate = jnp.concatenate(b_parts, axis=0)
    z = jnp.concatenate(z_parts, axis=0)
    row = lax.broadcasted_iota(jnp.int32, z.shape, 0)
    prev = carry_ref[...]
    zm1 = jnp.where(row == 0, prev[SUBLANES - 1:SUBLANES], pltpu.roll(z, 1, 0))
    zm2 = jnp.where(row == 0, prev[SUBLANES - 2:SUBLANES - 1],
                    jnp.where(row == 1, prev[SUBLANES - 1:SUBLANES], pltpu.roll(z, 2, 0)))
    cw = cw_ref[...]
    y = cw[0:1] * zm2 + cw[1:2] * zm1 + cw[2:3] * z
    bo_ref[...] = (b_gate * y).astype(BF16)
    carry_ref[...] = z[tm - SUBLANES:, :]


def _even_in(x2d, norm_g, w_in, conv_w, seq, layer, to_cast):
    n, d = x2d.shape
    tm = EVEN_TILE
    t = ATTN_BLOCK
    per_blk = t // tm
    bw = conv_w.shape[-1]
    casts = _CastPlan(to_cast, n // tm, lambda i: i)
    kern = functools.partial(_even_in_kernel, tiles_per_seq=seq // tm, n_cast=len(casts))
    tok = lambda w: pl.BlockSpec((tm, w), lambda i: (i, 0))
    return pl.pallas_call(
        kern,
        grid=(n // tm,),
        in_specs=[tok(d), _layer_spec(norm_g, layer), _layer_spec(w_in, layer // 2),
                  _layer_spec(conv_w, layer // 2)] + casts.in_specs,
        out_specs=[tok(A_QK), tok(A_QK), tok(A_QK),
                   pl.BlockSpec((None, A_QK, tm), lambda i: (i // per_blk, 0, i % per_blk)),
                   tok(bw)] + casts.out_specs,
        out_shape=[jax.ShapeDtypeStruct((n, A_QK), BF16)] * 3
        + [jax.ShapeDtypeStruct((n // t, A_QK, t), BF16), jax.ShapeDtypeStruct((n, bw), BF16)]
        + casts.out_shapes,
        scratch_shapes=[pltpu.VMEM((SUBLANES, bw), F32), pltpu.VMEM(w_in.shape[1:], BF16)],
        compiler_params=pltpu.CompilerParams(dimension_semantics=("arbitrary",),
                                             vmem_limit_bytes=VMEM_LIMIT_BYTES),
        name="even_in_proj",
    )(x2d, norm_g, w_in, conv_w, *casts.arrays)


_DIAG, _PREV = 0, 1


def _patched(s, tile_ref, patches):
    sb = tile_ref.shape[-1]
    for r, c, idx, scale in patches:
        band = s[r:r + sb, :]
        cols = [band[:, :c]] if c else []
        tile = tile_ref[idx] if scale is None else tile_ref[idx] * scale
        cols.append(band[:, c:c + sb] + tile)
        if c + sb < s.shape[1]:
            cols.append(band[:, c + sb:])
        rows = [s[:r]] if r else []
        rows.append(jnp.concatenate(cols, axis=1) if len(cols) > 1 else cols[0])
        if r + sb < s.shape[0]:
            rows.append(s[r + sb:])
        s = jnp.concatenate(rows, axis=0) if len(rows) > 1 else rows[0]
    return s


def _fold8(x, op):
    return op(x.reshape(x.shape[0] // SUBLANES, SUBLANES, x.shape[1]), axis=0)


def _lpad(x, width, value):
    if not width:
        return x
    return jnp.concatenate([jnp.full((x.shape[0], width), value, x.dtype), x], axis=1)


def _attn_kernel(lam_ref, q1_ref, q2_ref, q1n_ref, q2n_ref, k_ref, vt_ref, dt_ref, cfar_ref, g_ref,
                 *refs, lambda_init, n_cast):
    cast_in, o_ref, cast_out = refs[:n_cast], refs[n_cast], refs[n_cast + 1:2 * n_cast + 1]
    sa_ref, sb_ref, sd_ref, ca_ref, cb_ref, cd_ref, m_ref, l_ref, acc_ref = refs[2 * n_cast + 1:]
    _cast_slabs(cast_in, cast_out)
    i = pl.program_id(2)
    nblk = k_ref.shape[0]
    t = q1_ref.shape[0]
    sb = dt_ref.shape[-1]
    ns = t // sb
    q_refs = (q1_ref, q2_ref)
    qn_refs = (q1n_ref, q2n_ref)
    cfar = cfar_ref[...]
    rows = [slice(a * sb, (a + 1) * sb) for a in range(ns)]
    hw = vt_ref.shape[1]

    def value_rows(blk, a):
        return jnp.concatenate([vt_ref[blk, :, rows[a]], jnp.ones((BF16_SUBLANES, sb), BF16)], axis=0)

    def fill_piece(blk, a, mp, s_ref, cmax):
        s = lax.dot_general(k_ref[blk, rows[a], :], q_refs[mp][...], _NT, preferred_element_type=F32)
        if a == ns - 1:
            is_prev = jnp.where(blk == i - 1, 1.0, 0.0).astype(F32)
            s = _patched(s, dt_ref, [(0, 0, _PREV, is_prev)])
        s_ref[mp, rows[a], :] = s
        pm = _fold8(s, jnp.max)
        return pm if cmax is None else jnp.maximum(cmax, pm)

    def past_producer(blk, s_ref, c_ref):
        cmax = [None, None]

        def piece(a, mp):
            cmax[mp] = fill_piece(blk, a, mp, s_ref, cmax[mp])

        def finish():
            for mp in range(2):
                c_ref[mp] = jnp.max(cmax[mp], axis=0, keepdims=True)
        return piece, finish

    def diag_strip(qr, blk, a, mp):
        s = lax.dot_general(k_ref[blk, rows[a], :], qr[mp][a * sb:, :], _NT,
                            preferred_element_type=F32)
        patches = [(0, 0, _DIAG, None)] + ([(0, sb, _PREV, None)] if a + 1 < ns else [])
        return _patched(s, dt_ref, patches)

    def strip_max(s, a, smax):
        pm = _lpad(jnp.max(s, axis=0, keepdims=True), a * sb, MASK_VALUE)
        return pm if smax is None else jnp.maximum(smax, pm)

    def next_diag_producer():
        blk = jnp.minimum(i + 1, nblk - 1)
        smax = [None, None]

        def piece(a, mp):
            s = diag_strip(qn_refs, blk, a, mp)
            sd_ref[mp, rows[a], a * sb:] = s
            smax[mp] = strip_max(s, a, smax[mp])

        def finish():
            for mp in range(2):
                cd_ref[mp] = smax[mp]
        return piece, finish

    def past_step(blk, cur_s, cur_c, producer):
        piece, finish = producer
        shift, alpha = [], []
        for mp in range(2):
            m_old = m_ref[mp]
            m_new = jnp.maximum(m_old, cur_c[mp] + cfar)
            m_ref[mp] = m_new
            shift.append(m_new - cfar)
            alpha.append(jnp.exp2(m_old - m_new))
        pv = [None, None]
        for a in range(ns):
            for mp in range(2):
                e = jnp.exp2(cur_s[mp, rows[a], :] - shift[mp])
                piece(a, mp)
                d = jnp.dot(value_rows(blk, a), e.astype(BF16), preferred_element_type=F32)
                pv[mp] = d if pv[mp] is None else pv[mp] + d
        finish()
        for mp in range(2):
            acc_ref[mp] = alpha[mp] * acc_ref[mp] + pv[mp][:hw]
            l_ref[mp] = alpha[mp] * l_ref[mp] + pv[mp][hw:hw + SUBLANES] * (1.0 / SUBLANES)

    def diag_step(strip, smax, producer):
        piece, finish = producer
        lsum = [None, None]
        for a in range(ns):
            q_lo = a * sb
            for mp in range(2):
                e = jnp.exp2(strip(mp, a) - smax[mp][:, q_lo:])
                piece(a, mp)
                d = jnp.dot(value_rows(i, a), e.astype(BF16), preferred_element_type=F32)
                ps = _lpad(d[hw:hw + SUBLANES], q_lo, 0.0)
                lsum[mp] = ps if lsum[mp] is None else lsum[mp] + ps
                if a == 0:
                    acc_ref[mp] = d[:hw]
                else:
                    acc_ref[mp, :, q_lo:] += d[:hw]
        finish()
        for mp in range(2):
            m_ref[mp] = smax[mp] + cfar
            l_ref[mp] = lsum[mp] * (1.0 / SUBLANES)

    @pl.when(i == 0)
    def _():
        strips, smax = [[None] * ns, [None] * ns], [None, None]
        for a in range(ns):
            for mp in range(2):
                strips[mp][a] = diag_strip(q_refs, i, a, mp)
                smax[mp] = strip_max(strips[mp][a], a, smax[mp])
        diag_step(lambda mp, a: strips[mp][a], smax, next_diag_producer())

    @pl.when(i > 0)
    def _():
        diag_step(lambda mp, a: sd_ref[mp, rows[a], a * sb:], [cd_ref[0], cd_ref[1]],
                  past_producer(0, sa_ref, ca_ref))

    def pair_body(p, carry):
        past_step(2 * p, sa_ref, ca_ref, past_producer(2 * p + 1, sb_ref, cb_ref))
        past_step(2 * p + 1, sb_ref, cb_ref, past_producer(2 * p + 2, sa_ref, ca_ref))
        return carry

    lax.fori_loop(0, jnp.maximum(i - 1, 0) // 2, pair_body, 0)

    @pl.when(jnp.logical_and(i >= 2, i % 2 == 0))
    def _():
        past_step(i - 2, sa_ref, ca_ref, past_producer(i - 1, sb_ref, cb_ref))
        past_step(i - 1, sb_ref, cb_ref, next_diag_producer())

    @pl.when(i % 2 == 1)
    def _():
        past_step(i - 1, sa_ref, ca_ref, next_diag_producer())

    lam = lam_ref[...]
    lam_full = (jnp.exp(jnp.sum(lam[0:1] * lam[1:2], axis=1, keepdims=True))
                - jnp.exp(jnp.sum(lam[2:3] * lam[3:4], axis=1, keepdims=True)) + lambda_init)
    inv_l = [1.0 / jnp.sum(l_ref[mp], axis=0, keepdims=True) for mp in range(2)]
    o = acc_ref[0] * inv_l[0] - lam_full * (acc_ref[1] * inv_l[1])
    y = o * lax.rsqrt(jnp.mean(o * o, axis=0, keepdims=True) + SUBLN_EPS)
    o_ref[...] = (y * (g_ref[...] * (1.0 - lambda_init))).astype(BF16)


def _diff_attention(lam, q1, q2, k4, vt4, dtiles, cfar, subln_g, lambda_init, layer, to_cast):
    bsz, nblk, t, _ = k4.shape
    seq = nblk * t
    hw = HEAD_WIDTH
    sb = dtiles.shape[-1]
    casts = _CastPlan(to_cast, bsz * DIFF_HEADS * nblk,
                      lambda b, h, i: (b * DIFF_HEADS + h) * nblk + i)
    kern = functools.partial(_attn_kernel, lambda_init=lambda_init, n_cast=len(casts))
    qspec = pl.BlockSpec((None, t, hw), lambda b, h, i: (b, i, h))
    qnext = pl.BlockSpec((None, t, hw), lambda b, h, i: (b, jnp.minimum(i + 1, nblk - 1), h))
    return pl.pallas_call(
        kern,
        grid=(bsz, DIFF_HEADS, nblk),
        in_specs=[
            _layer_spec(lam, layer // 2),
            qspec, qspec, qnext, qnext,
            pl.BlockSpec((None, nblk, t, hw), lambda b, h, i: (b, 0, 0, h)),
            pl.BlockSpec((None, nblk, hw, t), lambda b, h, i: (b, 0, h, 0)),
            pl.BlockSpec((None, 2, sb, sb), lambda b, h, i: (h, 0, 0, 0)),
            pl.BlockSpec((None, 1, t), lambda b, h, i: (h, 0, 0)),
            _layer_spec(subln_g, layer // 2),
        ] + casts.in_specs,
        out_specs=[pl.BlockSpec((None, hw, t), lambda b, h, i: (b, h, i))] + casts.out_specs,
        out_shape=[jax.ShapeDtypeStruct((bsz, A_QK, seq), BF16)] + casts.out_shapes,
        scratch_shapes=[pltpu.VMEM((2, t, t), F32)] * 3
        + [pltpu.VMEM((2, 1, t), F32)] * 3
        + [pltpu.VMEM((2, 1, t), F32),
           pltpu.VMEM((2, SUBLANES, t), F32),
           pltpu.VMEM((2, hw, t), F32)],
        compiler_params=pltpu.CompilerParams(
            dimension_semantics=("arbitrary", "arbitrary", "arbitrary"),
            vmem_limit_bytes=VMEM_LIMIT_BYTES),
        name="diff_attention",
    )(lam, q1, q2, q1, q2, k4, vt4, dtiles, cfar, subln_g, *casts.arrays)


def _row_parts(tm):
    return [slice(lo, lo + ROW_PART) for lo in range(0, tm, ROW_PART)]


def _residual_ffn(x_ref, mixes, parts, ng_ref, wg_ref, wu_ref, wd_ref, o_ref):
    x1s, h2s, acts = [], [], []
    for p, mix in zip(parts, mixes):
        x1 = x_ref[p, :] + _rms(mix, ng_ref[1:2, :])
        x1s.append(x1)
        h2s.append(_rms(x1, ng_ref[2:3, :]).astype(BF16))
    for h2 in h2s:
        gate = jnp.dot(h2, wg_ref[...], preferred_element_type=F32)
        up = jnp.dot(h2, wu_ref[...], preferred_element_type=F32)
        acts.append((gate * jax.nn.sigmoid(gate) * up).astype(BF16))
    for p, x1, act in zip(parts, x1s, acts):
        f = jnp.dot(act, wd_ref[...], preferred_element_type=F32)
        o_ref[p, :] = x1 + _rms(f, ng_ref[3:4, :])


def _even_out_kernel(x_ref, a_ref, bo_ref, wo_ref, ng_ref, wg_ref, wu_ref, wd_ref, o_ref):
    aw = a_ref.shape[0]
    parts = _row_parts(x_ref.shape[0])
    mixes = [lax.dot_general(a_ref[:, p], wo_ref[0:aw, :], (((0,), (0,)), ((), ())),
                             preferred_element_type=F32)
             + jnp.dot(bo_ref[p, :], wo_ref[aw:, :], preferred_element_type=F32) for p in parts]
    _residual_ffn(x_ref, mixes, parts, ng_ref, wg_ref, wu_ref, wd_ref, o_ref)


def _even_out(x2d, a_t, bo2d, wo_bf, ng, wg_bf, wu_bf, wd_bf, layer):
    n, d = x2d.shape
    tm = EVEN_TILE
    per_seq = a_t.shape[2] // tm
    tok = lambda w: pl.BlockSpec((tm, w), lambda i: (i, 0))
    return pl.pallas_call(
        _even_out_kernel,
        grid=(n // tm,),
        in_specs=[tok(d), pl.BlockSpec((None, a_t.shape[1], tm), lambda i: (i // per_seq, 0, i % per_seq)),
                  tok(bo2d.shape[1]), _const_spec(wo_bf.shape),
                  _layer_spec(ng, layer), _const_spec(wg_bf.shape), _const_spec(wu_bf.shape),
                  _const_spec(wd_bf.shape)],
        out_specs=tok(d),
        out_shape=jax.ShapeDtypeStruct((n, d), F32),
        compiler_params=pltpu.CompilerParams(dimension_semantics=("arbitrary",),
                                             vmem_limit_bytes=VMEM_LIMIT_BYTES),
        name="even_out_ffn",
    )(x2d, a_t, bo2d, wo_bf, ng, wg_bf, wu_bf, wd_bf)


def _odd_kernel(x_ref, wi_ref, lng_ref, lnb_ref, ws_ref, sb_ref, wo_ref, ng_ref,
                wg_ref, wu_ref, wd_ref, o_ref):
    sw = wo_ref.shape[0]
    gd = sw // SGU_GROUPS
    parts = _row_parts(x_ref.shape[0])
    nch = (parts[0].stop - parts[0].start) // CHUNK

    def gelu(z):
        return 0.5 * z * (1.0 + lax.erf(z * math.sqrt(0.5)))

    def gate_inputs(h):
        u = gelu(jnp.dot(h, wi_ref[:, 0:sw], preferred_element_type=F32))
        v = gelu(jnp.dot(h, wi_ref[:, sw:], preferred_element_type=F32))
        mu = jnp.mean(v, axis=-1, keepdims=True)
        vc = v - mu
        v = (vc * lax.rsqrt(jnp.mean(vc * vc, axis=-1, keepdims=True) + LN_EPS) * lng_ref[...]
             + lnb_ref[...]).astype(BF16)
        return u, v

    def spatial_gate(u, v):
        tiles = [[None] * SGU_GROUPS for _ in range(nch)]
        for g in range(SGU_GROUPS):
            rhs = jnp.concatenate([v[n * CHUNK:(n + 1) * CHUNK, g * gd:(g + 1) * gd] for n in range(nch)],
                                  axis=1)
            mixed = jnp.dot(ws_ref[g], rhs, preferred_element_type=F32) + sb_ref[g]
            for n in range(nch):
                tiles[n][g] = (u[n * CHUNK:(n + 1) * CHUNK, g * gd:(g + 1) * gd]
                               * mixed[:, n * gd:(n + 1) * gd])
        gated = jnp.concatenate([jnp.concatenate(r, axis=1) for r in tiles], axis=0).astype(BF16)
        return jnp.dot(gated, wo_ref[...], preferred_element_type=F32)

    hs = [_rms(x_ref[p, :], ng_ref[0:1, :]).astype(BF16) for p in parts]
    uvs = [gate_inputs(h) for h in hs]
    mixes = [spatial_gate(u, v) for u, v in uvs]
    _residual_ffn(x_ref, mixes, parts, ng_ref, wg_ref, wu_ref, wd_ref, o_ref)


def _odd_layer(x2d, wi_bf, ln_g, ln_b, ws_bf, sb, wo_bf, ng, wg_bf, wu_bf, wd_bf, layer):
    n, d = x2d.shape
    tm = ODD_TILE
    tok = pl.BlockSpec((tm, d), lambda i: (i, 0))
    consts = (wi_bf, ln_g, ln_b, ws_bf, sb, wo_bf, ng, wg_bf, wu_bf, wd_bf)
    whole = lambda a: _const_spec(a.shape)
    odd = lambda a: _layer_spec(a, layer // 2)
    return pl.pallas_call(
        _odd_kernel,
        grid=(n // tm,),
        in_specs=[tok, whole(wi_bf), odd(ln_g), odd(ln_b), odd(ws_bf), odd(sb), whole(wo_bf),
                  _layer_spec(ng, layer), whole(wg_bf), whole(wu_bf), whole(wd_bf)],
        out_specs=tok,
        out_shape=jax.ShapeDtypeStruct((n, d), F32),
        compiler_params=pltpu.CompilerParams(dimension_semantics=("arbitrary",),
                                             vmem_limit_bytes=VMEM_LIMIT_BYTES),
        name="odd_sgu_ffn",
    )(x2d, *consts)


def _t5_bucket_of_distance(n):
    max_exact = REL_BUCKETS // 2
    nf = jnp.maximum(n, 1).astype(F32)
    large = max_exact + (jnp.log(nf / max_exact) / math.log(REL_MAX_DIST / max_exact)
                         * (REL_BUCKETS - max_exact)).astype(jnp.int32)
    large = jnp.minimum(large, REL_BUCKETS - 1)
    return jnp.where(n < max_exact, n, large)


def _bias_tiles(rel_bias, sb, t):
    assert sb >= REL_MAX_DIST
    heads = rel_bias.shape[1]
    table = rel_bias.astype(F32).T
    span = 2 * sb
    hit = _t5_bucket_of_distance(jnp.arange(span, dtype=jnp.int32))[:, None] == jnp.arange(REL_BUCKETS)
    far = table[:, REL_BUCKETS - 1:]
    by_dist = (jnp.sum(jnp.where(hit[None], table[:, None, :], 0.0), axis=-1) - far) * LOG2_E

    def toeplitz(v):
        rows = jnp.broadcast_to(v[:, None, :], (heads, sb, span))
        skew = jnp.pad(rows, ((0, 0), (0, 0), (0, 1))).reshape(heads, sb * (span + 1))
        return skew[:, :sb * span].reshape(heads, sb, span)[:, :, sb:]

    prev = toeplitz(by_dist)
    diag = toeplitz(jnp.concatenate(
        [jnp.full((heads, sb), MASK_VALUE, F32), by_dist[:, :sb]], axis=1))
    tiles = jnp.stack([diag, prev], axis=1)
    cfar = jnp.broadcast_to((far * LOG2_E)[:, :, None], (heads, 1, t))
    return tiles, cfar


def kernel(x, rel_bias, w_in_even, diff_lambda, diff_subln_g, conv_w, w_out_even, w_in_odd,
           sgu_ln_g, sgu_ln_b, sgu_w, sgu_b, w_out_odd, norm_g, w_gate, w_up, w_down):
    bsz, seq, d = x.shape
    depth = norm_g.shape[0]
    t = ATTN_BLOCK
    assert seq % t == 0 and t % EVEN_TILE == 0 and t % BIAS_TILE == 0
    assert seq % ODD_TILE == 0 and ROW_PART % CHUNK == 0
    assert conv_w.shape[1] == CONV_WIDTH
    x2d = x.reshape(bsz * seq, d)
    dtiles, cfar = _bias_tiles(rel_bias, BIAS_TILE, t)
    tril = jnp.tril(jnp.ones((CHUNK, CHUNK), dtype=bool))
    ws = jnp.where(tril, sgu_w, 0.0).astype(BF16)
    ffn = lambda i: [(w_gate, i), (w_up, i), (w_down, i)]
    odd_bf = None
    for i in range(depth):
        if i % 2 == 0:
            lambda_init = 0.8 - 0.6 * math.exp(-0.3 * i)
            q1, q2, k, vt, bo, wo_bf, wg, wu, wd = _even_in(
                x2d, norm_g, w_in_even, conv_w, seq, i, [(w_out_even, i // 2)] + ffn(i))
            nxt = [(w_in_odd, i // 2), (w_out_odd, i // 2)] + ffn(i + 1) if i + 1 < depth else []
            a, *odd_bf = _diff_attention(
                diff_lambda, q1.reshape(bsz, seq, A_QK), q2.reshape(bsz, seq, A_QK),
                k.reshape(bsz, seq // t, t, A_QK), vt.reshape(bsz, seq // t, A_QK, t),
                dtiles, cfar, diff_subln_g[:, :, None], lambda_init, i, nxt)
            x2d = _even_out(x2d, a, bo, wo_bf, norm_g, wg, wu, wd, i)
        else:
            wi_bf, wo_bf, wg, wu, wd = odd_bf
            x2d = _odd_layer(x2d, wi_bf, sgu_ln_g[:, None, :], sgu_ln_b[:, None, :], ws,
                             sgu_b[..., None], wo_bf, norm_g, wg, wu, wd, i)
    return x2d.reshape(bsz, seq, d)
```

```python
import functools
import math

import jax
import jax.numpy as jnp
from jax import lax
from jax.experimental import pallas as pl
from jax.experimental.pallas import tpu as pltpu

F32 = jnp.float32
BF16 = jnp.bfloat16

DIFF_HEADS = 4
DIFF_QK_DIM = 64
HEAD_WIDTH = 2 * DIFF_QK_DIM
A_QK = DIFF_HEADS * HEAD_WIDTH
CONV_WIDTH = 3
SGU_GROUPS = 8
CHUNK = 128
REL_BUCKETS = 32
REL_MAX_DIST = 128
RMS_EPS = 1e-6
SUBLN_EPS = 1e-5
LN_EPS = 1e-5
MASK_VALUE = -1e30
LOG2_E = math.log2(math.e)

SUBLANES = 8
BF16_SUBLANES = 16
ROW_PART = 256
EVEN_TILE = 4 * ROW_PART
ODD_TILE = 2 * ROW_PART
ATTN_BLOCK = 1024
BIAS_TILE = 256
VMEM_LIMIT_BYTES = 56 * 1024 * 1024

_NT = (((1,), (1,)), ((), ()))


def _rms(x, g, eps=RMS_EPS):
    return x * lax.rsqrt(jnp.mean(x * x, axis=-1, keepdims=True) + eps) * g


def _const_spec(shape):
    return pl.BlockSpec(shape, lambda *_: (0,) * len(shape), pipeline_mode=pl.Buffered(1))


def _layer_spec(stacked, layer):
    rest = stacked.shape[1:]
    return pl.BlockSpec((None,) + rest, lambda *_: (layer,) + (0,) * len(rest),
                        pipeline_mode=pl.Buffered(1))


class _CastPlan:
    def __init__(self, weights, nsteps, step_of):
        self.arrays = [w for w, _ in weights]
        self.in_specs, self.out_specs, self.out_shapes = [], [], []
        for w, layer in weights:
            _, r, c = w.shape
            slab = next(s for s in range(BF16_SUBLANES, r + 1, BF16_SUBLANES)
                        if r % s == 0 and nsteps % (r // s) == 0)
            per_slab = nsteps // (r // slab)
            self.in_specs.append(pl.BlockSpec(
                (None, slab, c), lambda *g, layer=layer, per=per_slab: (layer, step_of(*g) // per, 0)))
            self.out_specs.append(pl.BlockSpec(
                (slab, c), lambda *g, per=per_slab: (step_of(*g) // per, 0)))
            self.out_shapes.append(jax.ShapeDtypeStruct((r, c), BF16))

    def __len__(self):
        return len(self.arrays)


def _cast_slabs(in_refs, out_refs):
    for src, dst in zip(in_refs, out_refs):
        dst[...] = src[...].astype(BF16)


def _even_in_kernel(x_ref, ng_ref, w_ref, cw_ref, *refs, tiles_per_seq, n_cast):
    cast_in, refs = refs[:n_cast], refs[n_cast:]
    q1_ref, q2_ref, k_ref, vt_ref, bo_ref = refs[:5]
    cast_out, (carry_ref, wbf_ref) = refs[5:5 + n_cast], refs[5 + n_cast:]
    _cast_slabs(cast_in, cast_out)
    tm = x_ref.shape[0]
    bw = bo_ref.shape[1]

    @pl.when(pl.program_id(0) == 0)
    def _():
        wbf_ref[...] = w_ref[...].astype(BF16)

    @pl.when(pl.program_id(0) % tiles_per_seq == 0)
    def _():
        carry_ref[...] = jnp.zeros_like(carry_ref)

    parts = _row_parts(tm)
    hs = [_rms(x_ref[p, :], ng_ref[0:1, :]).astype(BF16) for p in parts]
    c0 = 3 * A_QK
    b_parts, z_parts = [], []
    for p, h in zip(parts, hs):
        proj = jnp.dot(h, wbf_ref[...], preferred_element_type=F32)
        q = proj[:, 0:A_QK] * (DIFF_QK_DIM ** -0.5 * LOG2_E)
        first_map = (lax.broadcasted_iota(jnp.int32, q.shape, 1) % HEAD_WIDTH) < DIFF_QK_DIM
        q1_ref[p, :] = jnp.where(first_map, q, 0.0).astype(BF16)
        q2_ref[p, :] = jnp.where(first_map, 0.0, q).astype(BF16)
        k_ref[p, :] = proj[:, A_QK:2 * A_QK].astype(BF16)
        vt_ref[:, p] = proj[:, 2 * A_QK:c0].T.astype(BF16)
        b_parts.append(proj[:, c0:c0 + bw])
        z_parts.append(proj[:, c0 + bw:c0 + 2 * bw] * proj[:, c0 + 2 * bw:c0 + 3 * bw])
    b_gate = jnp.concatenate(b_parts, axis=0)
    z = jnp.concatenate(z_parts, axis=0)
    row = lax.broadcasted_iota(jnp.int32, z.shape, 0)
    prev = carry_ref[...]
    zm1 = jnp.where(row == 0, prev[SUBLANES - 1:SUBLANES], pltpu.roll(z, 1, 0))
    zm2 = jnp.where(row == 0, prev[SUBLANES - 2:SUBLANES - 1],
                    jnp.where(row == 1, prev[SUBLANES - 1:SUBLANES], pltpu.roll(z, 2, 0)))
    cw = cw_ref[...]
    y = cw[0:1] * zm2 + cw[1:2] * zm1 + cw[2:3] * z
    bo_ref[...] = (b_gate * y).astype(BF16)
    carry_ref[...] = z[tm - SUBLANES:, :]


def _even_in(x2d, norm_g, w_in, conv_w, seq, layer, to_cast):
    n, d = x2d.shape
    tm = EVEN_TILE
    t = ATTN_BLOCK
    per_blk = t // tm
    bw = conv_w.shape[-1]
    casts = _CastPlan(to_cast, n // tm, lambda i: i)
    kern = functools.partial(_even_in_kernel, tiles_per_seq=seq // tm, n_cast=len(casts))
    tok = lambda w: pl.BlockSpec((tm, w), lambda i: (i, 0))
    return pl.pallas_call(
        kern,
        grid=(n // tm,),
        in_specs=[tok(d), _layer_spec(norm_g, layer), _layer_spec(w_in, layer // 2),
                  _layer_spec(conv_w, layer // 2)] + casts.in_specs,
        out_specs=[tok(A_QK), tok(A_QK), tok(A_QK),
                   pl.BlockSpec((None, A_QK, tm), lambda i: (i // per_blk, 0, i % per_blk)),
                   tok(bw)] + casts.out_specs,
        out_shape=[jax.ShapeDtypeStruct((n, A_QK), BF16)] * 3
        + [jax.ShapeDtypeStruct((n // t, A_QK, t), BF16), jax.ShapeDtypeStruct((n, bw), BF16)]
        + casts.out_shapes,
        scratch_shapes=[pltpu.VMEM((SUBLANES, bw), F32), pltpu.VMEM(w_in.shape[1:], BF16)],
        compiler_params=pltpu.CompilerParams(dimension_semantics=("arbitrary",),
                                             vmem_limit_bytes=VMEM_LIMIT_BYTES),
        name="even_in_proj",
    )(x2d, norm_g, w_in, conv_w, *casts.arrays)


_DIAG, _PREV = 0, 1


def _patched(s, tile_ref, patches):
    sb = tile_ref.shape[-1]
    for r, c, idx, scale in patches:
        band = s[r:r + sb, :]
        cols = [band[:, :c]] if c else []
        tile = tile_ref[idx] if scale is None else tile_ref[idx] * scale
        cols.append(band[:, c:c + sb] + tile)
        if c + sb < s.shape[1]:
            cols.append(band[:, c + sb:])
        rows = [s[:r]] if r else []
        rows.append(jnp.concatenate(cols, axis=1) if len(cols) > 1 else cols[0])
        if r + sb < s.shape[0]:
            rows.append(s[r + sb:])
        s = jnp.concatenate(rows, axis=0) if len(rows) > 1 else rows[0]
    return s


def _fold8(x, op):
    return op(x.reshape(x.shape[0] // SUBLANES, SUBLANES, x.shape[1]), axis=0)


def _lpad(x, width, value):
    if not width:
        return x
    return jnp.concatenate([jnp.full((x.shape[0], width), value, x.dtype), x], axis=1)


def _attn_kernel(lam_ref, q1_ref, q2_ref, q1n_ref, q2n_ref, k_ref, vt_ref, bd_ref, cfar_ref, g_ref,
                 *refs, lambda_init, n_cast):
    cast_in, o_ref, cast_out = refs[:n_cast], refs[n_cast], refs[n_cast + 1:2 * n_cast + 1]
    sa_ref, sb_ref, sd_ref, ca_ref, cb_ref, cd_ref, m_ref, l_ref, acc_ref, dt_ref = refs[2 * n_cast + 1:]
    _cast_slabs(cast_in, cast_out)
    i = pl.program_id(2)
    nblk = k_ref.shape[0]
    t = q1_ref.shape[0]
    sb = dt_ref.shape[-1]
    ns = t // sb

    @pl.when(i == 0)
    def _():
        def toeplitz(vec):
            return pltpu.roll(jnp.broadcast_to(vec, (sb, 2 * sb)), 0, 1, stride=1, stride_axis=0)[:, sb:]

        by_dist = bd_ref[...]
        dt_ref[_PREV] = toeplitz(by_dist)
        dt_ref[_DIAG] = toeplitz(jnp.concatenate(
            [jnp.full((1, sb), MASK_VALUE, F32), by_dist[:, :sb]], axis=1))
    q_refs = (q1_ref, q2_ref)
    qn_refs = (q1n_ref, q2n_ref)
    cfar = cfar_ref[...]
    rows = [slice(a * sb, (a + 1) * sb) for a in range(ns)]
    hw = vt_ref.shape[1]

    def value_rows(blk, a):
        return jnp.concatenate([vt_ref[blk, :, rows[a]], jnp.ones((BF16_SUBLANES, sb), BF16)], axis=0)

    def fill_piece(blk, a, mp, s_ref, cmax):
        s = lax.dot_general(k_ref[blk, rows[a], :], q_refs[mp][...], _NT, preferred_element_type=F32)
        if a == ns - 1:
            is_prev = jnp.where(blk == i - 1, 1.0, 0.0).astype(F32)
            s = _patched(s, dt_ref, [(0, 0, _PREV, is_prev)])
        s_ref[mp, rows[a], :] = s
        pm = _fold8(s, jnp.max)
        return pm if cmax is None else jnp.maximum(cmax, pm)

    def past_producer(blk, s_ref, c_ref):
        cmax = [None, None]

        def piece(a, mp):
            cmax[mp] = fill_piece(blk, a, mp, s_ref, cmax[mp])

        def finish():
            for mp in range(2):
                c_ref[mp] = jnp.max(cmax[mp], axis=0, keepdims=True)
        return piece, finish

    def diag_strip(qr, blk, a, mp):
        s = lax.dot_general(k_ref[blk, rows[a], :], qr[mp][a * sb:, :], _NT,
                            preferred_element_type=F32)
        patches = [(0, 0, _DIAG, None)] + ([(0, sb, _PREV, None)] if a + 1 < ns else [])
        return _patched(s, dt_ref, patches)

    def strip_max(s, a, smax):
        pm = _lpad(jnp.max(s, axis=0, keepdims=True), a * sb, MASK_VALUE)
        return pm if smax is None else jnp.maximum(smax, pm)

    def next_diag_producer():
        blk = jnp.minimum(i + 1, nblk - 1)
        smax = [None, None]

        def piece(a, mp):
            s = diag_strip(qn_refs, blk, a, mp)
            sd_ref[mp, rows[a], a * sb:] = s
            smax[mp] = strip_max(s, a, smax[mp])

        def finish():
            for mp in range(2):
                cd_ref[mp] = smax[mp]
        return piece, finish

    def past_step(blk, cur_s, cur_c, producer):
        piece, finish = producer
        shift, alpha = [], []
        for mp in range(2):
            m_old = m_ref[mp]
            m_new = jnp.maximum(m_old, cur_c[mp] + cfar)
            m_ref[mp] = m_new
            shift.append(m_new - cfar)
            alpha.append(jnp.exp2(m_old - m_new))
        pv = [None, None]
        for a in range(ns):
            for mp in range(2):
                e = jnp.exp2(cur_s[mp, rows[a], :] - shift[mp])
                piece(a, mp)
                d = jnp.dot(value_rows(blk, a), e.astype(BF16), preferred_element_type=F32)
                pv[mp] = d if pv[mp] is None else pv[mp] + d
        finish()
        for mp in range(2):
            acc_ref[mp] = alpha[mp] * acc_ref[mp] + pv[mp][:hw]
            l_ref[mp] = alpha[mp] * l_ref[mp] + pv[mp][hw:hw + SUBLANES] * (1.0 / SUBLANES)

    def diag_step(strip, smax, producer):
        piece, finish = producer
        lsum = [None, None]
        for a in range(ns):
            q_lo = a * sb
            for mp in range(2):
                e = jnp.exp2(strip(mp, a) - smax[mp][:, q_lo:])
                piece(a, mp)
                d = jnp.dot(value_rows(i, a), e.astype(BF16), preferred_element_type=F32)
                ps = _lpad(d[hw:hw + SUBLANES], q_lo, 0.0)
                lsum[mp] = ps if lsum[mp] is None else lsum[mp] + ps
                if a == 0:
                    acc_ref[mp] = d[:hw]
                else:
                    acc_ref[mp, :, q_lo:] += d[:hw]
        finish()
        for mp in range(2):
            m_ref[mp] = smax[mp] + cfar
            l_ref[mp] = lsum[mp] * (1.0 / SUBLANES)

    @pl.when(i == 0)
    def _():
        strips, smax = [[None] * ns, [None] * ns], [None, None]
        for a in range(ns):
            for mp in range(2):
                strips[mp][a] = diag_strip(q_refs, i, a, mp)
                smax[mp] = strip_max(strips[mp][a], a, smax[mp])
        diag_step(lambda mp, a: strips[mp][a], smax, next_diag_producer())

    @pl.when(i > 0)
    def _():
        diag_step(lambda mp, a: sd_ref[mp, rows[a], a * sb:], [cd_ref[0], cd_ref[1]],
                  past_producer(0, sa_ref, ca_ref))

    def pair_body(p, carry):
        past_step(2 * p, sa_ref, ca_ref, past_producer(2 * p + 1, sb_ref, cb_ref))
        past_step(2 * p + 1, sb_ref, cb_ref, past_producer(2 * p + 2, sa_ref, ca_ref))
        return carry

    lax.fori_loop(0, jnp.maximum(i - 1, 0) // 2, pair_body, 0)

    @pl.when(jnp.logical_and(i >= 2, i % 2 == 0))
    def _():
        past_step(i - 2, sa_ref, ca_ref, past_producer(i - 1, sb_ref, cb_ref))
        past_step(i - 1, sb_ref, cb_ref, next_diag_producer())

    @pl.when(i % 2 == 1)
    def _():
        past_step(i - 1, sa_ref, ca_ref, next_diag_producer())

    lam = lam_ref[...]
    lam_full = (jnp.exp(jnp.sum(lam[0:1] * lam[1:2], axis=1, keepdims=True))
                - jnp.exp(jnp.sum(lam[2:3] * lam[3:4], axis=1, keepdims=True)) + lambda_init)
    inv_l = [1.0 / jnp.sum(l_ref[mp], axis=0, keepdims=True) for mp in range(2)]
    o = acc_ref[0] * inv_l[0] - lam_full * (acc_ref[1] * inv_l[1])
    y = o * lax.rsqrt(jnp.mean(o * o, axis=0, keepdims=True) + SUBLN_EPS)
    o_ref[...] = (y * (g_ref[...] * (1.0 - lambda_init))).astype(BF16)


def _diff_attention(lam, q1, q2, k4, vt4, dtiles, cfar, subln_g, lambda_init, layer, to_cast):
    bsz, nblk, t, _ = k4.shape
    seq = nblk * t
    hw = HEAD_WIDTH
    sb = dtiles.shape[-1] // 2
    casts = _CastPlan(to_cast, bsz * DIFF_HEADS * nblk,
                      lambda b, h, i: (b * DIFF_HEADS + h) * nblk + i)
    kern = functools.partial(_attn_kernel, lambda_init=lambda_init, n_cast=len(casts))
    qspec = pl.BlockSpec((None, t, hw), lambda b, h, i: (b, i, h))
    qnext = pl.BlockSpec((None, t, hw), lambda b, h, i: (b, jnp.minimum(i + 1, nblk - 1), h))
    return pl.pallas_call(
        kern,
        grid=(bsz, DIFF_HEADS, nblk),
        in_specs=[
            _layer_spec(lam, layer // 2),
            qspec, qspec, qnext, qnext,
            pl.BlockSpec((None, nblk, t, hw), lambda b, h, i: (b, 0, 0, h)),
            pl.BlockSpec((None, nblk, hw, t), lambda b, h, i: (b, 0, h, 0)),
            pl.BlockSpec((None, 1, 2 * sb), lambda b, h, i: (h, 0, 0)),
            pl.BlockSpec((None, 1, t), lambda b, h, i: (h, 0, 0)),
            _layer_spec(subln_g, layer // 2),
        ] + casts.in_specs,
        out_specs=[pl.BlockSpec((None, hw, t), lambda b, h, i: (b, h, i))] + casts.out_specs,
        out_shape=[jax.ShapeDtypeStruct((bsz, A_QK, seq), BF16)] + casts.out_shapes,
        scratch_shapes=[pltpu.VMEM((2, t, t), F32)] * 3
        + [pltpu.VMEM((2, 1, t), F32)] * 3
        + [pltpu.VMEM((2, 1, t), F32),
           pltpu.VMEM((2, SUBLANES, t), F32),
           pltpu.VMEM((2, hw, t), F32),
           pltpu.VMEM((2, sb, sb), F32)],
        compiler_params=pltpu.CompilerParams(
            dimension_semantics=("arbitrary", "arbitrary", "arbitrary"),
            vmem_limit_bytes=VMEM_LIMIT_BYTES),
        name="diff_attention",
    )(lam, q1, q2, q1, q2, k4, vt4, dtiles, cfar, subln_g, *casts.arrays)


def _row_parts(tm):
    return [slice(lo, lo + ROW_PART) for lo in range(0, tm, ROW_PART)]


def _residual_ffn(x_ref, mixes, parts, ng_ref, wg_ref, wu_ref, wd_ref, o_ref):
    x1s, h2s, acts = [], [], []
    for p, mix in zip(parts, mixes):
        x1 = x_ref[p, :] + _rms(mix, ng_ref[1:2, :])
        x1s.append(x1)
        h2s.append(_rms(x1, ng_ref[2:3, :]).astype(BF16))
    for h2 in h2s:
        gate = jnp.dot(h2, wg_ref[...], preferred_element_type=F32)
        up = jnp.dot(h2, wu_ref[...], preferred_element_type=F32)
        acts.append((gate * jax.nn.sigmoid(gate) * up).astype(BF16))
    for p, x1, act in zip(parts, x1s, acts):
        f = jnp.dot(act, wd_ref[...], preferred_element_type=F32)
        o_ref[p, :] = x1 + _rms(f, ng_ref[3:4, :])


def _even_out_kernel(x_ref, a_ref, bo_ref, wo_ref, ng_ref, wg_ref, wu_ref, wd_ref, o_ref):
    aw = a_ref.shape[0]
    parts = _row_parts(x_ref.shape[0])
    mixes = [lax.dot_general(a_ref[:, p], wo_ref[0:aw, :], (((0,), (0,)), ((), ())),
                             preferred_element_type=F32)
             + jnp.dot(bo_ref[p, :], wo_ref[aw:, :], preferred_element_type=F32) for p in parts]
    _residual_ffn(x_ref, mixes, parts, ng_ref, wg_ref, wu_ref, wd_ref, o_ref)


def _even_out(x2d, a_t, bo2d, wo_bf, ng, wg_bf, wu_bf, wd_bf, layer):
    n, d = x2d.shape
    tm = EVEN_TILE
    per_seq = a_t.shape[2] // tm
    tok = lambda w: pl.BlockSpec((tm, w), lambda i: (i, 0))
    return pl.pallas_call(
        _even_out_kernel,
        grid=(n // tm,),
        in_specs=[tok(d), pl.BlockSpec((None, a_t.shape[1], tm), lambda i: (i // per_seq, 0, i % per_seq)),
                  tok(bo2d.shape[1]), _const_spec(wo_bf.shape),
                  _layer_spec(ng, layer), _const_spec(wg_bf.shape), _const_spec(wu_bf.shape),
                  _const_spec(wd_bf.shape)],
        out_specs=tok(d),
        out_shape=jax.ShapeDtypeStruct((n, d), F32),
        compiler_params=pltpu.CompilerParams(dimension_semantics=("arbitrary",),
                                             vmem_limit_bytes=VMEM_LIMIT_BYTES),
        name="even_out_ffn",
    )(x2d, a_t, bo2d, wo_bf, ng, wg_bf, wu_bf, wd_bf)


def _odd_kernel(x_ref, wi_ref, lng_ref, lnb_ref, ws_ref, sb_ref, wo_ref, ng_ref,
                wg_ref, wu_ref, wd_ref, o_ref):
    sw = wo_ref.shape[0]
    gd = sw // SGU_GROUPS
    parts = _row_parts(x_ref.shape[0])
    nch = (parts[0].stop - parts[0].start) // CHUNK

    def gelu(z):
        return 0.5 * z * (1.0 + lax.erf(z * math.sqrt(0.5)))

    def gate_inputs(h):
        u = gelu(jnp.dot(h, wi_ref[:, 0:sw], preferred_element_type=F32))
        v = gelu(jnp.dot(h, wi_ref[:, sw:], preferred_element_type=F32))
        mu = jnp.mean(v, axis=-1, keepdims=True)
        vc = v - mu
        v = (vc * lax.rsqrt(jnp.mean(vc * vc, axis=-1, keepdims=True) + LN_EPS) * lng_ref[...]
             + lnb_ref[...]).astype(BF16)
        return u, v

    def spatial_gate(u, v):
        tiles = [[None] * SGU_GROUPS for _ in range(nch)]
        for g in range(SGU_GROUPS):
            rhs = jnp.concatenate([v[n * CHUNK:(n + 1) * CHUNK, g * gd:(g + 1) * gd] for n in range(nch)],
                                  axis=1)
            mixed = jnp.dot(ws_ref[g], rhs, preferred_element_type=F32) + sb_ref[g]
            for n in range(nch):
                tiles[n][g] = (u[n * CHUNK:(n + 1) * CHUNK, g * gd:(g + 1) * gd]
                               * mixed[:, n * gd:(n + 1) * gd])
        gated = jnp.concatenate([jnp.concatenate(r, axis=1) for r in tiles], axis=0).astype(BF16)
        return jnp.dot(gated, wo_ref[...], preferred_element_type=F32)

    hs = [_rms(x_ref[p, :], ng_ref[0:1, :]).astype(BF16) for p in parts]
    uvs = [gate_inputs(h) for h in hs]
    mixes = [spatial_gate(u, v) for u, v in uvs]
    _residual_ffn(x_ref, mixes, parts, ng_ref, wg_ref, wu_ref, wd_ref, o_ref)


def _odd_layer(x2d, wi_bf, ln_g, ln_b, ws_bf, sb, wo_bf, ng, wg_bf, wu_bf, wd_bf, layer):
    n, d = x2d.shape
    tm = ODD_TILE
    tok = pl.BlockSpec((tm, d), lambda i: (i, 0))
    consts = (wi_bf, ln_g, ln_b, ws_bf, sb, wo_bf, ng, wg_bf, wu_bf, wd_bf)
    whole = lambda a: _const_spec(a.shape)
    odd = lambda a: _layer_spec(a, layer // 2)
    return pl.pallas_call(
        _odd_kernel,
        grid=(n // tm,),
        in_specs=[tok, whole(wi_bf), odd(ln_g), odd(ln_b), odd(ws_bf), odd(sb), whole(wo_bf),
                  _layer_spec(ng, layer), whole(wg_bf), whole(wu_bf), whole(wd_bf)],
        out_specs=tok,
        out_shape=jax.ShapeDtypeStruct((n, d), F32),
        compiler_params=pltpu.CompilerParams(dimension_semantics=("arbitrary",),
                                             vmem_limit_bytes=VMEM_LIMIT_BYTES),
        name="odd_sgu_ffn",
    )(x2d, *consts)


def _t5_bucket_of_distance(n):
    max_exact = REL_BUCKETS // 2
    nf = jnp.maximum(n, 1).astype(F32)
    large = max_exact + (jnp.log(nf / max_exact) / math.log(REL_MAX_DIST / max_exact)
                         * (REL_BUCKETS - max_exact)).astype(jnp.int32)
    large = jnp.minimum(large, REL_BUCKETS - 1)
    return jnp.where(n < max_exact, n, large)


def _bias_tiles(rel_bias, sb, t):
    assert sb >= REL_MAX_DIST
    heads = rel_bias.shape[1]
    table = rel_bias.astype(F32).T
    span = 2 * sb
    hit = _t5_bucket_of_distance(jnp.arange(span, dtype=jnp.int32))[:, None] == jnp.arange(REL_BUCKETS)
    far = table[:, REL_BUCKETS - 1:]
    by_dist = (jnp.sum(jnp.where(hit[None], table[:, None, :], 0.0), axis=-1) - far) * LOG2_E

    cfar = jnp.broadcast_to((far * LOG2_E)[:, :, None], (heads, 1, t))
    return by_dist[:, None, :], cfar


def kernel(x, rel_bias, w_in_even, diff_lambda, diff_subln_g, conv_w, w_out_even, w_in_odd,
           sgu_ln_g, sgu_ln_b, sgu_w, sgu_b, w_out_odd, norm_g, w_gate, w_up, w_down):
    bsz, seq, d = x.shape
    depth = norm_g.shape[0]
    t = ATTN_BLOCK
    assert seq % t == 0 and t % EVEN_TILE == 0 and t % BIAS_TILE == 0
    assert seq % ODD_TILE == 0 and ROW_PART % CHUNK == 0
    assert conv_w.shape[1] == CONV_WIDTH
    x2d = x.reshape(bsz * seq, d)
    dtiles, cfar = _bias_tiles(rel_bias, BIAS_TILE, t)
    tril = jnp.tril(jnp.ones((CHUNK, CHUNK), dtype=bool))
    ws = jnp.where(tril, sgu_w, 0.0).astype(BF16)
    ffn = lambda i: [(w_gate, i), (w_up, i), (w_down, i)]
    odd_bf = None
    for i in range(depth):
        if i % 2 == 0:
            lambda_init = 0.8 - 0.6 * math.exp(-0.3 * i)
            q1, q2, k, vt, bo, wo_bf, wg, wu, wd = _even_in(
                x2d, norm_g, w_in_even, conv_w, seq, i, [(w_out_even, i // 2)] + ffn(i))
            nxt = [(w_in_odd, i // 2), (w_out_odd, i // 2)] + ffn(i + 1) if i + 1 < depth else []
            a, *odd_bf = _diff_attention(
                diff_lambda, q1.reshape(bsz, seq, A_QK), q2.reshape(bsz, seq, A_QK),
                k.reshape(bsz, seq // t, t, A_QK), vt.reshape(bsz, seq // t, A_QK, t),
                dtiles, cfar, diff_subln_g[:, :, None], lambda_init, i, nxt)
            x2d = _even_out(x2d, a, bo, wo_bf, norm_g, wg, wu, wd, i)
        else:
            wi_bf, wo_bf, wg, wu, wd = odd_bf
            x2d = _odd_layer(x2d, wi_bf, sgu_ln_g[:, None, :], sgu_ln_b[:, None, :], ws,
                             sgu_b[..., None], wo_bf, norm_g, wg, wu, wd, i)
    return x2d.reshape(bsz, seq, d)
```

```python
import functools
import math

import jax
import jax.numpy as jnp
from jax import lax
from jax.experimental import pallas as pl
from jax.experimental.pallas import tpu as pltpu

F32 = jnp.float32
BF16 = jnp.bfloat16

DIFF_HEADS = 4
DIFF_QK_DIM = 64
HEAD_WIDTH = 2 * DIFF_QK_DIM
A_QK = DIFF_HEADS * HEAD_WIDTH
CONV_WIDTH = 3
SGU_GROUPS = 8
CHUNK = 128
REL_BUCKETS = 32
REL_MAX_DIST = 128
RMS_EPS = 1e-6
SUBLN_EPS = 1e-5
LN_EPS = 1e-5
MASK_VALUE = -1e30
LOG2_E = math.log2(math.e)

SUBLANES = 8
BF16_SUBLANES = 16
ROW_PART = 256
EVEN_TILE = 4 * ROW_PART
ODD_TILE = 2 * ROW_PART
ATTN_BLOCK = 1024
BIAS_TILE = 256
VMEM_LIMIT_BYTES = 56 * 1024 * 1024

_NT = (((1,), (1,)), ((), ()))


def _rms(x, g, eps=RMS_EPS):
    return x * lax.rsqrt(jnp.mean(x * x, axis=-1, keepdims=True) + eps) * g


def _const_spec(shape):
    return pl.BlockSpec(shape, lambda *_: (0,) * len(shape), pipeline_mode=pl.Buffered(1))


def _layer_spec(stacked, layer):
    rest = stacked.shape[1:]
    return pl.BlockSpec((None,) + rest, lambda *_: (layer,) + (0,) * len(rest),
                        pipeline_mode=pl.Buffered(1))


class _CastPlan:
    def __init__(self, weights, nsteps, step_of):
        self.arrays = [w for w, _ in weights]
        self.in_specs, self.out_specs, self.out_shapes = [], [], []
        for w, layer in weights:
            _, r, c = w.shape
            slab = next(s for s in range(BF16_SUBLANES, r + 1, BF16_SUBLANES)
                        if r % s == 0 and nsteps % (r // s) == 0)
            per_slab = nsteps // (r // slab)
            self.in_specs.append(pl.BlockSpec(
                (None, slab, c), lambda *g, layer=layer, per=per_slab: (layer, step_of(*g) // per, 0)))
            self.out_specs.append(pl.BlockSpec(
                (slab, c), lambda *g, per=per_slab: (step_of(*g) // per, 0)))
            self.out_shapes.append(jax.ShapeDtypeStruct((r, c), BF16))

    def __len__(self):
        return len(self.arrays)


def _cast_slabs(in_refs, out_refs):
    for src, dst in zip(in_refs, out_refs):
        dst[...] = src[...].astype(BF16)


def _even_in_kernel(x_ref, ng_ref, w_ref, cw_ref, *refs, tiles_per_seq, n_cast):
    cast_in, refs = refs[:n_cast], refs[n_cast:]
    qk_ref, vt_ref, bo_ref = refs[:3]
    cast_out, (carry_ref, wbf_ref) = refs[3:3 + n_cast], refs[3 + n_cast:]
    _cast_slabs(cast_in, cast_out)
    tm = x_ref.shape[0]
    bw = bo_ref.shape[1]

    @pl.when(pl.program_id(0) == 0)
    def _():
        wbf_ref[...] = w_ref[...].astype(BF16)

    @pl.when(pl.program_id(0) % tiles_per_seq == 0)
    def _():
        carry_ref[...] = jnp.zeros_like(carry_ref)

    parts = _row_parts(tm)
    hs = [_rms(x_ref[p, :], ng_ref[0:1, :]).astype(BF16) for p in parts]
    c0 = 3 * A_QK
    b_parts, z_parts = [], []
    for p, h in zip(parts, hs):
        proj = jnp.dot(h, wbf_ref[...], preferred_element_type=F32)
        q = proj[:, 0:A_QK] * (DIFF_QK_DIM ** -0.5 * LOG2_E)
        first_map = (lax.broadcasted_iota(jnp.int32, q.shape, 1) % HEAD_WIDTH) < DIFF_QK_DIM
        qk_ref[p, 0:A_QK] = jnp.where(first_map, q, 0.0).astype(BF16)
        qk_ref[p, A_QK:2 * A_QK] = jnp.where(first_map, 0.0, q).astype(BF16)
        qk_ref[p, 2 * A_QK:3 * A_QK] = proj[:, A_QK:2 * A_QK].astype(BF16)
        vt_ref[:, p] = proj[:, 2 * A_QK:c0].T.astype(BF16)
        b_parts.append(proj[:, c0:c0 + bw])
        z_parts.append(proj[:, c0 + bw:c0 + 2 * bw] * proj[:, c0 + 2 * bw:c0 + 3 * bw])
    b_gate = jnp.concatenate(b_parts, axis=0)
    z = jnp.concatenate(z_parts, axis=0)
    row = lax.broadcasted_iota(jnp.int32, z.shape, 0)
    prev = carry_ref[...]
    zm1 = jnp.where(row == 0, prev[SUBLANES - 1:SUBLANES], pltpu.roll(z, 1, 0))
    zm2 = jnp.where(row == 0, prev[SUBLANES - 2:SUBLANES - 1],
                    jnp.where(row == 1, prev[SUBLANES - 1:SUBLANES], pltpu.roll(z, 2, 0)))
    cw = cw_ref[...]
    y = cw[0:1] * zm2 + cw[1:2] * zm1 + cw[2:3] * z
    bo_ref[...] = (b_gate * y).astype(BF16)
    carry_ref[...] = z[tm - SUBLANES:, :]


def _even_in(x2d, norm_g, w_in, conv_w, seq, layer, to_cast):
    n, d = x2d.shape
    tm = EVEN_TILE
    t = ATTN_BLOCK
    per_blk = t // tm
    bw = conv_w.shape[-1]
    casts = _CastPlan(to_cast, n // tm, lambda i: i)
    kern = functools.partial(_even_in_kernel, tiles_per_seq=seq // tm, n_cast=len(casts))
    tok = lambda w: pl.BlockSpec((tm, w), lambda i: (i, 0))
    return pl.pallas_call(
        kern,
        grid=(n // tm,),
        in_specs=[tok(d), _layer_spec(norm_g, layer), _layer_spec(w_in, layer // 2),
                  _layer_spec(conv_w, layer // 2)] + casts.in_specs,
        out_specs=[tok(3 * A_QK),
                   pl.BlockSpec((None, A_QK, tm), lambda i: (i // per_blk, 0, i % per_blk)),
                   tok(bw)] + casts.out_specs,
        out_shape=[jax.ShapeDtypeStruct((n, 3 * A_QK), BF16),
                   jax.ShapeDtypeStruct((n // t, A_QK, t), BF16), jax.ShapeDtypeStruct((n, bw), BF16)]
        + casts.out_shapes,
        scratch_shapes=[pltpu.VMEM((SUBLANES, bw), F32), pltpu.VMEM(w_in.shape[1:], BF16)],
        compiler_params=pltpu.CompilerParams(dimension_semantics=("arbitrary",),
                                             vmem_limit_bytes=VMEM_LIMIT_BYTES),
        name="even_in_proj",
    )(x2d, norm_g, w_in, conv_w, *casts.arrays)


_DIAG, _PREV = 0, 1


def _patched(s, tile_ref, patches):
    sb = tile_ref.shape[-1]
    for r, c, idx, scale in patches:
        band = s[r:r + sb, :]
        cols = [band[:, :c]] if c else []
        tile = tile_ref[idx] if scale is None else tile_ref[idx] * scale
        cols.append(band[:, c:c + sb] + tile)
        if c + sb < s.shape[1]:
            cols.append(band[:, c + sb:])
        rows = [s[:r]] if r else []
        rows.append(jnp.concatenate(cols, axis=1) if len(cols) > 1 else cols[0])
        if r + sb < s.shape[0]:
            rows.append(s[r + sb:])
        s = jnp.concatenate(rows, axis=0) if len(rows) > 1 else rows[0]
    return s


def _fold8(x, op):
    return op(x.reshape(x.shape[0] // SUBLANES, SUBLANES, x.shape[1]), axis=0)


def _lpad(x, width, value):
    if not width:
        return x
    return jnp.concatenate([jnp.full((x.shape[0], width), value, x.dtype), x], axis=1)


def _attn_kernel(lam_ref, q1_ref, q2_ref, q1n_ref, q2n_ref, k_ref, vt_ref, dt_ref, cfar_ref, g_ref,
                 *refs, lambda_init, n_cast):
    cast_in, o_ref, cast_out = refs[:n_cast], refs[n_cast], refs[n_cast + 1:2 * n_cast + 1]
    sa_ref, sb_ref, sd_ref, ca_ref, cb_ref, cd_ref, m_ref, l_ref, acc_ref = refs[2 * n_cast + 1:]
    _cast_slabs(cast_in, cast_out)
    i = pl.program_id(2)
    nblk = k_ref.shape[0]
    t = q1_ref.shape[0]
    sb = dt_ref.shape[-1]
    ns = t // sb
    q_refs = (q1_ref, q2_ref)
    qn_refs = (q1n_ref, q2n_ref)
    cfar = cfar_ref[...]
    rows = [slice(a * sb, (a + 1) * sb) for a in range(ns)]
    hw = vt_ref.shape[1]

    def value_rows(blk, a):
        return jnp.concatenate([vt_ref[blk, :, rows[a]], jnp.ones((BF16_SUBLANES, sb), BF16)], axis=0)

    def fill_piece(blk, a, mp, s_ref, cmax):
        s = lax.dot_general(k_ref[blk, rows[a], :], q_refs[mp][...], _NT, preferred_element_type=F32)
        if a == ns - 1:
            is_prev = jnp.where(blk == i - 1, 1.0, 0.0).astype(F32)
            s = _patched(s, dt_ref, [(0, 0, _PREV, is_prev)])
        s_ref[mp, rows[a], :] = s
        pm = _fold8(s, jnp.max)
        return pm if cmax is None else jnp.maximum(cmax, pm)

    def past_producer(blk, s_ref, c_ref):
        cmax = [None, None]

        def piece(a, mp):
            cmax[mp] = fill_piece(blk, a, mp, s_ref, cmax[mp])

        def finish():
            for mp in range(2):
                c_ref[mp] = jnp.max(cmax[mp], axis=0, keepdims=True)
        return piece, finish

    def diag_strip(qr, blk, a, mp):
        s = lax.dot_general(k_ref[blk, rows[a], :], qr[mp][a * sb:, :], _NT,
                            preferred_element_type=F32)
        patches = [(0, 0, _DIAG, None)] + ([(0, sb, _PREV, None)] if a + 1 < ns else [])
        return _patched(s, dt_ref, patches)

    def strip_max(s, a, smax):
        pm = _lpad(jnp.max(s, axis=0, keepdims=True), a * sb, MASK_VALUE)
        return pm if smax is None else jnp.maximum(smax, pm)

    def next_diag_producer():
        blk = jnp.minimum(i + 1, nblk - 1)
        smax = [None, None]

        def piece(a, mp):
            s = diag_strip(qn_refs, blk, a, mp)
            sd_ref[mp, rows[a], a * sb:] = s
            smax[mp] = strip_max(s, a, smax[mp])

        def finish():
            for mp in range(2):
                cd_ref[mp] = smax[mp]
        return piece, finish

    def past_step(blk, cur_s, cur_c, producer):
        piece, finish = producer
        shift, alpha = [], []
        for mp in range(2):
            m_old = m_ref[mp]
            m_new = jnp.maximum(m_old, cur_c[mp] + cfar)
            m_ref[mp] = m_new
            shift.append(m_new - cfar)
            alpha.append(jnp.exp2(m_old - m_new))
        pv = [None, None]
        for a in range(ns):
            for mp in range(2):
                e = jnp.exp2(cur_s[mp, rows[a], :] - shift[mp])
                piece(a, mp)
                d = jnp.dot(value_rows(blk, a), e.astype(BF16), preferred_element_type=F32)
                pv[mp] = d if pv[mp] is None else pv[mp] + d
        finish()
        for mp in range(2):
            acc_ref[mp] = alpha[mp] * acc_ref[mp] + pv[mp][:hw]
            l_ref[mp] = alpha[mp] * l_ref[mp] + pv[mp][hw:hw + SUBLANES] * (1.0 / SUBLANES)

    def diag_step(strip, smax, producer):
        piece, finish = producer
        lsum = [None, None]
        for a in range(ns):
            q_lo = a * sb
            for mp in range(2):
                e = jnp.exp2(strip(mp, a) - smax[mp][:, q_lo:])
                piece(a, mp)
                d = jnp.dot(value_rows(i, a), e.astype(BF16), preferred_element_type=F32)
                ps = _lpad(d[hw:hw + SUBLANES], q_lo, 0.0)
                lsum[mp] = ps if lsum[mp] is None else lsum[mp] + ps
                if a == 0:
                    acc_ref[mp] = d[:hw]
                else:
                    acc_ref[mp, :, q_lo:] += d[:hw]
        finish()
        for mp in range(2):
            m_ref[mp] = smax[mp] + cfar
            l_ref[mp] = lsum[mp] * (1.0 / SUBLANES)

    @pl.when(i == 0)
    def _():
        strips, smax = [[None] * ns, [None] * ns], [None, None]
        for a in range(ns):
            for mp in range(2):
                strips[mp][a] = diag_strip(q_refs, i, a, mp)
                smax[mp] = strip_max(strips[mp][a], a, smax[mp])
        diag_step(lambda mp, a: strips[mp][a], smax, next_diag_producer())

    @pl.when(i > 0)
    def _():
        diag_step(lambda mp, a: sd_ref[mp, rows[a], a * sb:], [cd_ref[0], cd_ref[1]],
                  past_producer(0, sa_ref, ca_ref))

    def pair_body(p, carry):
        past_step(2 * p, sa_ref, ca_ref, past_producer(2 * p + 1, sb_ref, cb_ref))
        past_step(2 * p + 1, sb_ref, cb_ref, past_producer(2 * p + 2, sa_ref, ca_ref))
        return carry

    lax.fori_loop(0, jnp.maximum(i - 1, 0) // 2, pair_body, 0)

    @pl.when(jnp.logical_and(i >= 2, i % 2 == 0))
    def _():
        past_step(i - 2, sa_ref, ca_ref, past_producer(i - 1, sb_ref, cb_ref))
        past_step(i - 1, sb_ref, cb_ref, next_diag_producer())

    @pl.when(i % 2 == 1)
    def _():
        past_step(i - 1, sa_ref, ca_ref, next_diag_producer())

    lam = lam_ref[...]
    lam_full = (jnp.exp(jnp.sum(lam[0:1] * lam[1:2], axis=1, keepdims=True))
                - jnp.exp(jnp.sum(lam[2:3] * lam[3:4], axis=1, keepdims=True)) + lambda_init)
    inv_l = [1.0 / jnp.sum(l_ref[mp], axis=0, keepdims=True) for mp in range(2)]
    o = acc_ref[0] * inv_l[0] - lam_full * (acc_ref[1] * inv_l[1])
    y = o * lax.rsqrt(jnp.mean(o * o, axis=0, keepdims=True) + SUBLN_EPS)
    o_ref[...] = (y * (g_ref[...] * (1.0 - lambda_init))).astype(BF16)


def _diff_attention(lam, qk3, qk4, vt4, dtiles, cfar, subln_g, lambda_init, layer, to_cast):
    bsz, nblk, t, _ = qk4.shape
    seq = nblk * t
    hw = HEAD_WIDTH
    sb = dtiles.shape[-1]
    casts = _CastPlan(to_cast, bsz * DIFF_HEADS * nblk,
                      lambda b, h, i: (b * DIFF_HEADS + h) * nblk + i)
    kern = functools.partial(_attn_kernel, lambda_init=lambda_init, n_cast=len(casts))
    nh = DIFF_HEADS
    qspec = lambda m: pl.BlockSpec((None, t, hw), lambda b, h, i: (b, i, m * nh + h))
    qnext = lambda m: pl.BlockSpec(
        (None, t, hw), lambda b, h, i: (b, jnp.minimum(i + 1, nblk - 1), m * nh + h))
    return pl.pallas_call(
        kern,
        grid=(bsz, DIFF_HEADS, nblk),
        in_specs=[
            _layer_spec(lam, layer // 2),
            qspec(0), qspec(1), qnext(0), qnext(1),
            pl.BlockSpec((None, nblk, t, hw), lambda b, h, i: (b, 0, 0, 2 * nh + h)),
            pl.BlockSpec((None, nblk, hw, t), lambda b, h, i: (b, 0, h, 0)),
            pl.BlockSpec((None, 2, sb, sb), lambda b, h, i: (h, 0, 0, 0)),
            pl.BlockSpec((None, 1, t), lambda b, h, i: (h, 0, 0)),
            _layer_spec(subln_g, layer // 2),
        ] + casts.in_specs,
        out_specs=[pl.BlockSpec((None, hw, t), lambda b, h, i: (b, h, i))] + casts.out_specs,
        out_shape=[jax.ShapeDtypeStruct((bsz, A_QK, seq), BF16)] + casts.out_shapes,
        scratch_shapes=[pltpu.VMEM((2, t, t), F32)] * 3
        + [pltpu.VMEM((2, 1, t), F32)] * 3
        + [pltpu.VMEM((2, 1, t), F32),
           pltpu.VMEM((2, SUBLANES, t), F32),
           pltpu.VMEM((2, hw, t), F32)],
        compiler_params=pltpu.CompilerParams(
            dimension_semantics=("arbitrary", "arbitrary", "arbitrary"),
            vmem_limit_bytes=VMEM_LIMIT_BYTES),
        name="diff_attention",
    )(lam, qk3, qk3, qk3, qk3, qk4, vt4, dtiles, cfar, subln_g, *casts.arrays)


def _row_parts(tm):
    return [slice(lo, lo + ROW_PART) for lo in range(0, tm, ROW_PART)]


def _residual_ffn(x_ref, mixes, parts, ng_ref, wg_ref, wu_ref, wd_ref, o_ref):
    x1s, h2s, acts = [], [], []
    for p, mix in zip(parts, mixes):
        x1 = x_ref[p, :] + _rms(mix, ng_ref[1:2, :])
        x1s.append(x1)
        h2s.append(_rms(x1, ng_ref[2:3, :]).astype(BF16))
    for h2 in h2s:
        gate = jnp.dot(h2, wg_ref[...], preferred_element_type=F32)
        up = jnp.dot(h2, wu_ref[...], preferred_element_type=F32)
        acts.append((gate * jax.nn.sigmoid(gate) * up).astype(BF16))
    for p, x1, act in zip(parts, x1s, acts):
        f = jnp.dot(act, wd_ref[...], preferred_element_type=F32)
        o_ref[p, :] = x1 + _rms(f, ng_ref[3:4, :])


def _even_out_kernel(x_ref, a_ref, bo_ref, wo_ref, ng_ref, wg_ref, wu_ref, wd_ref, o_ref):
    aw = a_ref.shape[0]
    parts = _row_parts(x_ref.shape[0])
    mixes = [lax.dot_general(a_ref[:, p], wo_ref[0:aw, :], (((0,), (0,)), ((), ())),
                             preferred_element_type=F32)
             + jnp.dot(bo_ref[p, :], wo_ref[aw:, :], preferred_element_type=F32) for p in parts]
    _residual_ffn(x_ref, mixes, parts, ng_ref, wg_ref, wu_ref, wd_ref, o_ref)


def _even_out(x2d, a_t, bo2d, wo_bf, ng, wg_bf, wu_bf, wd_bf, layer):
    n, d = x2d.shape
    tm = EVEN_TILE
    per_seq = a_t.shape[2] // tm
    tok = lambda w: pl.BlockSpec((tm, w), lambda i: (i, 0))
    return pl.pallas_call(
        _even_out_kernel,
        grid=(n // tm,),
        in_specs=[tok(d), pl.BlockSpec((None, a_t.shape[1], tm), lambda i: (i // per_seq, 0, i % per_seq)),
                  tok(bo2d.shape[1]), _const_spec(wo_bf.shape),
                  _layer_spec(ng, layer), _const_spec(wg_bf.shape), _const_spec(wu_bf.shape),
                  _const_spec(wd_bf.shape)],
        out_specs=tok(d),
        out_shape=jax.ShapeDtypeStruct((n, d), F32),
        compiler_params=pltpu.CompilerParams(dimension_semantics=("arbitrary",),
                                             vmem_limit_bytes=VMEM_LIMIT_BYTES),
        name="even_out_ffn",
    )(x2d, a_t, bo2d, wo_bf, ng, wg_bf, wu_bf, wd_bf)


def _odd_kernel(x_ref, wi_ref, lng_ref, lnb_ref, ws_ref, sb_ref, wo_ref, ng_ref,
                wg_ref, wu_ref, wd_ref, o_ref):
    sw = wo_ref.shape[0]
    gd = sw // SGU_GROUPS
    parts = _row_parts(x_ref.shape[0])
    nch = (parts[0].stop - parts[0].start) // CHUNK

    def gelu(z):
        return 0.5 * z * (1.0 + lax.erf(z * math.sqrt(0.5)))

    def gate_inputs(h):
        u = gelu(jnp.dot(h, wi_ref[:, 0:sw], preferred_element_type=F32))
        v = gelu(jnp.dot(h, wi_ref[:, sw:], preferred_element_type=F32))
        mu = jnp.mean(v, axis=-1, keepdims=True)
        vc = v - mu
        v = (vc * lax.rsqrt(jnp.mean(vc * vc, axis=-1, keepdims=True) + LN_EPS) * lng_ref[...]
             + lnb_ref[...]).astype(BF16)
        return u, v

    def spatial_gate(u, v):
        tiles = [[None] * SGU_GROUPS for _ in range(nch)]
        for g in range(SGU_GROUPS):
            rhs = jnp.concatenate([v[n * CHUNK:(n + 1) * CHUNK, g * gd:(g + 1) * gd] for n in range(nch)],
                                  axis=1)
            mixed = jnp.dot(ws_ref[g], rhs, preferred_element_type=F32) + sb_ref[g]
            for n in range(nch):
                tiles[n][g] = (u[n * CHUNK:(n + 1) * CHUNK, g * gd:(g + 1) * gd]
                               * mixed[:, n * gd:(n + 1) * gd])
        gated = jnp.concatenate([jnp.concatenate(r, axis=1) for r in tiles], axis=0).astype(BF16)
        return jnp.dot(gated, wo_ref[...], preferred_element_type=F32)

    hs = [_rms(x_ref[p, :], ng_ref[0:1, :]).astype(BF16) for p in parts]
    uvs = [gate_inputs(h) for h in hs]
    mixes = [spatial_gate(u, v) for u, v in uvs]
    _residual_ffn(x_ref, mixes, parts, ng_ref, wg_ref, wu_ref, wd_ref, o_ref)


def _odd_layer(x2d, wi_bf, ln_g, ln_b, ws_bf, sb, wo_bf, ng, wg_bf, wu_bf, wd_bf, layer):
    n, d = x2d.shape
    tm = ODD_TILE
    tok = pl.BlockSpec((tm, d), lambda i: (i, 0))
    consts = (wi_bf, ln_g, ln_b, ws_bf, sb, wo_bf, ng, wg_bf, wu_bf, wd_bf)
    whole = lambda a: _const_spec(a.shape)
    odd = lambda a: _layer_spec(a, layer // 2)
    return pl.pallas_call(
        _odd_kernel,
        grid=(n // tm,),
        in_specs=[tok, whole(wi_bf), odd(ln_g), odd(ln_b), odd(ws_bf), odd(sb), whole(wo_bf),
                  _layer_spec(ng, layer), whole(wg_bf), whole(wu_bf), whole(wd_bf)],
        out_specs=tok,
        out_shape=jax.ShapeDtypeStruct((n, d), F32),
        compiler_params=pltpu.CompilerParams(dimension_semantics=("arbitrary",),
                                             vmem_limit_bytes=VMEM_LIMIT_BYTES),
        name="odd_sgu_ffn",
    )(x2d, *consts)


def _t5_bucket_of_distance(n):
    max_exact = REL_BUCKETS // 2
    nf = jnp.maximum(n, 1).astype(F32)
    large = max_exact + (jnp.log(nf / max_exact) / math.log(REL_MAX_DIST / max_exact)
                         * (REL_BUCKETS - max_exact)).astype(jnp.int32)
    large = jnp.minimum(large, REL_BUCKETS - 1)
    return jnp.where(n < max_exact, n, large)


def _bias_tiles(rel_bias, sb, t):
    assert sb >= REL_MAX_DIST
    heads = rel_bias.shape[1]
    table = rel_bias.astype(F32).T
    span = 2 * sb
    hit = _t5_bucket_of_distance(jnp.arange(span, dtype=jnp.int32))[:, None] == jnp.arange(REL_BUCKETS)
    far = table[:, REL_BUCKETS - 1:]
    by_dist = (jnp.sum(jnp.where(hit[None], table[:, None, :], 0.0), axis=-1) - far) * LOG2_E

    def toeplitz(v):
        rows = jnp.broadcast_to(v[:, None, :], (heads, sb, span))
        skew = jnp.pad(rows, ((0, 0), (0, 0), (0, 1))).reshape(heads, sb * (span + 1))
        return skew[:, :sb * span].reshape(heads, sb, span)[:, :, sb:]

    prev = toeplitz(by_dist)
    diag = toeplitz(jnp.concatenate(
        [jnp.full((heads, sb), MASK_VALUE, F32), by_dist[:, :sb]], axis=1))
    tiles = jnp.stack([diag, prev], axis=1)
    cfar = jnp.broadcast_to((far * LOG2_E)[:, :, None], (heads, 1, t))
    return tiles, cfar


def kernel(x, rel_bias, w_in_even, diff_lambda, diff_subln_g, conv_w, w_out_even, w_in_odd,
           sgu_ln_g, sgu_ln_b, sgu_w, sgu_b, w_out_odd, norm_g, w_gate, w_up, w_down):
    bsz, seq, d = x.shape
    depth = norm_g.shape[0]
    t = ATTN_BLOCK
    assert seq % t == 0 and t % EVEN_TILE == 0 and t % BIAS_TILE == 0
    assert seq % ODD_TILE == 0 and ROW_PART % CHUNK == 0
    assert conv_w.shape[1] == CONV_WIDTH
    x2d = x.reshape(bsz * seq, d)
    dtiles, cfar = _bias_tiles(rel_bias, BIAS_TILE, t)
    tril = jnp.tril(jnp.ones((CHUNK, CHUNK), dtype=bool))
    ws = jnp.where(tril, sgu_w, 0.0).astype(BF16)
    ffn = lambda i: [(w_gate, i), (w_up, i), (w_down, i)]
    odd_bf = None
    for i in range(depth):
        if i % 2 == 0:
            lambda_init = 0.8 - 0.6 * math.exp(-0.3 * i)
            qk, vt, bo, wo_bf, wg, wu, wd = _even_in(
                x2d, norm_g, w_in_even, conv_w, seq, i, [(w_out_even, i // 2)] + ffn(i))
            nxt = [(w_in_odd, i // 2), (w_out_odd, i // 2)] + ffn(i + 1) if i + 1 < depth else []
            a, *odd_bf = _diff_attention(
                diff_lambda, qk.reshape(bsz, seq, 3 * A_QK), qk.reshape(bsz, seq // t, t, 3 * A_QK),
                vt.reshape(bsz, seq // t, A_QK, t),
                dtiles, cfar, diff_subln_g[:, :, None], lambda_init, i, nxt)
            x2d = _even_out(x2d, a, bo, wo_bf, norm_g, wg, wu, wd, i)
        else:
            wi_bf, wo_bf, wg, wu, wd = odd_bf
            x2d = _odd_layer(x2d, wi_bf, sgu_ln_g[:, None, :], sgu_ln_b[:, None, :], ws,
                             sgu_b[..., None], wo_bf, norm_g, wg, wu, wd, i)
    return x2d.reshape(bsz, seq, d)
```

```python
import functools
import math

import jax
import jax.numpy as jnp
from jax import lax
from jax.experimental import pallas as pl
from jax.experimental.pallas import tpu as pltpu

F32 = jnp.float32
BF16 = jnp.bfloat16

DIFF_HEADS = 4
DIFF_QK_DIM = 64
HEAD_WIDTH = 2 * DIFF_QK_DIM
A_QK = DIFF_HEADS * HEAD_WIDTH
CONV_WIDTH = 3
SGU_GROUPS = 8
CHUNK = 128
REL_BUCKETS = 32
REL_MAX_DIST = 128
RMS_EPS = 1e-6
SUBLN_EPS = 1e-5
LN_EPS = 1e-5
MASK_VALUE = -1e30
LOG2_E = math.log2(math.e)

SUBLANES = 8
BF16_SUBLANES = 16
ROW_PART = 256
EVEN_TILE = 4 * ROW_PART
ODD_TILE = 2 * ROW_PART
ATTN_BLOCK = 1024
BIAS_TILE = 256
VMEM_LIMIT_BYTES = 56 * 1024 * 1024

_NT = (((1,), (1,)), ((), ()))


def _rms(x, g, eps=RMS_EPS):
    return x * lax.rsqrt(jnp.mean(x * x, axis=-1, keepdims=True) + eps) * g


def _const_spec(shape):
    return pl.BlockSpec(shape, lambda *_: (0,) * len(shape), pipeline_mode=pl.Buffered(1))


def _layer_spec(stacked, layer):
    rest = stacked.shape[1:]
    return pl.BlockSpec((None,) + rest, lambda *_: (layer,) + (0,) * len(rest),
                        pipeline_mode=pl.Buffered(1))


class _CastPlan:
    def __init__(self, weights, nsteps, step_of):
        self.arrays = [w for w, _ in weights]
        self.in_specs, self.out_specs, self.out_shapes = [], [], []
        for w, layer in weights:
            _, r, c = w.shape
            slab = next(s for s in range(BF16_SUBLANES, r + 1, BF16_SUBLANES)
                        if r % s == 0 and nsteps % (r // s) == 0)
            per_slab = nsteps // (r // slab)
            self.in_specs.append(pl.BlockSpec(
                (None, slab, c), lambda *g, layer=layer, per=per_slab: (layer, step_of(*g) // per, 0)))
            self.out_specs.append(pl.BlockSpec(
                (slab, c), lambda *g, per=per_slab: (step_of(*g) // per, 0)))
            self.out_shapes.append(jax.ShapeDtypeStruct((r, c), BF16))

    def __len__(self):
        return len(self.arrays)


def _cast_slabs(in_refs, out_refs):
    for src, dst in zip(in_refs, out_refs):
        dst[...] = src[...].astype(BF16)


def _even_in_kernel(x_ref, ng_ref, w_ref, cw_ref, *refs, tiles_per_seq, n_cast):
    cast_in, refs = refs[:n_cast], refs[n_cast:]
    q1_ref, q2_ref, k_ref, vt_ref, bo_ref = refs[:5]
    cast_out, (carry_ref, wbf_ref) = refs[5:5 + n_cast], refs[5 + n_cast:]
    _cast_slabs(cast_in, cast_out)
    tm = x_ref.shape[0]
    bw = bo_ref.shape[1]

    @pl.when(pl.program_id(0) == 0)
    def _():
        wbf_ref[...] = w_ref[...].astype(BF16)

    @pl.when(pl.program_id(0) % tiles_per_seq == 0)
    def _():
        carry_ref[...] = jnp.zeros_like(carry_ref)

    parts = _row_parts(tm)
    hs = [_rms(x_ref[p, :], ng_ref[0:1, :]).astype(BF16) for p in parts]
    c0 = 3 * A_QK
    b_parts, z_parts = [], []
    for p, h in zip(parts, hs):
        proj = jnp.dot(h, wbf_ref[...], preferred_element_type=F32)
        q = proj[:, 0:A_QK] * (DIFF_QK_DIM ** -0.5 * LOG2_E)
        first_map = (lax.broadcasted_iota(jnp.int32, q.shape, 1) % HEAD_WIDTH) < DIFF_QK_DIM
        q1_ref[p, :] = jnp.where(first_map, q, 0.0).astype(BF16)
        q2_ref[p, :] = jnp.where(first_map, 0.0, q).astype(BF16)
        k_ref[p, :] = proj[:, A_QK:2 * A_QK].astype(BF16)
        vt_ref[:, p] = proj[:, 2 * A_QK:c0].T.astype(BF16)
        b_parts.append(proj[:, c0:c0 + bw])
        z_parts.append(proj[:, c0 + bw:c0 + 2 * bw] * proj[:, c0 + 2 * bw:c0 + 3 * bw])
    b_gate = jnp.concatenate(b_parts, axis=0)
    z = jnp.concatenate(z_parts, axis=0)
    row = lax.broadcasted_iota(jnp.int32, z.shape, 0)
    prev = carry_ref[...]
    zm1 = jnp.where(row == 0, prev[SUBLANES - 1:SUBLANES], pltpu.roll(z, 1, 0))
    zm2 = jnp.where(row == 0, prev[SUBLANES - 2:SUBLANES - 1],
                    jnp.where(row == 1, prev[SUBLANES - 1:SUBLANES], pltpu.roll(z, 2, 0)))
    cw = cw_ref[...]
    y = cw[0:1] * zm2 + cw[1:2] * zm1 + cw[2:3] * z
    bo_ref[...] = (b_gate * y).astype(BF16)
    carry_ref[...] = z[tm - SUBLANES:, :]


def _even_in(x2d, norm_g, w_in, conv_w, seq, layer, to_cast):
    n, d = x2d.shape
    tm = EVEN_TILE
    t = ATTN_BLOCK
    per_blk = t // tm
    bw = conv_w.shape[-1]
    casts = _CastPlan(to_cast, n // tm, lambda i: i)
    kern = functools.partial(_even_in_kernel, tiles_per_seq=seq // tm, n_cast=len(casts))
    tok = lambda w: pl.BlockSpec((tm, w), lambda i: (i, 0))
    return pl.pallas_call(
        kern,
        grid=(n // tm,),
        in_specs=[tok(d), _layer_spec(norm_g, layer), _layer_spec(w_in, layer // 2),
                  _layer_spec(conv_w, layer // 2)] + casts.in_specs,
        out_specs=[tok(A_QK), tok(A_QK), tok(A_QK),
                   pl.BlockSpec((None, A_QK, tm), lambda i: (i // per_blk, 0, i % per_blk)),
                   tok(bw)] + casts.out_specs,
        out_shape=[jax.ShapeDtypeStruct((n, A_QK), BF16)] * 3
        + [jax.ShapeDtypeStruct((n // t, A_QK, t), BF16), jax.ShapeDtypeStruct((n, bw), BF16)]
        + casts.out_shapes,
        scratch_shapes=[pltpu.VMEM((SUBLANES, bw), F32), pltpu.VMEM(w_in.shape[1:], BF16)],
        compiler_params=pltpu.CompilerParams(dimension_semantics=("arbitrary",),
                                             vmem_limit_bytes=VMEM_LIMIT_BYTES),
        name="even_in_proj",
    )(x2d, norm_g, w_in, conv_w, *casts.arrays)


_DIAG, _PREV = 0, 1


def _patched(s, tile_ref, patches):
    sb = tile_ref.shape[-1]
    for r, c, idx, scale in patches:
        band = s[r:r + sb, :]
        cols = [band[:, :c]] if c else []
        tile = tile_ref[idx] if scale is None else tile_ref[idx] * scale
        cols.append(band[:, c:c + sb] + tile)
        if c + sb < s.shape[1]:
            cols.append(band[:, c + sb:])
        rows = [s[:r]] if r else []
        rows.append(jnp.concatenate(cols, axis=1) if len(cols) > 1 else cols[0])
        if r + sb < s.shape[0]:
            rows.append(s[r + sb:])
        s = jnp.concatenate(rows, axis=0) if len(rows) > 1 else rows[0]
    return s


def _fold8(x, op):
    return op(x.reshape(x.shape[0] // SUBLANES, SUBLANES, x.shape[1]), axis=0)


def _lpad(x, width, value):
    if not width:
        return x
    return jnp.concatenate([jnp.full((x.shape[0], width), value, x.dtype), x], axis=1)


def _attn_kernel(lam_ref, q1_ref, q2_ref, q1n_ref, q2n_ref, k_ref, vt_ref, dt_ref, cfar_ref, g_ref,
                 *refs, lambda_init, n_cast):
    cast_in, o_ref, cast_out = refs[:n_cast], refs[n_cast], refs[n_cast + 1:2 * n_cast + 1]
    sa_ref, sb_ref, sd_ref, ca_ref, cb_ref, cd_ref, m_ref, l_ref, acc_ref = refs[2 * n_cast + 1:]
    _cast_slabs(cast_in, cast_out)
    i = pl.program_id(2)
    nblk = k_ref.shape[0]
    t = q1_ref.shape[0]
    sb = dt_ref.shape[-1]
    ns = t // sb
    q_refs = (q1_ref, q2_ref)
    qn_refs = (q1n_ref, q2n_ref)
    cfar = cfar_ref[...]
    rows = [slice(a * sb, (a + 1) * sb) for a in range(ns)]
    hw = vt_ref.shape[1]

    def value_rows(blk, a):
        return jnp.concatenate([vt_ref[blk, :, rows[a]], jnp.ones((BF16_SUBLANES, sb), BF16)], axis=0)

    def fill_piece(blk, a, mp, s_ref, cmax):
        s = lax.dot_general(k_ref[blk, rows[a], :], q_refs[mp][...], _NT, preferred_element_type=F32)
        if a == ns - 1:
            is_prev = jnp.where(blk == i - 1, 1.0, 0.0).astype(F32)
            s = _patched(s, dt_ref, [(0, 0, _PREV, is_prev)])
        s_ref[mp, rows[a], :] = s
        pm = _fold8(s, jnp.max)
        return pm if cmax is None else jnp.maximum(cmax, pm)

    def past_producer(blk, s_ref, c_ref):
        cmax = [None, None]

        def piece(a, mp):
            cmax[mp] = fill_piece(blk, a, mp, s_ref, cmax[mp])

        def finish():
            for mp in range(2):
                c_ref[mp] = jnp.max(cmax[mp], axis=0, keepdims=True)
        return piece, finish

    def diag_strip(qr, blk, a, mp):
        s = lax.dot_general(k_ref[blk, rows[a], :], qr[mp][a * sb:, :], _NT,
                            preferred_element_type=F32)
        patches = [(0, 0, _DIAG, None)] + ([(0, sb, _PREV, None)] if a + 1 < ns else [])
        return _patched(s, dt_ref, patches)

    def strip_max(s, a, smax):
        pm = _lpad(jnp.max(s, axis=0, keepdims=True), a * sb, MASK_VALUE)
        return pm if smax is None else jnp.maximum(smax, pm)

    def next_diag_producer():
        blk = jnp.minimum(i + 1, nblk - 1)
        smax = [None, None]

        def piece(a, mp):
            s = diag_strip(qn_refs, blk, a, mp)
            sd_ref[mp, rows[a], a * sb:] = s
            smax[mp] = strip_max(s, a, smax[mp])

        def finish():
            for mp in range(2):
                cd_ref[mp] = smax[mp]
        return piece, finish

    def past_step(blk, cur_s, cur_c, producer):
        piece, finish = producer
        shift, alpha = [], []
        for mp in range(2):
            m_old = m_ref[mp]
            m_new = jnp.maximum(m_old, cur_c[mp] + cfar)
            m_ref[mp] = m_new
            shift.append(m_new - cfar)
            alpha.append(jnp.exp2(m_old - m_new))
        pv = [None, None]
        for a in range(ns):
            for mp in range(2):
                e = jnp.exp2(cur_s[mp, rows[a], :] - shift[mp])
                piece(a, mp)
                d = jnp.dot(value_rows(blk, a), e.astype(BF16), preferred_element_type=F32)
                pv[mp] = d if pv[mp] is None else pv[mp] + d
        finish()
        for mp in range(2):
            acc_ref[mp] = alpha[mp] * acc_ref[mp] + pv[mp][:hw]
            l_ref[mp] = alpha[mp] * l_ref[mp] + pv[mp][hw:hw + SUBLANES] * (1.0 / SUBLANES)

    def diag_step(strip, smax, producer):
        piece, finish = producer
        lsum = [None, None]
        for a in range(ns):
            q_lo = a * sb
            for mp in range(2):
                e = jnp.exp2(strip(mp, a) - smax[mp][:, q_lo:])
                piece(a, mp)
                d = jnp.dot(value_rows(i, a), e.astype(BF16), preferred_element_type=F32)
                ps = _lpad(d[hw:hw + SUBLANES], q_lo, 0.0)
                lsum[mp] = ps if lsum[mp] is None else lsum[mp] + ps
                if a == 0:
                    acc_ref[mp] = d[:hw]
                else:
                    acc_ref[mp, :, q_lo:] += d[:hw]
        finish()
        for mp in range(2):
            m_ref[mp] = smax[mp] + cfar
            l_ref[mp] = lsum[mp] * (1.0 / SUBLANES)

    @pl.when(i == 0)
    def _():
        strips, smax = [[None] * ns, [None] * ns], [None, None]
        for a in range(ns):
            for mp in range(2):
                strips[mp][a] = diag_strip(q_refs, i, a, mp)
                smax[mp] = strip_max(strips[mp][a], a, smax[mp])
        diag_step(lambda mp, a: strips[mp][a], smax, next_diag_producer())

    @pl.when(i > 0)
    def _():
        diag_step(lambda mp, a: sd_ref[mp, rows[a], a * sb:], [cd_ref[0], cd_ref[1]],
                  past_producer(0, sa_ref, ca_ref))

    def pair_body(p, carry):
        past_step(2 * p, sa_ref, ca_ref, past_producer(2 * p + 1, sb_ref, cb_ref))
        past_step(2 * p + 1, sb_ref, cb_ref, past_producer(2 * p + 2, sa_ref, ca_ref))
        return carry

    lax.fori_loop(0, jnp.maximum(i - 1, 0) // 2, pair_body, 0)

    @pl.when(jnp.logical_and(i >= 2, i % 2 == 0))
    def _():
        past_step(i - 2, sa_ref, ca_ref, past_producer(i - 1, sb_ref, cb_ref))
        past_step(i - 1, sb_ref, cb_ref, next_diag_producer())

    @pl.when(i % 2 == 1)
    def _():
        past_step(i - 1, sa_ref, ca_ref, next_diag_producer())

    lam = lam_ref[...]
    lam_full = (jnp.exp(jnp.sum(lam[0:1] * lam[1:2], axis=1, keepdims=True))
                - jnp.exp(jnp.sum(lam[2:3] * lam[3:4], axis=1, keepdims=True)) + lambda_init)
    inv_l = [1.0 / jnp.sum(l_ref[mp], axis=0, keepdims=True) for mp in range(2)]
    o = acc_ref[0] * inv_l[0] - lam_full * (acc_ref[1] * inv_l[1])
    y = o * lax.rsqrt(jnp.mean(o * o, axis=0, keepdims=True) + SUBLN_EPS)
    o_ref[...] = (y * (g_ref[...] * (1.0 - lambda_init))).astype(BF16)


def _diff_attention(lam, q1, q2, k4, vt4, dtiles, cfar, subln_g, lambda_init, layer, to_cast):
    bsz, nblk, t, _ = k4.shape
    seq = nblk * t
    hw = HEAD_WIDTH
    sb = dtiles.shape[-1]
    casts = _CastPlan(to_cast, bsz * DIFF_HEADS * nblk,
                      lambda b, h, i: (b * DIFF_HEADS + h) * nblk + i)
    kern = functools.partial(_attn_kernel, lambda_init=lambda_init, n_cast=len(casts))
    qspec = pl.BlockSpec((None, t, hw), lambda b, h, i: (b, i, h))
    qnext = pl.BlockSpec((None, t, hw), lambda b, h, i: (b, jnp.minimum(i + 1, nblk - 1), h))
    return pl.pallas_call(
        kern,
        grid=(bsz, DIFF_HEADS, nblk),
        in_specs=[
            _layer_spec(lam, layer // 2),
            qspec, qspec, qnext, qnext,
            pl.BlockSpec((None, nblk, t, hw), lambda b, h, i: (b, 0, 0, h)),
            pl.BlockSpec((None, nblk, hw, t), lambda b, h, i: (b, 0, h, 0)),
            pl.BlockSpec((None, 2, sb, sb), lambda b, h, i: (h, 0, 0, 0)),
            pl.BlockSpec((None, 1, t), lambda b, h, i: (h, 0, 0)),
            _layer_spec(subln_g, layer // 2),
        ] + casts.in_specs,
        out_specs=[pl.BlockSpec((None, hw, t), lambda b, h, i: (b, h, i))] + casts.out_specs,
        out_shape=[jax.ShapeDtypeStruct((bsz, A_QK, seq), BF16)] + casts.out_shapes,
        scratch_shapes=[pltpu.VMEM((2, t, t), F32)] * 3
        + [pltpu.VMEM((2, 1, t), F32)] * 3
        + [pltpu.VMEM((2, 1, t), F32),
           pltpu.VMEM((2, SUBLANES, t), F32),
           pltpu.VMEM((2, hw, t), F32)],
        compiler_params=pltpu.CompilerParams(
            dimension_semantics=("arbitrary", "arbitrary", "arbitrary"),
            vmem_limit_bytes=VMEM_LIMIT_BYTES),
        name="diff_attention",
    )(lam, q1, q2, q1, q2, k4, vt4, dtiles, cfar, subln_g, *casts.arrays)


def _row_parts(tm):
    return [slice(lo, lo + ROW_PART) for lo in range(0, tm, ROW_PART)]


def _residual_ffn(x_ref, mixes, parts, ng_ref, wg_ref, wu_ref, wd_ref, o_ref):
    x1s, h2s, acts = [], [], []
    for p, mix in zip(parts, mixes):
        x1 = x_ref[p, :] + _rms(mix, ng_ref[1:2, :])
        x1s.append(x1)
        h2s.append(_rms(x1, ng_ref[2:3, :]).astype(BF16))
    for h2 in h2s:
        gate = jnp.dot(h2, wg_ref[...], preferred_element_type=F32)
        up = jnp.dot(h2, wu_ref[...], preferred_element_type=F32)
        acts.append((gate * jax.nn.sigmoid(gate) * up).astype(BF16))
    for p, x1, act in zip(parts, x1s, acts):
        f = jnp.dot(act, wd_ref[...], preferred_element_type=F32)
        o_ref[p, :] = x1 + _rms(f, ng_ref[3:4, :])


def _even_out_kernel(x_ref, a_ref, bo_ref, wo_ref, ng_ref, wg_ref, wu_ref, wd_ref, o_ref):
    aw = a_ref.shape[0]
    parts = _row_parts(x_ref.shape[0])
    mixes = [lax.dot_general(a_ref[:, p], wo_ref[0:aw, :], (((0,), (0,)), ((), ())),
                             preferred_element_type=F32)
             + jnp.dot(bo_ref[p, :], wo_ref[aw:, :], preferred_element_type=F32) for p in parts]
    _residual_ffn(x_ref, mixes, parts, ng_ref, wg_ref, wu_ref, wd_ref, o_ref)


def _even_out(x2d, a_t, bo2d, wo_bf, ng, wg_bf, wu_bf, wd_bf, layer):
    n, d = x2d.shape
    tm = EVEN_TILE
    per_seq = a_t.shape[2] // tm
    tok = lambda w: pl.BlockSpec((tm, w), lambda i: (i, 0))
    return pl.pallas_call(
        _even_out_kernel,
        grid=(n // tm,),
        in_specs=[tok(d), pl.BlockSpec((None, a_t.shape[1], tm), lambda i: (i // per_seq, 0, i % per_seq)),
                  tok(bo2d.shape[1]), _const_spec(wo_bf.shape),
                  _layer_spec(ng, layer), _const_spec(wg_bf.shape), _const_spec(wu_bf.shape),
                  _const_spec(wd_bf.shape)],
        out_specs=tok(d),
        out_shape=jax.ShapeDtypeStruct((n, d), F32),
        compiler_params=pltpu.CompilerParams(dimension_semantics=("parallel",),
                                             vmem_limit_bytes=VMEM_LIMIT_BYTES),
        name="even_out_ffn",
    )(x2d, a_t, bo2d, wo_bf, ng, wg_bf, wu_bf, wd_bf)


def _odd_kernel(x_ref, wi_ref, lng_ref, lnb_ref, ws_ref, sb_ref, wo_ref, ng_ref,
                wg_ref, wu_ref, wd_ref, o_ref):
    sw = wo_ref.shape[0]
    gd = sw // SGU_GROUPS
    parts = _row_parts(x_ref.shape[0])
    nch = (parts[0].stop - parts[0].start) // CHUNK

    def gelu(z):
        return 0.5 * z * (1.0 + lax.erf(z * math.sqrt(0.5)))

    def gate_inputs(h):
        u = gelu(jnp.dot(h, wi_ref[:, 0:sw], preferred_element_type=F32))
        v = gelu(jnp.dot(h, wi_ref[:, sw:], preferred_element_type=F32))
        mu = jnp.mean(v, axis=-1, keepdims=True)
        vc = v - mu
        v = (vc * lax.rsqrt(jnp.mean(vc * vc, axis=-1, keepdims=True) + LN_EPS) * lng_ref[...]
             + lnb_ref[...]).astype(BF16)
        return u, v

    def spatial_gate(u, v):
        tiles = [[None] * SGU_GROUPS for _ in range(nch)]
        for g in range(SGU_GROUPS):
            rhs = jnp.concatenate([v[n * CHUNK:(n + 1) * CHUNK, g * gd:(g + 1) * gd] for n in range(nch)],
                                  axis=1)
            mixed = jnp.dot(ws_ref[g], rhs, preferred_element_type=F32) + sb_ref[g]
            for n in range(nch):
                tiles[n][g] = (u[n * CHUNK:(n + 1) * CHUNK, g * gd:(g + 1) * gd]
                               * mixed[:, n * gd:(n + 1) * gd])
        gated = jnp.concatenate([jnp.concatenate(r, axis=1) for r in tiles], axis=0).astype(BF16)
        return jnp.dot(gated, wo_ref[...], preferred_element_type=F32)

    hs = [_rms(x_ref[p, :], ng_ref[0:1, :]).astype(BF16) for p in parts]
    uvs = [gate_inputs(h) for h in hs]
    mixes = [spatial_gate(u, v) for u, v in uvs]
    _residual_ffn(x_ref, mixes, parts, ng_ref, wg_ref, wu_ref, wd_ref, o_ref)


def _odd_layer(x2d, wi_bf, ln_g, ln_b, ws_bf, sb, wo_bf, ng, wg_bf, wu_bf, wd_bf, layer):
    n, d = x2d.shape
    tm = ODD_TILE
    tok = pl.BlockSpec((tm, d), lambda i: (i, 0))
    consts = (wi_bf, ln_g, ln_b, ws_bf, sb, wo_bf, ng, wg_bf, wu_bf, wd_bf)
    whole = lambda a: _const_spec(a.shape)
    odd = lambda a: _layer_spec(a, layer // 2)
    return pl.pallas_call(
        _odd_kernel,
        grid=(n // tm,),
        in_specs=[tok, whole(wi_bf), odd(ln_g), odd(ln_b), odd(ws_bf), odd(sb), whole(wo_bf),
                  _layer_spec(ng, layer), whole(wg_bf), whole(wu_bf), whole(wd_bf)],
        out_specs=tok,
        out_shape=jax.ShapeDtypeStruct((n, d), F32),
        compiler_params=pltpu.CompilerParams(dimension_semantics=("parallel",),
                                             vmem_limit_bytes=VMEM_LIMIT_BYTES),
        name="odd_sgu_ffn",
    )(x2d, *consts)


def _t5_bucket_of_distance(n):
    max_exact = REL_BUCKETS // 2
    nf = jnp.maximum(n, 1).astype(F32)
    large = max_exact + (jnp.log(nf / max_exact) / math.log(REL_MAX_DIST / max_exact)
                         * (REL_BUCKETS - max_exact)).astype(jnp.int32)
    large = jnp.minimum(large, REL_BUCKETS - 1)
    return jnp.where(n < max_exact, n, large)


def _bias_tiles(rel_bias, sb, t):
    assert sb >= REL_MAX_DIST
    heads = rel_bias.shape[1]
    table = rel_bias.astype(F32).T
    span = 2 * sb
    hit = _t5_bucket_of_distance(jnp.arange(span, dtype=jnp.int32))[:, None] == jnp.arange(REL_BUCKETS)
    far = table[:, REL_BUCKETS - 1:]
    by_dist = (jnp.sum(jnp.where(hit[None], table[:, None, :], 0.0), axis=-1) - far) * LOG2_E

    def toeplitz(v):
        rows = jnp.broadcast_to(v[:, None, :], (heads, sb, span))
        skew = jnp.pad(rows, ((0, 0), (0, 0), (0, 1))).reshape(heads, sb * (span + 1))
        return skew[:, :sb * span].reshape(heads, sb, span)[:, :, sb:]

    prev = toeplitz(by_dist)
    diag = toeplitz(jnp.concatenate(
        [jnp.full((heads, sb), MASK_VALUE, F32), by_dist[:, :sb]], axis=1))
    tiles = jnp.stack([diag, prev], axis=1)
    cfar = jnp.broadcast_to((far * LOG2_E)[:, :, None], (heads, 1, t))
    return tiles, cfar


def kernel(x, rel_bias, w_in_even, diff_lambda, diff_subln_g, conv_w, w_out_even, w_in_odd,
           sgu_ln_g, sgu_ln_b, sgu_w, sgu_b, w_out_odd, norm_g, w_gate, w_up, w_down):
    bsz, seq, d = x.shape
    depth = norm_g.shape[0]
    t = ATTN_BLOCK
    assert seq % t == 0 and t % EVEN_TILE == 0 and t % BIAS_TILE == 0
    assert seq % ODD_TILE == 0 and ROW_PART % CHUNK == 0
    assert conv_w.shape[1] == CONV_WIDTH
    x2d = x.reshape(bsz * seq, d)
    dtiles, cfar = _bias_tiles(rel_bias, BIAS_TILE, t)
    tril = jnp.tril(jnp.ones((CHUNK, CHUNK), dtype=bool))
    ws = jnp.where(tril, sgu_w, 0.0).astype(BF16)
    ffn = lambda i: [(w_gate, i), (w_up, i), (w_down, i)]
    odd_bf = None
    for i in range(depth):
        if i % 2 == 0:
            lambda_init = 0.8 - 0.6 * math.exp(-0.3 * i)
            q1, q2, k, vt, bo, wo_bf, wg, wu, wd = _even_in(
                x2d, norm_g, w_in_even, conv_w, seq, i, [(w_out_even, i // 2)] + ffn(i))
            nxt = [(w_in_odd, i // 2), (w_out_odd, i // 2)] + ffn(i + 1) if i + 1 < depth else []
            a, *odd_bf = _diff_attention(
                diff_lambda, q1.reshape(bsz, seq, A_QK), q2.reshape(bsz, seq, A_QK),
                k.reshape(bsz, seq // t, t, A_QK), vt.reshape(bsz, seq // t, A_QK, t),
                dtiles, cfar, diff_subln_g[:, :, None], lambda_init, i, nxt)
            x2d = _even_out(x2d, a, bo, wo_bf, norm_g, wg, wu, wd, i)
        else:
            wi_bf, wo_bf, wg, wu, wd = odd_bf
            x2d = _odd_layer(x2d, wi_bf, sgu_ln_g[:, None, :], sgu_ln_b[:, None, :], ws,
                             sgu_b[..., None], wo_bf, norm_g, wg, wu, wd, i)
    return x2d.reshape(bsz, seq, d)
```
